```python
import math, functools
import jax, jax.numpy as jnp
from jax import lax
import numpy as np

D_MODEL = 1024
BATCH = 8
SEQ = 8192
DEPTH = 1
DEC_BATCH = 128
DEC_SEQ = 4
PAST_LEN = 8192
PAGE_SIZE = 128

ATT_HEADS = 8
ATT_HEAD_DIM = 64
ATT_WIDTH = ATT_HEADS * ATT_HEAD_DIM
IDX_HEADS = 8
IDX_DIM = 64
TOPK_MAX = 256
Q_BLOCK = 128
GLA_HEADS = 4
GLA_DK = 64
GLA_DV = 128
GLA_KEY_WIDTH = GLA_HEADS * GLA_DK
GLA_WIDTH = GLA_HEADS * GLA_DV
GLA_GATE_RANK = 16
GLA_GATE_TAU = 16.0
GLA_CHUNK = 16
MIX_WIDTH = ATT_WIDTH + GLA_WIDTH
REL_BUCKETS = 32
REL_MAX_DIST = 128
D_FF = -(-8 * D_MODEL // (3 * 256)) * 256
PLE_DIM = 256
RMS_EPS = 1e-6
NEG_INF = -1e30
PROJ_SPLITS = (ATT_WIDTH, ATT_WIDTH, ATT_WIDTH, IDX_HEADS * IDX_DIM, IDX_DIM, IDX_HEADS,
               GLA_KEY_WIDTH, GLA_KEY_WIDTH, GLA_WIDTH, GLA_GATE_RANK, GLA_WIDTH)
PROJ_WIDTH = sum(PROJ_SPLITS)

kernel_name = 'hymba_dsa_gla_decoder_step'


def rmsnorm(x, g):
    xf = x.astype(jnp.float32)
    y = xf * lax.rsqrt(jnp.mean(xf * xf, axis=-1, keepdims=True) + RMS_EPS)
    return (y * g.astype(jnp.float32)).astype(x.dtype)


def split_cols(z):
    parts = []
    off = 0
    for w in PROJ_SPLITS:
        parts.append(z[..., off:off + w])
        off += w
    return parts


def rel_bucket(n):
    n = jnp.maximum(n, 0)
    exact = REL_BUCKETS // 2
    nf = jnp.maximum(n, 1).astype(jnp.float32)
    large = exact + (jnp.log(nf / exact) / math.log(REL_MAX_DIST / exact)
                     * (REL_BUCKETS - exact)).astype(jnp.int32)
    large = jnp.minimum(large, REL_BUCKETS - 1)
    return jnp.where(n < exact, n, large)


def indexer_scores(q_idx, w_idx, k_idx, q_pos, k_pos):
    s = jnp.einsum('bqhd,bld->bqhl', q_idx, k_idx).astype(jnp.float32) * (IDX_DIM ** -0.5)
    i_score = jnp.einsum('bqhl,bqh->bql', jax.nn.relu(s), w_idx.astype(jnp.float32)) * (IDX_HEADS ** -0.5)
    allowed = k_pos[None, :] <= q_pos[:, None]
    return jnp.where(allowed[None], i_score, NEG_INF)


def sparse_attend(q, k_sel, v_sel, q_pos, sel, rel_bias):
    B, Q, H, dh = q.shape
    logits = jnp.einsum('bqhd,bqkhd->bqhk', q, k_sel).astype(jnp.float32) * (dh ** -0.5)
    dist = q_pos[None, :, None] - sel
    bias = rel_bias[rel_bucket(dist)].astype(jnp.float32)
    logits = logits + jnp.moveaxis(bias, -1, 2)
    logits = jnp.where((dist >= 0)[:, :, None, :], logits, NEG_INF)
    probs = jax.nn.softmax(logits, axis=-1)
    out = jnp.einsum('bqhk,bqkhd->bqhd', probs.astype(v_sel.dtype), v_sel)
    return out.reshape(B, Q, H * dh)


def prompt_attention(q, k, v, q_idx, w_idx, k_idx, rel_bias):
    B, S = q.shape[:2]
    top_k = min(TOPK_MAX, S // 4)
    qb = min(Q_BLOCK, S)
    nb = S // qb
    k_pos = jnp.arange(S, dtype=jnp.int32)

    def blocks(t):
        return jnp.moveaxis(t.reshape((B, nb, qb) + t.shape[2:]), 1, 0)

    def one_block(xs):
        q_b, qi_b, wi_b, start = xs
        q_pos = start + jnp.arange(qb, dtype=jnp.int32)
        scores = indexer_scores(qi_b, wi_b, k_idx, q_pos, k_pos)
        sel = lax.top_k(scores, top_k)[1]
        k_sel = jax.vmap(lambda kb, ib: kb[ib])(k, sel)
        v_sel = jax.vmap(lambda vb, ib: vb[ib])(v, sel)
        return sparse_attend(q_b, k_sel, v_sel, q_pos, sel, rel_bias)

    starts = jnp.arange(nb, dtype=jnp.int32) * qb
    out = lax.map(one_block, (blocks(q), blocks(q_idx), blocks(w_idx), starts))
    return jnp.moveaxis(out, 0, 1).reshape(B, S, ATT_WIDTH)


def sample_attention(q, k_new, v_new, q_idx, w_idx, k_idx_new, cache_k, cache_v, cache_idx_k,
                     page_table, rel_bias, layer):
    DB, T = q.shape[:2]
    past = page_table.shape[1] * PAGE_SIZE
    top_k = min(TOPK_MAX, (past + T) // 4)
    past_idx = cache_idx_k[layer, page_table].reshape(DB, past, IDX_DIM)
    k_idx_all = jnp.concatenate([past_idx, k_idx_new.astype(past_idx.dtype)], axis=1)
    q_pos = past + jnp.arange(T, dtype=jnp.int32)
    k_pos = jnp.arange(past + T, dtype=jnp.int32)
    scores = indexer_scores(q_idx, w_idx, k_idx_all, q_pos, k_pos)
    sel = lax.top_k(scores, top_k)[1]
    in_past = sel < past
    ps = jnp.minimum(sel, past - 1)
    phys = jnp.take_along_axis(page_table, (ps // PAGE_SIZE).reshape(DB, -1), axis=1).reshape(ps.shape)
    off = ps % PAGE_SIZE
    ns = jnp.clip(sel - past, 0, T - 1)
    k_cur = jax.vmap(lambda kb, ib: kb[ib])(k_new, ns)
    v_cur = jax.vmap(lambda vb, ib: vb[ib])(v_new, ns)
    k_sel = jnp.where(in_past[..., None, None], cache_k[layer, phys, off], k_cur)
    v_sel = jnp.where(in_past[..., None, None], cache_v[layer, phys, off], v_cur)
    return sparse_attend(q, k_sel, v_sel, q_pos, sel, rel_bias)


def gla_chunked(q, k, v, log_a, s0):
    B, T, H, dk = q.shape
    dv = v.shape[-1]
    C = min(GLA_CHUNK, T)
    pad = (-T) % C
    n = (T + pad) // C

    def prep(t):
        t = jnp.pad(t.astype(jnp.float32), ((0, 0), (0, pad), (0, 0), (0, 0)))
        return t.reshape(B, n, C, H, t.shape[-1]).transpose(1, 0, 3, 2, 4)

    q, k, v, la = prep(q), prep(k), prep(v), prep(log_a)
    b = jnp.cumsum(la, axis=3)
    b_last = b[:, :, :, -1:, :]
    qt = q * jnp.exp(b)
    kt = k * jnp.exp(-b)
    kd = k * jnp.exp(b_last - b)
    causal = jnp.tril(jnp.ones((C, C), dtype=bool))
    attn = jnp.where(causal, jnp.einsum('nbhcd,nbhsd->nbhcs', qt, kt), 0.0)
    o_intra = jnp.einsum('nbhcs,nbhsv->nbhcv', attn, v)

    def step(S, xs):
        qt_i, kd_i, v_i, dec_i = xs
        o_i = jnp.einsum('bhcd,bhdv->bhcv', qt_i, S)
        S = dec_i[..., :, None] * S + jnp.einsum('bhcd,bhcv->bhdv', kd_i, v_i)
        return S, o_i

    S, o_inter = lax.scan(step, s0.astype(jnp.float32), (qt, kd, v, jnp.exp(b_last[:, :, :, 0, :])))
    o = (o_intra + o_inter).transpose(1, 0, 3, 2, 4).reshape(B, n * C, H, dv)[:, :T]
    return o, S


def gla_branch(gq, gk, gv, glr, gout, s0, w_gate_up, b_gate, g_gla_out):
    B, T = gq.shape[:2]
    log_a = jax.nn.log_sigmoid((glr @ w_gate_up + b_gate).astype(jnp.float32)) / GLA_GATE_TAU
    q = gq.reshape(B, T, GLA_HEADS, GLA_DK) * (GLA_DK ** -0.5)
    k = gk.reshape(B, T, GLA_HEADS, GLA_DK)
    v = gv.reshape(B, T, GLA_HEADS, GLA_DV)
    o, s_new = gla_chunked(q, k, v, log_a.reshape(B, T, GLA_HEADS, GLA_DK), s0)
    o = rmsnorm(o, g_gla_out)
    o = o.reshape(B, T, GLA_WIDTH).astype(gout.dtype) * jax.nn.silu(gout)
    return o, s_new


def run_layer(x, p_i, attn_fn, gla_s0, g_pre_mix, w_in, w_gate_up, b_gate, g_gla_out, w_out,
              g_post_mix, g_pre_ffn, w_ff_gate, w_ff_up, w_ff_down, g_post_ffn, w_ple_gate, w_ple_proj):
    B, T, _ = x.shape
    a = rmsnorm(x, g_pre_mix)
    aq, ak, av, iq, ik, iw, gq, gk, gv, glr, gout = split_cols(a @ w_in)
    q = aq.reshape(B, T, ATT_HEADS, ATT_HEAD_DIM)
    k = ak.reshape(B, T, ATT_HEADS, ATT_HEAD_DIM)
    v = av.reshape(B, T, ATT_HEADS, ATT_HEAD_DIM)
    q_idx = iq.reshape(B, T, IDX_HEADS, IDX_DIM)
    att = attn_fn(q, k, v, q_idx, iw, ik)
    gla, s_new = gla_branch(gq, gk, gv, glr, gout, gla_s0, w_gate_up, b_gate, g_gla_out)
    mixed = jnp.concatenate([att, gla.astype(att.dtype)], axis=-1) @ w_out
    h = x + rmsnorm(mixed, g_post_mix)
    f = rmsnorm(h, g_pre_ffn)
    ff = (jax.nn.silu(f @ w_ff_gate) * (f @ w_ff_up)) @ w_ff_down
    h = h + rmsnorm(ff, g_post_ffn)
    h = h + jax.nn.sigmoid(h @ w_ple_gate) * (p_i @ w_ple_proj)
    return h, k, v, ik, s_new


def setup_inputs(seed: int = 0) -> dict:
    key = jax.random.key(seed)
    ks = jax.random.split(key, 24)
    f32 = jnp.float32

    def nrm(k, shape, scale):
        return jax.random.normal(k, shape, f32) * scale

    n_pages = PAST_LEN // PAGE_SIZE
    n_used = DEC_BATCH * n_pages
    n_pool = n_used + n_used // 4
    page_table = jax.random.permutation(ks[8], n_pool)[:n_used].reshape(DEC_BATCH, n_pages).astype(jnp.int32)
    return {
        'x_prompt': nrm(ks[0], (BATCH, SEQ, D_MODEL), 1.0),
        'x_sample': nrm(ks[1], (DEC_BATCH, DEC_SEQ, D_MODEL), 1.0),
        'p_prompt': nrm(ks[2], (DEPTH, BATCH, SEQ, PLE_DIM), 1.0),
        'p_sample': nrm(ks[3], (DEPTH, DEC_BATCH, DEC_SEQ, PLE_DIM), 1.0),
        'cache_k': nrm(ks[4], (DEPTH, n_pool, PAGE_SIZE, ATT_HEADS, ATT_HEAD_DIM), 1.0),
        'cache_v': nrm(ks[5], (DEPTH, n_pool, PAGE_SIZE, ATT_HEADS, ATT_HEAD_DIM), 1.0),
        'cache_idx_k': nrm(ks[6], (DEPTH, n_pool, PAGE_SIZE, IDX_DIM), 1.0),
        'state_gla': nrm(ks[7], (DEPTH, DEC_BATCH, GLA_HEADS, GLA_DK, GLA_DV), 0.5),
        'page_table': page_table,
        'rel_bias': nrm(ks[9], (REL_BUCKETS, ATT_HEADS), 0.5),
        'g_pre_mix': 1.0 + nrm(ks[10], (DEPTH, D_MODEL), 0.05),
        'w_in': nrm(ks[11], (DEPTH, D_MODEL, PROJ_WIDTH), D_MODEL ** -0.5),
        'w_gate_up': nrm(ks[12], (DEPTH, GLA_GATE_RANK, GLA_KEY_WIDTH), GLA_GATE_RANK ** -0.5),
        'b_gate': nrm(ks[13], (DEPTH, GLA_KEY_WIDTH), 0.1),
        'g_gla_out': 1.0 + nrm(ks[14], (DEPTH, GLA_DV), 0.05),
        'w_out': nrm(ks[15], (DEPTH, MIX_WIDTH, D_MODEL), MIX_WIDTH ** -0.5),
        'g_post_mix': 1.0 + nrm(ks[16], (DEPTH, D_MODEL), 0.05),
        'g_pre_ffn': 1.0 + nrm(ks[17], (DEPTH, D_MODEL), 0.05),
        'w_ff_gate': nrm(ks[18], (DEPTH, D_MODEL, D_FF), D_MODEL ** -0.5),
        'w_ff_up': nrm(ks[19], (DEPTH, D_MODEL, D_FF), D_MODEL ** -0.5),
        'w_ff_down': nrm(ks[20], (DEPTH, D_FF, D_MODEL), D_FF ** -0.5),
        'g_post_ffn': 1.0 + nrm(ks[21], (DEPTH, D_MODEL), 0.05),
        'w_ple_gate': nrm(ks[22], (DEPTH, D_MODEL, D_MODEL), D_MODEL ** -0.5),
        'w_ple_proj': nrm(ks[23], (DEPTH, PLE_DIM, D_MODEL), PLE_DIM ** -0.5),
    }


def reference(x_prompt, x_sample, p_prompt, p_sample, cache_k, cache_v, cache_idx_k, state_gla,
              page_table, rel_bias, g_pre_mix, w_in, w_gate_up, b_gate, g_gla_out, w_out,
              g_post_mix, g_pre_ffn, w_ff_gate, w_ff_up, w_ff_down, g_post_ffn, w_ple_gate, w_ple_proj):
    h_p = x_prompt
    h_s = x_sample
    kp_l, vp_l, ikp_l, sp_l = [], [], [], []
    ks_l, vs_l, iks_l, ss_l = [], [], [], []
    s0_prompt = jnp.zeros((x_prompt.shape[0], GLA_HEADS, GLA_DK, GLA_DV), jnp.float32)
    prompt_attn = functools.partial(prompt_attention, rel_bias=rel_bias)
    for i in range(DEPTH):
        layer_w = (g_pre_mix[i], w_in[i], w_gate_up[i], b_gate[i], g_gla_out[i], w_out[i],
                   g_post_mix[i], g_pre_ffn[i], w_ff_gate[i], w_ff_up[i], w_ff_down[i],
                   g_post_ffn[i], w_ple_gate[i], w_ple_proj[i])
        h_p, kp, vp, ikp, sp = run_layer(h_p, p_prompt[i], prompt_attn, s0_prompt, *layer_w)
        sample_attn = functools.partial(sample_attention, cache_k=cache_k, cache_v=cache_v,
                                        cache_idx_k=cache_idx_k, page_table=page_table,
                                        rel_bias=rel_bias, layer=i)
        h_s, ksm, vsm, iks, ss = run_layer(h_s, p_sample[i], sample_attn, state_gla[i], *layer_w)
        kp_l.append(kp); vp_l.append(vp); ikp_l.append(ikp); sp_l.append(sp)
        ks_l.append(ksm); vs_l.append(vsm); iks_l.append(iks); ss_l.append(ss)
    return (h_p, h_s,
            jnp.stack(kp_l), jnp.stack(vp_l), jnp.stack(ikp_l), jnp.stack(sp_l),
            jnp.stack(ks_l), jnp.stack(vs_l), jnp.stack(iks_l), jnp.stack(ss_l))
```

```python
import functools
import math

import jax
import jax.numpy as jnp
from jax import lax
from jax.experimental import pallas as pl
from jax.experimental.pallas import tpu as pltpu

F32 = jnp.float32
BF16 = jnp.bfloat16

D_MODEL = 1024
ATT_HEADS = 8
ATT_HEAD_DIM = 64
ATT_WIDTH = ATT_HEADS * ATT_HEAD_DIM
IDX_HEADS = 8
IDX_DIM = 64
TOPK_MAX = 256
PAGE_SIZE = 128
GLA_HEADS = 4
GLA_DK = 64
GLA_DV = 128
GLA_KEY_WIDTH = GLA_HEADS * GLA_DK
GLA_WIDTH = GLA_HEADS * GLA_DV
GLA_GATE_RANK = 16
GLA_GATE_TAU = 16.0
REL_BUCKETS = 32
REL_MAX_DIST = 128
RMS_EPS = 1e-6

LANE = 128
Q_TILE = 128
K_CHUNK = 256
HEAD_GROUP = 256 // ATT_HEAD_DIM
VMEM_LIMIT = 56 * 1024 * 1024

SEG_Q = (0, 512)
SEG_K = (512, 1024)
SEG_V = (1024, 1536)
SEG_IQ = (1536, 2048)
SEG_IKW = (2048, 2176)
SEG_GLA = (2176, 3840)
PROJ_PACKED = 3840
GLA_IN_WIDTH = SEG_GLA[1] - SEG_GLA[0]

NT_DIMS = (((1,), (1,)), ((), ()))
TN_DIMS = (((0,), (0,)), ((), ()))


def _rms(x, g):
    return x * lax.rsqrt(jnp.mean(x * x, axis=-1, keepdims=True) + RMS_EPS) * g


def _inproj_body(x_ref, g_ref, w_ref, q_ref, k_ref, v_ref, kb_ref, vb_ref, iq_ref, ikw_ref,
                 ikb_ref, gla_ref):
    a = _rms(x_ref[...], g_ref[...]).astype(BF16)

    def proj(seg):
        return jnp.dot(a, w_ref[:, seg[0]:seg[1]], preferred_element_type=F32)

    q_ref[...] = proj(SEG_Q).astype(BF16)
    kf = proj(SEG_K)
    k_ref[...] = kf
    kb_ref[...] = kf.astype(BF16)
    vf = proj(SEG_V)
    v_ref[...] = vf
    vb_ref[...] = vf.astype(BF16)
    iq_ref[...] = proj(SEG_IQ).astype(BF16)
    lane = lax.broadcasted_iota(jnp.int32, (1, LANE), 1)
    is_w = (lane >= IDX_DIM) & (lane < IDX_DIM + IDX_HEADS)
    ikw = proj(SEG_IKW) * jnp.where(is_w, IDX_HEADS ** -0.5, 1.0)
    ikw_ref[...] = ikw
    ikb_ref[...] = ikw.astype(BF16)
    gla_ref[...] = proj(SEG_GLA)


def _inproj(x, g, w_packed, tm):
    t = x.shape[0]
    row = lambda w: pl.BlockSpec((tm, w), lambda i: (i, 0))
    const = lambda s: pl.BlockSpec(s, lambda i: (0, 0))
    out_shape = (
        jax.ShapeDtypeStruct((t, ATT_WIDTH), BF16),
        jax.ShapeDtypeStruct((t, ATT_WIDTH), F32),
        jax.ShapeDtypeStruct((t, ATT_WIDTH), F32),
        jax.ShapeDtypeStruct((t, ATT_WIDTH), BF16),
        jax.ShapeDtypeStruct((t, ATT_WIDTH), BF16),
        jax.ShapeDtypeStruct((t, ATT_WIDTH), BF16),
        jax.ShapeDtypeStruct((t, LANE), F32),
        jax.ShapeDtypeStruct((t, LANE), BF16),
        jax.ShapeDtypeStruct((t, GLA_IN_WIDTH), F32),
    )
    return pl.pallas_call(
        _inproj_body,
        grid=(t // tm,),
        in_specs=[row(D_MODEL), const((1, D_MODEL)), const((D_MODEL, PROJ_PACKED))],
        out_specs=(row(ATT_WIDTH), row(ATT_WIDTH), row(ATT_WIDTH), row(ATT_WIDTH), row(ATT_WIDTH),
                   row(ATT_WIDTH), row(LANE), row(LANE), row(GLA_IN_WIDTH)),
        out_shape=out_shape,
        compiler_params=pltpu.CompilerParams(dimension_semantics=("arbitrary",),
                                             vmem_limit_bytes=VMEM_LIMIT),
        name="inproj",
    )(x, g, w_packed)


def _log_sigmoid(z):
    return jnp.minimum(z, 0.0) - jnp.log1p(jnp.exp(-jnp.abs(z)))


def _gla_body(gin_ref, s0_ref, wgu_ref, bg_ref, gg_ref, o_ref, sout_ref, st_ref, *, chunk, n_valid):
    @pl.when(pl.program_id(1) == 0)
    def _():
        st_ref[...] = s0_ref[0]

    n_chunks = gin_ref.shape[1] // chunk
    ri = lax.broadcasted_iota(jnp.int32, (chunk, chunk), 0)
    ci = lax.broadcasted_iota(jnp.int32, (chunk, chunk), 1)
    causal = ri >= ci
    tri = causal.astype(F32)
    row_valid = lax.broadcasted_iota(jnp.int32, (chunk, 1), 0) < n_valid

    def one_chunk(c, carry):
        r0 = pl.multiple_of(c * chunk, chunk)
        blk = gin_ref[0, pl.ds(r0, chunk), :]
        gq = blk[:, 0:256]
        gk = blk[:, 256:512]
        gv = blk[:, 512:1024]
        glr = blk[:, 1024:1152]
        gout = blk[:, 1152:1664]
        z = jnp.dot(glr.astype(BF16), wgu_ref[...], preferred_element_type=F32) + bg_ref[...]
        la = _log_sigmoid(z) / GLA_GATE_TAU
        if n_valid < chunk:
            la = jnp.where(row_valid, la, 0.0)
        b = jnp.dot(tri, la, precision=lax.Precision.HIGHEST, preferred_element_type=F32)
        b_last = b[chunk - 1:chunk, :]
        qt = gq * jnp.exp(b)
        kt = gk * jnp.exp(-b)
        kd = gk * jnp.exp(b_last - b)
        dec = jnp.exp(b_last)
        st = st_ref[...]
        o_parts = []
        st_parts = []
        for h in range(GLA_HEADS):
            ks = slice(GLA_DK * h, GLA_DK * (h + 1))
            vs = slice(GLA_DV * h, GLA_DV * (h + 1))
            qh = qt[:, ks].astype(BF16)
            kh = kt[:, ks].astype(BF16)
            kdh = kd[:, ks].astype(BF16)
            vh = gv[:, vs].astype(BF16)
            sth = st[:, ks]
            attn = lax.dot_general(qh, kh, NT_DIMS, preferred_element_type=F32)
            attn = jnp.where(causal, attn, 0.0)
            o = jnp.dot(attn.astype(BF16), vh, preferred_element_type=F32)
            o = o + lax.dot_general(qh, sth.astype(BF16), NT_DIMS, preferred_element_type=F32)
            upd = lax.dot_general(vh, kdh, TN_DIMS, preferred_element_type=F32)
            st_parts.append(dec[:, ks] * sth + upd)
            o_parts.append(_rms(o, gg_ref[...]))
        st_ref[...] = jnp.concatenate(st_parts, axis=1)
        o_all = jnp.concatenate(o_parts, axis=1)
        gate = gout * jax.nn.sigmoid(gout)
        o_ref[0, pl.ds(r0, chunk), :] = (o_all * gate).astype(BF16)
        return carry

    lax.fori_loop(0, n_chunks, one_chunk, 0)
    sout_ref[0] = st_ref[...]


def _gla(gin, s0_t, wgu, bg, gg, *, tile, chunk, n_valid):
    nseq, t, _ = gin.shape
    body = functools.partial(_gla_body, chunk=chunk, n_valid=n_valid)
    return pl.pallas_call(
        body,
        grid=(nseq, t // tile),
        in_specs=[
            pl.BlockSpec((1, tile, GLA_IN_WIDTH), lambda b, j: (b, j, 0)),
            pl.BlockSpec((1, GLA_DV, GLA_KEY_WIDTH), lambda b, j: (b, 0, 0)),
            pl.BlockSpec((LANE, GLA_KEY_WIDTH), lambda b, j: (0, 0)),
            pl.BlockSpec((1, GLA_KEY_WIDTH), lambda b, j: (0, 0)),
            pl.BlockSpec((1, GLA_DV), lambda b, j: (0, 0)),
        ],
        out_specs=(
            pl.BlockSpec((1, tile, GLA_WIDTH), lambda b, j: (b, j, 0)),
            pl.BlockSpec((1, GLA_DV, GLA_KEY_WIDTH), lambda b, j: (b, 0, 0)),
        ),
        out_shape=(
            jax.ShapeDtypeStruct((nseq, t, GLA_WIDTH), BF16),
            jax.ShapeDtypeStruct((nseq, GLA_DV, GLA_KEY_WIDTH), F32),
        ),
        scratch_shapes=[pltpu.VMEM((GLA_DV, GLA_KEY_WIDTH), F32)],
        compiler_params=pltpu.CompilerParams(dimension_semantics=("arbitrary", "arbitrary"),
                                             vmem_limit_bytes=VMEM_LIMIT),
        name="gla",
    )(gin, s0_t, wgu, bg, gg)


def _rel_bucket(dist):
    n = jnp.maximum(dist, 0)
    exact = REL_BUCKETS // 2
    nf = jnp.maximum(n, 1).astype(F32)
    large = exact + (jnp.log(nf / exact) / math.log(REL_MAX_DIST / exact)
                     * (REL_BUCKETS - exact)).astype(jnp.int32)
    large = jnp.minimum(large, REL_BUCKETS - 1)
    return jnp.where(n < exact, n, large)


def _bias_from_bucket(bucket, rb_ref, h):
    acc = jnp.zeros(bucket.shape, F32)
    for bk in range(REL_BUCKETS):
        acc = jnp.where(bucket == bk, rb_ref[bk, h], acc)
    return acc


def _select_topk(sc_ref, rows, n_chunks, n_allowed, topk, max_iter=48):
    neg_inf = -jnp.inf
    halves = K_CHUNK // LANE

    def sweep(fn, init):
        def body(c, acc):
            k0 = pl.multiple_of(c * K_CHUNK, K_CHUNK)
            x = sc_ref[:, pl.ds(k0, K_CHUNK)]
            for j in range(halves):
                acc = fn(acc, x[:, j * LANE:(j + 1) * LANE])
            return acc
        return lax.fori_loop(0, n_chunks, body, init)

    def count_ge(t):
        tb = jnp.broadcast_to(t, (rows, LANE))
        acc = sweep(lambda a, x: a + jnp.where(x >= tb, 1.0, 0.0), jnp.zeros((rows, LANE), F32))
        return jnp.sum(acc, axis=1, keepdims=True)

    def count_gt(t):
        tb = jnp.broadcast_to(t, (rows, LANE))
        acc = sweep(lambda a, x: a + jnp.where(x > tb, 1.0, 0.0), jnp.zeros((rows, LANE), F32))
        return jnp.sum(acc, axis=1, keepdims=True)

    def max_below(t):
        tb = jnp.broadcast_to(t, (rows, LANE))
        acc = sweep(lambda a, x: jnp.maximum(a, jnp.where(x < tb, x, neg_inf)),
                    jnp.full((rows, LANE), neg_inf, F32))
        return jnp.max(acc, axis=1, keepdims=True)

    row_max = jnp.max(sweep(jnp.maximum, jnp.full((rows, LANE), neg_inf, F32)),
                      axis=1, keepdims=True)
    row_min = jnp.min(sweep(lambda a, x: jnp.minimum(a, jnp.where(x == neg_inf, jnp.inf, x)),
                            jnp.full((rows, LANE), jnp.inf, F32)), axis=1, keepdims=True)

    k = float(topk)
    take_all = n_allowed <= k
    lo0 = row_min
    hi0 = row_max + jnp.maximum(jnp.abs(row_max) * 1e-6, 1e-30)
    t0 = jnp.full((rows, 1), jnp.finfo(F32).min, F32)
    done0 = jnp.where(take_all, 1.0, 0.0)

    def bis_cond(st):
        it, lo, hi, t, done, stuck = st
        return (it < max_iter) & (jnp.min(jnp.maximum(done, stuck)) < 0.5)

    def bis_body(st):
        it, lo, hi, t, done, stuck = st
        mid = lo + (hi - lo) * 0.5
        no_room = (mid <= lo) | (mid >= hi)
        c = count_ge(mid)
        hit = (c == k) & (done < 0.5)
        t = jnp.where(hit, mid, t)
        done = jnp.where(hit, 1.0, done)
        lo = jnp.where(c > k, mid, lo)
        hi = jnp.where(c < k, mid, hi)
        stuck = jnp.where(no_room, 1.0, stuck)
        return it + 1, lo, hi, t, done, stuck

    _, lo, hi, t, done, _ = lax.while_loop(
        bis_cond, bis_body, (jnp.int32(0), lo0, hi0, t0, done0, jnp.zeros((rows, 1), F32)))

    all_done = jnp.min(done) > 0.5

    def write(fn):
        def body(c, carry):
            k0 = pl.multiple_of(c * K_CHUNK, K_CHUNK)
            x = sc_ref[:, pl.ds(k0, K_CHUNK)]
            sel, carry = fn(x, carry)
            sc_ref[:, pl.ds(k0, K_CHUNK)] = jnp.where(sel, 0.0, neg_inf)
            return carry
        return body

    @pl.when(all_done)
    def _():
        tb = jnp.broadcast_to(t, (rows, K_CHUNK))
        lax.fori_loop(0, n_chunks, write(lambda x, cr: (x >= tb, cr)), 0)

    @pl.when(jnp.logical_not(all_done))
    def _():
        def snap_cond(st):
            return jnp.min(st[2]) < 0.5

        def snap_body(st):
            hi_, t_, done_ = st
            v = max_below(hi_)
            c = count_ge(v)
            ok = (c >= k) & (done_ < 0.5)
            t_ = jnp.where(ok, v, t_)
            hi_ = jnp.where((done_ < 0.5) & jnp.logical_not(ok), v, hi_)
            done_ = jnp.where(ok, 1.0, done_)
            return hi_, t_, done_

        _, tv, _ = lax.while_loop(snap_cond, snap_body, (hi, t, done))
        quota = k - count_gt(tv)
        tb = jnp.broadcast_to(tv, (rows, K_CHUNK))
        qb = jnp.broadcast_to(quota, (rows, K_CHUNK))
        ui = lax.broadcasted_iota(jnp.int32, (K_CHUNK, K_CHUNK), 0)
        uj = lax.broadcasted_iota(jnp.int32, (K_CHUNK, K_CHUNK), 1)
        upper = (ui <= uj).astype(BF16)

        def tie_fn(x, seen):
            tie = x == tb
            rank = jnp.dot(tie.astype(BF16), upper, preferred_element_type=F32) + seen
            sel = (x > tb) | (tie & (rank <= qb))
            return sel, rank[:, K_CHUNK - 1:K_CHUNK]

        lax.fori_loop(0, n_chunks, write(tie_fn), jnp.zeros((rows, 1), F32))


def _head_lane_mask(h, width):
    lane = lax.broadcasted_iota(jnp.int32, (1, width), 1)
    lo = (h % HEAD_GROUP) * ATT_HEAD_DIM
    return (lane >= lo) & (lane < lo + ATT_HEAD_DIM)


def _pattn_body(rb_ref, q_ref, iq_ref, w_ref, k_ref, v_ref, ik_ref, o_ref,
                sc_ref, qp_ref, iqp_ref, wb_ref, bias_ref, m_ref, l_ref, acc_ref, *, topk):
    i = pl.program_id(1)
    n_chunks = (i + 2) // 2

    @pl.when((pl.program_id(0) == 0) & (i == 0))
    def _():
        r = lax.broadcasted_iota(jnp.int32, (Q_TILE, Q_TILE), 0)
        c = lax.broadcasted_iota(jnp.int32, (Q_TILE, Q_TILE), 1)
        for delta in range(2):
            bucket = _rel_bucket(r - c + Q_TILE * delta)
            for h in range(ATT_HEADS):
                bias_ref[delta, h] = _bias_from_bucket(bucket, rb_ref, h)
        for h in range(ATT_HEADS):
            bias_ref[2, h] = jnp.full((Q_TILE, Q_TILE), rb_ref[REL_BUCKETS - 1, h], F32)

    q = q_ref[0]
    iq = iq_ref[0]
    w = w_ref[0]
    for h in range(ATT_HEADS):
        g = h // HEAD_GROUP
        qp_ref[h] = jnp.where(_head_lane_mask(h, 256), q[:, 256 * g:256 * (g + 1)], 0)
        iqp_ref[h] = jnp.concatenate(
            [iq[:, IDX_DIM * h:IDX_DIM * (h + 1)], jnp.zeros((Q_TILE, LANE - IDX_DIM), BF16)], axis=1)
        wb_ref[h] = jnp.broadcast_to(w[:, IDX_DIM + h:IDX_DIM + h + 1], (Q_TILE, K_CHUNK))

    q_pos = i * Q_TILE + lax.broadcasted_iota(jnp.int32, (Q_TILE, 1), 0)

    def score_chunk(c, carry):
        k0 = pl.multiple_of(c * K_CHUNK, K_CHUNK)
        ikc = ik_ref[0, pl.ds(k0, K_CHUNK), :]
        acc = jnp.zeros((Q_TILE, K_CHUNK), F32)
        for h in range(IDX_HEADS):
            s = lax.dot_general(iqp_ref[h], ikc, NT_DIMS, preferred_element_type=F32)
            acc = acc + jnp.maximum(s, 0.0) * wb_ref[h]
        k_pos = k0 + lax.broadcasted_iota(jnp.int32, (1, K_CHUNK), 1)
        sc_ref[:, pl.ds(k0, K_CHUNK)] = jnp.where(k_pos <= q_pos, acc, -jnp.inf)
        return carry

    lax.fori_loop(0, n_chunks, score_chunk, 0)

    _select_topk(sc_ref, Q_TILE, n_chunks, (q_pos + 1).astype(F32), topk)

    m_ref[...] = jnp.full(m_ref.shape, -1e30, F32)
    l_ref[...] = jnp.zeros(l_ref.shape, F32)
    acc_ref[...] = jnp.zeros(acc_ref.shape, F32)

    def attend_chunk(c, carry):
        k0 = pl.multiple_of(c * K_CHUNK, K_CHUNK)
        mb = sc_ref[:, pl.ds(k0, K_CHUNK)]
        d0 = i - 2 * c
        b0 = jnp.clip(d0, 0, 2)
        b1 = jnp.clip(d0 - 1, 0, 2)
        for h in range(ATT_HEADS):
            g = h // HEAD_GROUP
            kc = k_ref[0, pl.ds(k0, K_CHUNK), 256 * g:256 * (g + 1)]
            vc = v_ref[0, pl.ds(k0, K_CHUNK), 256 * g:256 * (g + 1)]
            s = lax.dot_general(qp_ref[h], kc, NT_DIMS, preferred_element_type=F32)
            bias = jnp.concatenate([bias_ref[b0, h], bias_ref[b1, h]], axis=1)
            logit = s + bias + mb
            m_old = m_ref[h]
            m_new = jnp.maximum(m_old, jnp.max(logit, axis=1, keepdims=True))
            alpha = jnp.exp(m_old - m_new)
            p = jnp.exp(logit - m_new)
            l_ref[h] = alpha * l_ref[h] + jnp.sum(p, axis=1, keepdims=True)
            acc_ref[h] = alpha * acc_ref[h] + jnp.dot(p.astype(BF16), vc, preferred_element_type=F32)
            m_ref[h] = m_new
        return carry

    lax.fori_loop(0, n_chunks, attend_chunk, 0)

    outs = []
    for g in range(ATT_HEADS // HEAD_GROUP):
        og = jnp.zeros((Q_TILE, 256), F32)
        for hh in range(HEAD_GROUP):
            h = g * HEAD_GROUP + hh
            og = og + jnp.where(_head_lane_mask(h, 256), acc_ref[h] / l_ref[h], 0.0)
        outs.append(og)
    o_ref[0] = jnp.concatenate(outs, axis=1).astype(BF16)


def _prompt_attention(rel_bias, q, iq, ikw, kb, vb, ikb, topk):
    bsz, s, _ = q.shape
    tile = lambda w: pl.BlockSpec((1, Q_TILE, w), lambda b, i: (b, i, 0))
    full = lambda w: pl.BlockSpec((1, s, w), lambda b, i: (b, 0, 0))
    body = functools.partial(_pattn_body, topk=topk)
    return pl.pallas_call(
        body,
        grid=(bsz, s // Q_TILE),
        in_specs=[pl.BlockSpec(memory_space=pltpu.SMEM),
                  tile(ATT_WIDTH), tile(ATT_WIDTH), tile(LANE),
                  full(ATT_WIDTH), full(ATT_WIDTH), full(LANE)],
        out_specs=tile(ATT_WIDTH),
        out_shape=jax.ShapeDtypeStruct((bsz, s, ATT_WIDTH), BF16),
        scratch_shapes=[
            pltpu.VMEM((Q_TILE, s), F32),
            pltpu.VMEM((ATT_HEADS, Q_TILE, 256), BF16),
            pltpu.VMEM((IDX_HEADS, Q_TILE, LANE), BF16),
            pltpu.VMEM((IDX_HEADS, Q_TILE, K_CHUNK), F32),
            pltpu.VMEM((3, ATT_HEADS, Q_TILE, Q_TILE), F32),
            pltpu.VMEM((ATT_HEADS, Q_TILE, 1), F32),
            pltpu.VMEM((ATT_HEADS, Q_TILE, 1), F32),
            pltpu.VMEM((ATT_HEADS, Q_TILE, 256), F32),
        ],
        compiler_params=pltpu.CompilerParams(dimension_semantics=("arbitrary", "arbitrary"),
                                             vmem_limit_bytes=VMEM_LIMIT),
        name="prompt_attention",
    )(rel_bias, q, iq, ikw, kb, vb, ikb)


SCORE_PAGES = 16
ATTN_PAGES = 8
NEW_PAD = 16


def _sscore_body(pt_ref, iq_ref, w_ref, *rest, dec_seq):
    del pt_ref
    pages = rest[:SCORE_PAGES]
    out_ref = rest[SCORE_PAGES]
    iq = iq_ref[0]
    w = w_ref[0]
    for m in range(SCORE_PAGES):
        pg = pages[m][0].astype(BF16)
        s = lax.dot_general(iq, pg, NT_DIMS, preferred_element_type=F32)
        y = jnp.maximum(s, 0.0) * w
        rows = [jnp.sum(y[IDX_HEADS * t:IDX_HEADS * (t + 1)], axis=0, keepdims=True)
                for t in range(dec_seq)]
        out_ref[0, :, m * PAGE_SIZE:(m + 1) * PAGE_SIZE] = jnp.concatenate(rows, axis=0)


def _sample_scores(page_table, iq_rows, w_rows, cache_idx, dec_seq):
    nseq, n_pages = page_table.shape
    steps = n_pages // SCORE_PAGES
    rows = dec_seq * IDX_HEADS

    def page_spec(m):
        return pl.BlockSpec((1, PAGE_SIZE, IDX_DIM),
                            lambda b, j, pt: (pt[b, j * SCORE_PAGES + m], 0, 0))

    grid_spec = pltpu.PrefetchScalarGridSpec(
        num_scalar_prefetch=1,
        grid=(nseq, steps),
        in_specs=[pl.BlockSpec((1, rows, IDX_DIM), lambda b, j, pt: (b, 0, 0)),
                  pl.BlockSpec((1, rows, 1), lambda b, j, pt: (b, 0, 0))]
                 + [page_spec(m) for m in range(SCORE_PAGES)],
        out_specs=pl.BlockSpec((1, dec_seq, SCORE_PAGES * PAGE_SIZE), lambda b, j, pt: (b, 0, j)),
    )
    return pl.pallas_call(
        functools.partial(_sscore_body, dec_seq=dec_seq),
        grid_spec=grid_spec,
        out_shape=jax.ShapeDtypeStruct((nseq, dec_seq, n_pages * PAGE_SIZE), F32),
        compiler_params=pltpu.CompilerParams(dimension_semantics=("arbitrary", "arbitrary"),
                                             vmem_limit_bytes=VMEM_LIMIT),
        name="sample_scores",
    )(page_table, iq_rows, w_rows, *([cache_idx] * SCORE_PAGES))


def _sattn_body(pt_ref, rb_ref, sc_in_ref, iq_ref, w_ref, iknew_ref, qp_ref, knew_ref, vnew_ref,
                *rest, dec_seq, past, topk):
    del pt_ref
    kpages = rest[:ATTN_PAGES]
    vpages = rest[ATTN_PAGES:2 * ATTN_PAGES]
    o_ref = rest[2 * ATTN_PAGES]
    sc_ref, mb_ref, m_ref, l_ref, acc_ref = rest[2 * ATTN_PAGES + 1:]
    j = pl.program_id(1)
    n_steps = pl.num_programs(1)
    rows = dec_seq * ATT_HEADS
    n_chunks = past // K_CHUNK + 1
    row_t = lax.broadcasted_iota(jnp.int32, (rows, 1), 0) // ATT_HEADS
    row_h = lax.broadcasted_iota(jnp.int32, (rows, 1), 0) % ATT_HEADS

    @pl.when(j == 0)
    def _():
        s = lax.dot_general(iq_ref[0], iknew_ref[0], NT_DIMS, preferred_element_type=F32)
        y = jnp.maximum(s, 0.0) * w_ref[0]
        new_rows = [jnp.sum(y[IDX_HEADS * t:IDX_HEADS * (t + 1)], axis=0, keepdims=True)
                    for t in range(dec_seq)]
        new_sc = jnp.concatenate(new_rows, axis=0)
        t_idx = lax.broadcasted_iota(jnp.int32, (dec_seq, K_CHUNK), 0)
        k_idx = lax.broadcasted_iota(jnp.int32, (dec_seq, K_CHUNK), 1)
        padded = jnp.concatenate(
            [new_sc, jnp.zeros((dec_seq, K_CHUNK - NEW_PAD), F32)], axis=1)
        sc_ref[:, 0:past] = sc_in_ref[0]
        sc_ref[:, past:past + K_CHUNK] = jnp.where(k_idx <= t_idx, padded, -jnp.inf)
        n_allowed = (past + 1 + lax.broadcasted_iota(jnp.int32, (dec_seq, 1), 0)).astype(F32)
        _select_topk(sc_ref, dec_seq, n_chunks, n_allowed, topk)
        for t in range(dec_seq):
            mb_ref[ATT_HEADS * t:ATT_HEADS * (t + 1), :] = jnp.broadcast_to(
                sc_ref[t:t + 1, :], (ATT_HEADS, past + K_CHUNK))
        m_ref[...] = jnp.full(m_ref.shape, -1e30, F32)
        l_ref[...] = jnp.zeros(l_ref.shape, F32)
        acc_ref[...] = jnp.zeros(acc_ref.shape, F32)

    far = jnp.zeros((rows, 1), F32)
    for h in range(ATT_HEADS):
        far = jnp.where(row_h == h, rb_ref[REL_BUCKETS - 1, h], far)
    qp = qp_ref[0]

    def near_bias(dist):
        bucket = _rel_bucket(dist)
        acc = jnp.zeros(dist.shape, F32)
        for h in range(ATT_HEADS):
            acc = jnp.where(row_h == h, _bias_from_bucket(bucket, rb_ref, h), acc)
        return acc

    def accumulate(logit, v_bf):
        m_old = m_ref[...]
        m_new = jnp.maximum(m_old, jnp.max(logit, axis=1, keepdims=True))
        alpha = jnp.exp(m_old - m_new)
        p = jnp.exp(logit - m_new)
        l_ref[...] = alpha * l_ref[...] + jnp.sum(p, axis=1, keepdims=True)
        acc_ref[...] = alpha * acc_ref[...] + jnp.dot(p.astype(BF16), v_bf,
                                                     preferred_element_type=F32)
        m_ref[...] = m_new

    for m in range(ATTN_PAGES):
        kp = kpages[m][0].astype(BF16)
        vp = vpages[m][0].astype(BF16)
        s = lax.dot_general(qp, kp, NT_DIMS, preferred_element_type=F32)
        k0 = pl.multiple_of((j * ATTN_PAGES + m) * PAGE_SIZE, PAGE_SIZE)
        mb = mb_ref[:, pl.ds(k0, PAGE_SIZE)]
        if m == ATTN_PAGES - 1:
            off = lax.broadcasted_iota(jnp.int32, (rows, PAGE_SIZE), 1)
            is_last = j == n_steps - 1
            bias = jnp.where(is_last, near_bias(PAGE_SIZE + row_t - off), far)
        else:
            bias = far
        accumulate(s + bias + mb, vp)

    @pl.when(j == n_steps - 1)
    def _():
        s = lax.dot_general(qp, knew_ref[0], NT_DIMS, preferred_element_type=F32)
        kk = lax.broadcasted_iota(jnp.int32, (rows, NEW_PAD), 1)
        bias = near_bias(row_t - kk)
        mb = mb_ref[:, past:past + NEW_PAD]
        accumulate(s + bias + mb, vnew_ref[0])
        res = acc_ref[...] / l_ref[...]
        lane_h = lax.broadcasted_iota(jnp.int32, (rows, ATT_WIDTH), 1) // ATT_HEAD_DIM
        res = jnp.where(lane_h == row_h, res, 0.0)
        out_rows = [jnp.sum(res[ATT_HEADS * t:ATT_HEADS * (t + 1)], axis=0, keepdims=True)
                    for t in range(dec_seq)]
        o_ref[0] = jnp.concatenate(out_rows, axis=0).astype(BF16)


def _sample_attention(page_table, rel_bias, scores, iq_rows, w_rows, iknew, qp_rows, knew, vnew,
                      cache_k, cache_v, dec_seq, topk):
    nseq, n_pages = page_table.shape
    past = n_pages * PAGE_SIZE
    steps = n_pages // ATTN_PAGES
    rows = dec_seq * ATT_HEADS

    def page_spec(m):
        return pl.BlockSpec((1, PAGE_SIZE, ATT_WIDTH),
                            lambda b, j, pt: (pt[b, j * ATTN_PAGES + m], 0, 0))

    per_seq = lambda shape: pl.BlockSpec((1,) + shape, lambda b, j, pt: (b, 0, 0))
    grid_spec = pltpu.PrefetchScalarGridSpec(
        num_scalar_prefetch=1,
        grid=(nseq, steps),
        in_specs=[pl.BlockSpec(memory_space=pltpu.SMEM),
                  per_seq((dec_seq, past)), per_seq((rows, IDX_DIM)), per_seq((rows, 1)),
                  per_seq((NEW_PAD, IDX_DIM)), per_seq((rows, ATT_WIDTH)),
                  per_seq((NEW_PAD, ATT_WIDTH)), per_seq((NEW_PAD, ATT_WIDTH))]
                 + [page_spec(m) for m in range(ATTN_PAGES)] * 2,
        out_specs=per_seq((dec_seq, ATT_WIDTH)),
        scratch_shapes=[
            pltpu.VMEM((dec_seq, past + K_CHUNK), F32),
            pltpu.VMEM((rows, past + K_CHUNK), F32),
            pltpu.VMEM((rows, 1), F32),
            pltpu.VMEM((rows, 1), F32),
            pltpu.VMEM((rows, ATT_WIDTH), F32),
        ],
    )
    return pl.pallas_call(
        functools.partial(_sattn_body, dec_seq=dec_seq, past=past, topk=topk),
        grid_spec=grid_spec,
        out_shape=jax.ShapeDtypeStruct((nseq, dec_seq, ATT_WIDTH), BF16),
        compiler_params=pltpu.CompilerParams(dimension_semantics=("arbitrary", "arbitrary"),
                                             vmem_limit_bytes=VMEM_LIMIT),
        name="sample_attention",
    )(page_table, rel_bias, scores, iq_rows, w_rows, iknew, qp_rows, knew, vnew,
      *([cache_k] * ATTN_PAGES), *([cache_v] * ATTN_PAGES))


FF_CHUNK = 256


def _post_body(att_ref, gla_ref, x_ref, p_ref, wo_ref, gpm_ref, gpf_ref, wg_ref, wu_ref, wd_ref,
               gpo_ref, wpg_ref, wpp_ref, o_ref, *, d_ff):
    wo = wo_ref
    mixed = jnp.dot(att_ref[...], wo[0:ATT_WIDTH, :], preferred_element_type=F32)
    mixed = mixed + jnp.dot(gla_ref[...], wo[ATT_WIDTH:ATT_WIDTH + GLA_WIDTH, :],
                            preferred_element_type=F32)
    h = x_ref[...] + _rms(mixed, gpm_ref[...])
    f = _rms(h, gpf_ref[...]).astype(BF16)
    ff = jnp.zeros(h.shape, F32)
    for c in range(d_ff // FF_CHUNK):
        cs = slice(c * FF_CHUNK, (c + 1) * FF_CHUNK)
        gate = jnp.dot(f, wg_ref[:, cs], preferred_element_type=F32)
        up = jnp.dot(f, wu_ref[:, cs], preferred_element_type=F32)
        act = (gate * jax.nn.sigmoid(gate) * up).astype(BF16)
        ff = ff + jnp.dot(act, wd_ref[cs, :], preferred_element_type=F32)
    h = h + _rms(ff, gpo_ref[...])
    gate = jax.nn.sigmoid(jnp.dot(h.astype(BF16), wpg_ref[...], preferred_element_type=F32))
    ple = jnp.dot(p_ref[...].astype(BF16), wpp_ref[...], preferred_element_type=F32)
    o_ref[...] = h + gate * ple


def _post(att, gla, x, p, wo, gpm, gpf, wg, wu, wd, gpo, wpg, wpp, tm):
    t = x.shape[0]
    d_ff = wg.shape[1]
    row = lambda w: pl.BlockSpec((tm, w), lambda i: (i, 0))
    const = lambda a: pl.BlockSpec(a.shape, lambda i: (0, 0), pipeline_mode=pl.Buffered(1))
    return pl.pallas_call(
        functools.partial(_post_body, d_ff=d_ff),
        grid=(t // tm,),
        in_specs=[row(ATT_WIDTH), row(GLA_WIDTH), row(D_MODEL), row(p.shape[1]),
                  const(wo), const(gpm), const(gpf), const(wg), const(wu), const(wd),
                  const(gpo), const(wpg), const(wpp)],
        out_specs=row(D_MODEL),
        out_shape=jax.ShapeDtypeStruct((t, D_MODEL), F32),
        compiler_params=pltpu.CompilerParams(dimension_semantics=("arbitrary",),
                                             vmem_limit_bytes=VMEM_LIMIT),
        name="post",
    )(att, gla, x, p, wo, gpm, gpf, wg, wu, wd, gpo, wpg, wpp)


def _pack_w_in(w_in):
    offs = [0]
    for wdt in (ATT_WIDTH, ATT_WIDTH, ATT_WIDTH, IDX_HEADS * IDX_DIM, IDX_DIM, IDX_HEADS,
                GLA_KEY_WIDTH, GLA_KEY_WIDTH, GLA_WIDTH, GLA_GATE_RANK, GLA_WIDTH):
        offs.append(offs[-1] + wdt)
    aq, ak, av, iq, ik, iw, gq, gk, gv, glr, gout = [w_in[:, offs[n]:offs[n + 1]] for n in range(11)]
    d = w_in.shape[0]
    zeros = lambda n: jnp.zeros((d, n), w_in.dtype)
    packed = jnp.concatenate([
        aq * (ATT_HEAD_DIM ** -0.5), ak, av, iq * (IDX_DIM ** -0.5),
        ik, iw, zeros(LANE - IDX_DIM - IDX_HEADS),
        gq * (GLA_DK ** -0.5), gk, gv, glr, zeros(LANE - GLA_GATE_RANK), gout], axis=1)
    return packed.astype(BF16)


def _state_to_t(s):
    n = s.shape[0]
    return jnp.transpose(s, (0, 3, 1, 2)).reshape(n, GLA_DV, GLA_KEY_WIDTH)


def _state_from_t(st):
    n = st.shape[0]
    return jnp.transpose(st.reshape(n, GLA_DV, GLA_HEADS, GLA_DK), (0, 2, 3, 1))


def _layer(x_prompt, x_sample, p_prompt, p_sample, cache_k, cache_v, cache_idx_k, state_gla,
           page_table, rel_bias, g_pre_mix, w_in, w_gate_up, b_gate, g_gla_out, w_out,
           g_post_mix, g_pre_ffn, w_ff_gate, w_ff_up, w_ff_down, g_post_ffn, w_ple_gate,
           w_ple_proj):
    bsz, seq, d = x_prompt.shape
    nseq, dec_seq, _ = x_sample.shape
    n_pool = cache_k.shape[0]
    past = page_table.shape[1] * PAGE_SIZE
    row2 = lambda g: g.reshape(1, -1)

    w_packed = _pack_w_in(w_in)
    wgu = jnp.concatenate(
        [w_gate_up, jnp.zeros((LANE - GLA_GATE_RANK, GLA_KEY_WIDTH), w_gate_up.dtype)],
        axis=0).astype(BF16)
    post_w = (w_out.astype(BF16), row2(g_post_mix), row2(g_pre_ffn), w_ff_gate.astype(BF16),
              w_ff_up.astype(BF16), w_ff_down.astype(BF16), row2(g_post_ffn),
              w_ple_gate.astype(BF16), w_ple_proj.astype(BF16))
    gla_w = (wgu, row2(b_gate), row2(g_gla_out))

    tp = bsz * seq
    q, kf, vf, kb, vb, iq, ikw, ikb, gin = _inproj(
        x_prompt.reshape(tp, d), row2(g_pre_mix), w_packed, min(512, tp))
    r3 = lambda a: a.reshape(bsz, seq, a.shape[-1])
    att = _prompt_attention(rel_bias, r3(q), r3(iq), r3(ikw), r3(kb), r3(vb), r3(ikb),
                            min(TOPK_MAX, seq // 4))
    gla_chunk = min(64, seq)
    gla_o, st_p = _gla(r3(gin), jnp.zeros((bsz, GLA_DV, GLA_KEY_WIDTH), F32), *gla_w,
                       tile=min(512, seq), chunk=gla_chunk, n_valid=gla_chunk)
    y_p = _post(att.reshape(tp, ATT_WIDTH), gla_o.reshape(tp, GLA_WIDTH), x_prompt.reshape(tp, d),
                p_prompt.reshape(tp, -1), *post_w, min(256, tp))
    outs_p = (y_p.reshape(bsz, seq, d),
              kf.reshape(bsz, seq, ATT_HEADS, ATT_HEAD_DIM),
              vf.reshape(bsz, seq, ATT_HEADS, ATT_HEAD_DIM),
              r3(ikw)[..., :IDX_DIM],
              _state_from_t(st_p))

    ts = nseq * dec_seq
    q, kf, vf, kb, vb, iq, ikw, ikb, gin = _inproj(
        x_sample.reshape(ts, d), row2(g_pre_mix), w_packed, min(512, ts))
    rows = dec_seq * ATT_HEADS
    iq_rows = iq.reshape(nseq, rows, IDX_DIM)
    w_rows = ikw[:, IDX_DIM:IDX_DIM + IDX_HEADS].reshape(nseq, rows, 1)
    head_of_col = jnp.arange(ATT_WIDTH, dtype=jnp.int32) // ATT_HEAD_DIM
    head_mask = head_of_col[None, :] == jnp.arange(ATT_HEADS, dtype=jnp.int32)[:, None]
    qp_rows = jnp.where(head_mask[None, None], q.reshape(nseq, dec_seq, 1, ATT_WIDTH),
                        jnp.zeros((), BF16)).reshape(nseq, rows, ATT_WIDTH)
    pad_new = lambda a: jnp.pad(a.reshape(nseq, dec_seq, a.shape[-1]),
                                ((0, 0), (0, NEW_PAD - dec_seq), (0, 0)))
    scores = _sample_scores(page_table, iq_rows, w_rows, cache_idx_k, dec_seq)
    att_s = _sample_attention(
        page_table, rel_bias, scores, iq_rows, w_rows, pad_new(ikb[:, :IDX_DIM]), qp_rows,
        pad_new(kb), pad_new(vb), cache_k.reshape(n_pool, PAGE_SIZE, ATT_WIDTH),
        cache_v.reshape(n_pool, PAGE_SIZE, ATT_WIDTH), dec_seq,
        min(TOPK_MAX, (past + dec_seq) // 4))
    gin_s = jnp.pad(gin.reshape(nseq, dec_seq, GLA_IN_WIDTH),
                    ((0, 0), (0, NEW_PAD - dec_seq), (0, 0)))
    gla_s, st_s = _gla(gin_s, _state_to_t(state_gla), *gla_w,
                       tile=NEW_PAD, chunk=NEW_PAD, n_valid=dec_seq)
    y_s = _post(att_s.reshape(ts, ATT_WIDTH), gla_s[:, :dec_seq].reshape(ts, GLA_WIDTH),
                x_sample.reshape(ts, d), p_sample.reshape(ts, -1), *post_w, min(256, ts))
    outs_s = (y_s.reshape(nseq, dec_seq, d),
              kf.reshape(nseq, dec_seq, ATT_HEADS, ATT_HEAD_DIM),
              vf.reshape(nseq, dec_seq, ATT_HEADS, ATT_HEAD_DIM),
              ikw[:, :IDX_DIM].reshape(nseq, dec_seq, IDX_DIM),
              _state_from_t(st_s))
    return outs_p, outs_s


def kernel(x_prompt, x_sample, p_prompt, p_sample, cache_k, cache_v, cache_idx_k, state_gla,
           page_table, rel_bias, g_pre_mix, w_in, w_gate_up, b_gate, g_gla_out, w_out,
           g_post_mix, g_pre_ffn, w_ff_gate, w_ff_up, w_ff_down, g_post_ffn, w_ple_gate,
           w_ple_proj):
    depth = w_in.shape[0]
    h_p, h_s = x_prompt, x_sample
    per_layer = []
    for i in range(depth):
        outs_p, outs_s = _layer(
            h_p, h_s, p_prompt[i], p_sample[i], cache_k[i], cache_v[i], cache_idx_k[i],
            state_gla[i], page_table, rel_bias, g_pre_mix[i], w_in[i], w_gate_up[i], b_gate[i],
            g_gla_out[i], w_out[i], g_post_mix[i], g_pre_ffn[i], w_ff_gate[i], w_ff_up[i],
            w_ff_down[i], g_post_ffn[i], w_ple_gate[i], w_ple_proj[i])
        h_p, h_s = outs_p[0], outs_s[0]
        per_layer.append(outs_p[1:] + outs_s[1:])
    stacked = [jnp.stack([lay[n] for lay in per_layer]) for n in range(8)]
    return (h_p, h_s, stacked[0], stacked[1], stacked[2], stacked[3],
            stacked[4], stacked[5], stacked[6], stacked[7])
```

```python
import functools
import math

import jax
import jax.numpy as jnp
from jax import lax
from jax.experimental import pallas as pl
from jax.experimental.pallas import tpu as pltpu

F32 = jnp.float32
BF16 = jnp.bfloat16

D_MODEL = 1024
ATT_HEADS = 8
ATT_HEAD_DIM = 64
ATT_WIDTH = ATT_HEADS * ATT_HEAD_DIM
IDX_HEADS = 8
IDX_DIM = 64
TOPK_MAX = 256
PAGE_SIZE = 128
GLA_HEADS = 4
GLA_DK = 64
GLA_DV = 128
GLA_KEY_WIDTH = GLA_HEADS * GLA_DK
GLA_WIDTH = GLA_HEADS * GLA_DV
GLA_GATE_RANK = 16
GLA_GATE_TAU = 16.0
REL_BUCKETS = 32
REL_MAX_DIST = 128
RMS_EPS = 1e-6

LANE = 128
SUBLANE = 8
MXU_WIDTH = 256
Q_TILE = MXU_WIDTH
K_CHUNK = MXU_WIDTH
HEAD_GROUP = MXU_WIDTH // ATT_HEAD_DIM
VMEM_LIMIT = 56 * 1024 * 1024

SEG_Q = (0, 512)
SEG_K = (512, 1024)
SEG_V = (1024, 1536)
SEG_IQ = (1536, 2048)
SEG_IKW = (2048, 2176)
SEG_GLA = (2176, 3840)
PROJ_PACKED = 3840
GLA_IN_WIDTH = SEG_GLA[1] - SEG_GLA[0]

NT_DIMS = (((1,), (1,)), ((), ()))
TN_DIMS = (((0,), (0,)), ((), ()))


def _rms(x, g):
    return x * lax.rsqrt(jnp.mean(x * x, axis=-1, keepdims=True) + RMS_EPS) * g


def _inproj_body(x_ref, g_ref, w_ref, wvt_ref, q_ref, k_ref, v_ref, kb_ref, vb_ref, vt_ref, iq_ref,
                 ikw_ref, ikb_ref, gla_ref):
    a = _rms(x_ref[...], g_ref[...]).astype(BF16)

    def proj(seg):
        return jnp.dot(a, w_ref[:, seg[0]:seg[1]], preferred_element_type=F32)

    q_ref[...] = proj(SEG_Q).astype(BF16)
    kf = proj(SEG_K)
    k_ref[...] = kf
    kb_ref[...] = kf.astype(BF16)
    vf = proj(SEG_V)
    v_ref[...] = vf
    vb_ref[...] = vf.astype(BF16)
    vt_ref[0] = lax.dot_general(wvt_ref[...], a, NT_DIMS, preferred_element_type=F32).astype(BF16)
    iq_ref[...] = proj(SEG_IQ).astype(BF16)
    lane = lax.broadcasted_iota(jnp.int32, (1, LANE), 1)
    is_w = (lane >= IDX_DIM) & (lane < IDX_DIM + IDX_HEADS)
    ikw = proj(SEG_IKW) * jnp.where(is_w, IDX_HEADS ** -0.5, 1.0)
    ikw_ref[...] = ikw
    ikb_ref[...] = ikw.astype(BF16)
    gla_ref[...] = proj(SEG_GLA)


def _inproj(x, g, w_packed, wv_t, tm, seq):
    t = x.shape[0]
    tiles_per_seq = seq // tm
    row = lambda w: pl.BlockSpec((tm, w), lambda i: (i, 0))
    const = lambda s: pl.BlockSpec(s, lambda i: (0, 0))
    out_shape = (
        jax.ShapeDtypeStruct((t, ATT_WIDTH), BF16),
        jax.ShapeDtypeStruct((t, ATT_WIDTH), F32),
        jax.ShapeDtypeStruct((t, ATT_WIDTH), F32),
        jax.ShapeDtypeStruct((t, ATT_WIDTH), BF16),
        jax.ShapeDtypeStruct((t, ATT_WIDTH), BF16),
        jax.ShapeDtypeStruct((t // seq, ATT_WIDTH, seq), BF16),
        jax.ShapeDtypeStruct((t, ATT_WIDTH), BF16),
        jax.ShapeDtypeStruct((t, LANE), F32),
        jax.ShapeDtypeStruct((t, LANE), BF16),
        jax.ShapeDtypeStruct((t, GLA_IN_WIDTH), F32),
    )
    vt_spec = pl.BlockSpec((1, ATT_WIDTH, tm), lambda i: (i // tiles_per_seq, 0, i % tiles_per_seq))
    return pl.pallas_call(
        _inproj_body,
        grid=(t // tm,),
        in_specs=[row(D_MODEL), const((1, D_MODEL)), const((D_MODEL, PROJ_PACKED)),
                  const((ATT_WIDTH, D_MODEL))],
        out_specs=(row(ATT_WIDTH), row(ATT_WIDTH), row(ATT_WIDTH), row(ATT_WIDTH), row(ATT_WIDTH),
                   vt_spec, row(ATT_WIDTH), row(LANE), row(LANE), row(GLA_IN_WIDTH)),
        out_shape=out_shape,
        compiler_params=pltpu.CompilerParams(dimension_semantics=("arbitrary",),
                                             vmem_limit_bytes=VMEM_LIMIT),
        name="inproj",
    )(x, g, w_packed, wv_t)


def _log_sigmoid(z):
    return jnp.minimum(z, 0.0) - jnp.log1p(jnp.exp(-jnp.abs(z)))


def _gla_body(gin_ref, s0_ref, wgu_ref, bg_ref, gg_ref, o_ref, sout_ref, st_ref, *, chunk, n_valid):
    @pl.when(pl.program_id(1) == 0)
    def _():
        st_ref[...] = s0_ref[0]

    n_chunks = gin_ref.shape[1] // chunk
    ri = lax.broadcasted_iota(jnp.int32, (chunk, chunk), 0)
    ci = lax.broadcasted_iota(jnp.int32, (chunk, chunk), 1)
    causal = ri >= ci
    tri = causal.astype(F32)
    row_valid = lax.broadcasted_iota(jnp.int32, (chunk, 1), 0) < n_valid

    def one_chunk(c, carry):
        r0 = pl.multiple_of(c * chunk, chunk)
        blk = gin_ref[0, pl.ds(r0, chunk), :]
        gq = blk[:, 0:256]
        gk = blk[:, 256:512]
        gv = blk[:, 512:1024]
        glr = blk[:, 1024:1152]
        gout = blk[:, 1152:1664]
        z = jnp.dot(glr.astype(BF16), wgu_ref[...], preferred_element_type=F32) + bg_ref[...]
        la = _log_sigmoid(z) / GLA_GATE_TAU
        if n_valid < chunk:
            la = jnp.where(row_valid, la, 0.0)
        b = jnp.dot(tri, la, precision=lax.Precision.HIGHEST, preferred_element_type=F32)
        b_last = b[chunk - 1:chunk, :]
        qt = gq * jnp.exp(b)
        kt = gk * jnp.exp(-b)
        kd = gk * jnp.exp(b_last - b)
        dec = jnp.exp(b_last)
        st = st_ref[...]
        o_parts = []
        st_parts = []
        for h in range(GLA_HEADS):
            ks = slice(GLA_DK * h, GLA_DK * (h + 1))
            vs = slice(GLA_DV * h, GLA_DV * (h + 1))
            qh = qt[:, ks].astype(BF16)
            kh = kt[:, ks].astype(BF16)
            kdh = kd[:, ks].astype(BF16)
            vh = gv[:, vs].astype(BF16)
            sth = st[:, ks]
            attn = lax.dot_general(qh, kh, NT_DIMS, preferred_element_type=F32)
            attn = jnp.where(causal, attn, 0.0)
            o = jnp.dot(attn.astype(BF16), vh, preferred_element_type=F32)
            o = o + lax.dot_general(qh, sth.astype(BF16), NT_DIMS, preferred_element_type=F32)
            upd = lax.dot_general(vh, kdh, TN_DIMS, preferred_element_type=F32)
            st_parts.append(dec[:, ks] * sth + upd)
            o_parts.append(_rms(o, gg_ref[...]))
        st_ref[...] = jnp.concatenate(st_parts, axis=1)
        o_all = jnp.concatenate(o_parts, axis=1)
        gate = gout * jax.nn.sigmoid(gout)
        o_ref[0, pl.ds(r0, chunk), :] = (o_all * gate).astype(BF16)
        return carry

    lax.fori_loop(0, n_chunks, one_chunk, 0)
    sout_ref[0] = st_ref[...]


def _gla(gin, s0_t, wgu, bg, gg, *, tile, chunk, n_valid):
    nseq, t, _ = gin.shape
    body = functools.partial(_gla_body, chunk=chunk, n_valid=n_valid)
    return pl.pallas_call(
        body,
        grid=(nseq, t // tile),
        in_specs=[
            pl.BlockSpec((1, tile, GLA_IN_WIDTH), lambda b, j: (b, j, 0)),
            pl.BlockSpec((1, GLA_DV, GLA_KEY_WIDTH), lambda b, j: (b, 0, 0)),
            pl.BlockSpec((LANE, GLA_KEY_WIDTH), lambda b, j: (0, 0)),
            pl.BlockSpec((1, GLA_KEY_WIDTH), lambda b, j: (0, 0)),
            pl.BlockSpec((1, GLA_DV), lambda b, j: (0, 0)),
        ],
        out_specs=(
            pl.BlockSpec((1, tile, GLA_WIDTH), lambda b, j: (b, j, 0)),
            pl.BlockSpec((1, GLA_DV, GLA_KEY_WIDTH), lambda b, j: (b, 0, 0)),
        ),
        out_shape=(
            jax.ShapeDtypeStruct((nseq, t, GLA_WIDTH), BF16),
            jax.ShapeDtypeStruct((nseq, GLA_DV, GLA_KEY_WIDTH), F32),
        ),
        scratch_shapes=[pltpu.VMEM((GLA_DV, GLA_KEY_WIDTH), F32)],
        compiler_params=pltpu.CompilerParams(dimension_semantics=("arbitrary", "arbitrary"),
                                             vmem_limit_bytes=VMEM_LIMIT),
        name="gla",
    )(gin, s0_t, wgu, bg, gg)


def _rel_bucket(dist):
    n = jnp.maximum(dist, 0)
    exact = REL_BUCKETS // 2
    nf = jnp.maximum(n, 1).astype(F32)
    large = exact + (jnp.log(nf / exact) / math.log(REL_MAX_DIST / exact)
                     * (REL_BUCKETS - exact)).astype(jnp.int32)
    large = jnp.minimum(large, REL_BUCKETS - 1)
    return jnp.where(n < exact, n, large)


def _select_topk(sc_ref, n_queries, n_chunks, n_allowed, topk, keys_axis, max_iter=18):
    neg_inf = -jnp.inf
    n_acc = 4 if keys_axis == 0 else K_CHUNK // LANE
    if keys_axis == 1:
        part_shape = (n_queries, LANE)
        q_shape = (n_queries, 1)
        chunk_shape = (n_queries, K_CHUNK)
        load = lambda k0: sc_ref[:, pl.ds(k0, K_CHUNK)]
        pieces = lambda x: [x[:, j * LANE:(j + 1) * LANE] for j in range(K_CHUNK // LANE)]
        last = lambda r: r[:, K_CHUNK - 1:K_CHUNK]
    else:
        part_shape = (SUBLANE, n_queries)
        q_shape = (1, n_queries)
        chunk_shape = (K_CHUNK, n_queries)
        load = lambda k0: sc_ref[pl.ds(k0, K_CHUNK), :]
        pieces = lambda x: [x[j * SUBLANE:(j + 1) * SUBLANE] for j in range(K_CHUNK // SUBLANE)]
        last = lambda r: r[K_CHUNK - 1:K_CHUNK, :]

    def store(k0, val):
        if keys_axis == 1:
            sc_ref[:, pl.ds(k0, K_CHUNK)] = val
        else:
            sc_ref[pl.ds(k0, K_CHUNK), :] = val

    def sweep(fn, init, red, combine):
        def body(c, accs):
            x = load(pl.multiple_of(c * K_CHUNK, K_CHUNK))
            accs = list(accs)
            for n, piece in enumerate(pieces(x)):
                accs[n % n_acc] = fn(accs[n % n_acc], piece)
            return tuple(accs)
        accs = lax.fori_loop(0, n_chunks, body,
                             tuple(jnp.full(part_shape, init, F32) for _ in range(n_acc)))
        acc = accs[0]
        for other in accs[1:]:
            acc = combine(acc, other)
        return red(acc, axis=keys_axis, keepdims=True)

    def count_ge(t):
        tb = jnp.broadcast_to(t, part_shape)
        return sweep(lambda a, x: a + jnp.where(x >= tb, 1.0, 0.0), 0.0, jnp.sum, jnp.add)

    def count_gt(t):
        tb = jnp.broadcast_to(t, part_shape)
        return sweep(lambda a, x: a + jnp.where(x > tb, 1.0, 0.0), 0.0, jnp.sum, jnp.add)

    def max_below(t):
        tb = jnp.broadcast_to(t, part_shape)
        return sweep(lambda a, x: jnp.maximum(a, jnp.where(x < tb, x, neg_inf)), neg_inf,
                     jnp.max, jnp.maximum)

    row_max = sweep(jnp.maximum, neg_inf, jnp.max, jnp.maximum)
    row_min = sweep(lambda a, x: jnp.minimum(a, jnp.where(x == neg_inf, jnp.inf, x)), jnp.inf,
                    jnp.min, jnp.minimum)

    k = float(topk)
    take_all = n_allowed <= k
    lo0 = row_min
    hi0 = row_max + jnp.maximum(jnp.abs(row_max) * 1e-6, 1e-30)
    t0 = jnp.full(q_shape, jnp.finfo(F32).min, F32)
    done0 = jnp.where(take_all, 1.0, 0.0)

    def bis_cond(st):
        it, lo, hi, t, done, stuck = st
        return (it < max_iter) & (jnp.min(jnp.maximum(done, stuck)) < 0.5)

    def bis_body(st):
        it, lo, hi, t, done, stuck = st
        mid = lo + (hi - lo) * 0.5
        no_room = (mid <= lo) | (mid >= hi)
        c = count_ge(mid)
        hit = (c == k) & (done < 0.5)
        t = jnp.where(hit, mid, t)
        done = jnp.where(hit, 1.0, done)
        lo = jnp.where(c > k, mid, lo)
        hi = jnp.where(c < k, mid, hi)
        stuck = jnp.where(no_room, 1.0, stuck)
        return it + 1, lo, hi, t, done, stuck

    _, lo, hi, t, done, _ = lax.while_loop(
        bis_cond, bis_body, (jnp.int32(0), lo0, hi0, t0, done0, jnp.zeros(q_shape, F32)))

    all_done = jnp.min(done) > 0.5

    def write(fn):
        def body(c, carry):
            k0 = pl.multiple_of(c * K_CHUNK, K_CHUNK)
            sel, carry = fn(load(k0), carry)
            store(k0, jnp.where(sel, 0.0, neg_inf))
            return carry
        return body

    @pl.when(all_done)
    def _():
        tb = jnp.broadcast_to(t, chunk_shape)
        lax.fori_loop(0, n_chunks, write(lambda x, cr: (x >= tb, cr)), 0)

    @pl.when(jnp.logical_not(all_done))
    def _():
        def snap_cond(st):
            return jnp.min(st[2]) < 0.5

        def snap_body(st):
            hi_, t_, done_ = st
            v = max_below(hi_)
            c = count_ge(v)
            ok = (c >= k) & (done_ < 0.5)
            t_ = jnp.where(ok, v, t_)
            hi_ = jnp.where((done_ < 0.5) & jnp.logical_not(ok), v, hi_)
            done_ = jnp.where(ok, 1.0, done_)
            return hi_, t_, done_

        _, tv, _ = lax.while_loop(snap_cond, snap_body, (hi, t, done))
        quota = k - count_gt(tv)
        tb = jnp.broadcast_to(tv, chunk_shape)
        qb = jnp.broadcast_to(quota, chunk_shape)
        ui = lax.broadcasted_iota(jnp.int32, (K_CHUNK, K_CHUNK), 0)
        uj = lax.broadcasted_iota(jnp.int32, (K_CHUNK, K_CHUNK), 1)
        prefix = ((ui <= uj) if keys_axis == 1 else (ui >= uj)).astype(BF16)

        def tie_fn(x, seen):
            tie = x == tb
            if keys_axis == 1:
                rank = jnp.dot(tie.astype(BF16), prefix, preferred_element_type=F32) + seen
            else:
                rank = jnp.dot(prefix, tie.astype(BF16), preferred_element_type=F32) + seen
            sel = (x > tb) | (tie & (rank <= qb))
            return sel, last(rank)

        lax.fori_loop(0, n_chunks, write(tie_fn), jnp.zeros(q_shape, F32))


def _head_lane_mask(h, width):
    lane = lax.broadcasted_iota(jnp.int32, (1, width), 1)
    lo = (h % HEAD_GROUP) * ATT_HEAD_DIM
    return (lane >= lo) & (lane < lo + ATT_HEAD_DIM)


def _pattn_body(rb_ref, q_ref, iq_ref, w_ref, k_ref, vt_ref, ik_ref, o_ref,
                sc_ref, qp_ref, iqp_ref, bias_ref, m_ref, l_ref, alpha_ref, acc_ref, x_ref, p_ref,
                *, topk):
    i = pl.program_id(1)
    n_chunks = i + 1

    @pl.when((pl.program_id(0) == 0) & (i == 0))
    def _():
        kk = lax.broadcasted_iota(jnp.int32, (K_CHUNK, Q_TILE), 0)
        qq = lax.broadcasted_iota(jnp.int32, (K_CHUNK, Q_TILE), 1)
        for delta in range(2):
            bucket = _rel_bucket(qq - kk + Q_TILE * delta)
            for h in range(ATT_HEADS):
                acc = jnp.zeros((K_CHUNK, Q_TILE), F32)
                for bk in range(REL_BUCKETS):
                    acc = jnp.where(bucket == bk, rb_ref[bk, h], acc)
                bias_ref[delta, h] = acc - rb_ref[REL_BUCKETS - 1, h]

    q = q_ref[0]
    iq = iq_ref[0]
    for h in range(ATT_HEADS):
        g = h // HEAD_GROUP
        qp_ref[h] = jnp.where(_head_lane_mask(h, MXU_WIDTH),
                              q[:, MXU_WIDTH * g:MXU_WIDTH * (g + 1)], 0)
        iqp_ref[h] = jnp.concatenate(
            [iq[:, IDX_DIM * h:IDX_DIM * (h + 1)], jnp.zeros((Q_TILE, LANE - IDX_DIM), BF16)], axis=1)
    w_t = jnp.transpose(w_ref[0])[IDX_DIM:IDX_DIM + IDX_HEADS, :]

    q_pos = i * Q_TILE + lax.broadcasted_iota(jnp.int32, (1, Q_TILE), 1)

    def score_chunk(c, carry):
        k0 = pl.multiple_of(c * K_CHUNK, K_CHUNK)
        ikc = ik_ref[0, pl.ds(k0, K_CHUNK), :]
        acc = jnp.zeros((K_CHUNK, Q_TILE), F32)
        for h in range(IDX_HEADS):
            s = lax.dot_general(ikc, iqp_ref[h], NT_DIMS, preferred_element_type=F32)
            acc = acc + jnp.maximum(s, 0.0) * w_t[h:h + 1, :]
        k_pos = k0 + lax.broadcasted_iota(jnp.int32, (K_CHUNK, 1), 0)
        sc_ref[pl.ds(k0, K_CHUNK), :] = jnp.where(k_pos <= q_pos, acc, -jnp.inf)
        return carry

    lax.fori_loop(0, n_chunks, score_chunk, 0)

    _select_topk(sc_ref, Q_TILE, n_chunks, (q_pos + 1).astype(F32), topk, keys_axis=0)

    m_ref[...] = jnp.full(m_ref.shape, -1e30, F32)
    l_ref[...] = jnp.zeros(l_ref.shape, F32)
    acc_ref[...] = jnp.zeros(acc_ref.shape, F32)

    def attend_chunk(c, carry, *, near):
        k0 = pl.multiple_of(c * K_CHUNK, K_CHUNK)
        mb = sc_ref[pl.ds(k0, K_CHUNK), :]
        for h in range(ATT_HEADS):
            g = h // HEAD_GROUP
            kc = k_ref[0, pl.ds(k0, K_CHUNK), MXU_WIDTH * g:MXU_WIDTH * (g + 1)]
            x = lax.dot_general(kc, qp_ref[h], NT_DIMS, preferred_element_type=F32) + mb
            if near:
                x = x + bias_ref[i - c, h]
            x_ref[h] = x
            m_old = m_ref[h]
            m_new = jnp.maximum(m_old, jnp.max(x, axis=0, keepdims=True))
            alpha_ref[h] = jnp.exp(m_old - m_new)
            m_ref[h] = m_new
        for h in range(ATT_HEADS):
            p = jnp.exp(x_ref[h] - m_ref[h])
            l_ref[h] = alpha_ref[h] * l_ref[h] + jnp.sum(p, axis=0, keepdims=True)
            p_ref[h] = p.astype(BF16)
        for h in range(ATT_HEADS):
            vt = vt_ref[0, ATT_HEAD_DIM * h:ATT_HEAD_DIM * (h + 1), pl.ds(k0, K_CHUNK)]
            acc_ref[h] = alpha_ref[h] * acc_ref[h] + jnp.dot(vt, p_ref[h],
                                                             preferred_element_type=F32)
        return carry

    n_far = jnp.maximum(i - 1, 0)
    lax.fori_loop(0, n_far, functools.partial(attend_chunk, near=False), 0)
    lax.fori_loop(n_far, n_chunks, functools.partial(attend_chunk, near=True), 0)

    out_t = jnp.concatenate([acc_ref[h] / l_ref[h] for h in range(ATT_HEADS)], axis=0)
    o_ref[0] = jnp.transpose(out_t).astype(BF16)


def _prompt_attention(rel_bias, q, iq, ikw, kb, vt, ikb, topk):
    bsz, s, _ = q.shape
    tile = lambda w: pl.BlockSpec((1, Q_TILE, w), lambda b, i: (b, i, 0))
    per_batch = lambda shape: pl.BlockSpec((1,) + shape, lambda b, i: (b, 0, 0),
                                           pipeline_mode=pl.Buffered(1))
    body = functools.partial(_pattn_body, topk=topk)
    return pl.pallas_call(
        body,
        grid=(bsz, s // Q_TILE),
        in_specs=[pl.BlockSpec(memory_space=pltpu.SMEM),
                  tile(ATT_WIDTH), tile(ATT_WIDTH), tile(LANE),
                  per_batch((s, ATT_WIDTH)), per_batch((ATT_WIDTH, s)), per_batch((s, LANE))],
        out_specs=tile(ATT_WIDTH),
        out_shape=jax.ShapeDtypeStruct((bsz, s, ATT_WIDTH), BF16),
        scratch_shapes=[
            pltpu.VMEM((s, Q_TILE), F32),
            pltpu.VMEM((ATT_HEADS, Q_TILE, MXU_WIDTH), BF16),
            pltpu.VMEM((IDX_HEADS, Q_TILE, LANE), BF16),
            pltpu.VMEM((2, ATT_HEADS, K_CHUNK, Q_TILE), F32),
            pltpu.VMEM((ATT_HEADS, 1, Q_TILE), F32),
            pltpu.VMEM((ATT_HEADS, 1, Q_TILE), F32),
            pltpu.VMEM((ATT_HEADS, 1, Q_TILE), F32),
            pltpu.VMEM((ATT_HEADS, ATT_HEAD_DIM, Q_TILE), F32),
            pltpu.VMEM((ATT_HEADS, K_CHUNK, Q_TILE), F32),
            pltpu.VMEM((ATT_HEADS, K_CHUNK, Q_TILE), BF16),
        ],
        compiler_params=pltpu.CompilerParams(dimension_semantics=("arbitrary", "arbitrary"),
                                             vmem_limit_bytes=VMEM_LIMIT),
        name="prompt_attention",
    )(rel_bias, q, iq, ikw, kb, vt, ikb)


SCORE_PAGES = 16
ATTN_PAGES = 8
NEW_PAD = 16
PAGE_ROWS = PAGE_SIZE * ATT_HEADS


def _sscore_body(pt_ref, iq_ref, w_ref, *rest, dec_seq):
    del pt_ref
    pages = rest[:SCORE_PAGES]
    out_ref = rest[SCORE_PAGES]
    iq = iq_ref[0]
    w = w_ref[0]
    for m in range(SCORE_PAGES):
        pg = pages[m][0].astype(BF16)
        s = lax.dot_general(iq, pg, NT_DIMS, preferred_element_type=F32)
        y = jnp.maximum(s, 0.0) * w
        rows = [jnp.sum(y[IDX_HEADS * t:IDX_HEADS * (t + 1)], axis=0, keepdims=True)
                for t in range(dec_seq)]
        out_ref[0, :, m * PAGE_SIZE:(m + 1) * PAGE_SIZE] = jnp.concatenate(rows, axis=0)


def _sample_scores(page_table, iq_rows, w_rows, cache_idx, dec_seq):
    nseq, n_pages = page_table.shape
    steps = n_pages // SCORE_PAGES
    rows = dec_seq * IDX_HEADS

    def page_spec(m):
        return pl.BlockSpec((1, PAGE_SIZE, IDX_DIM),
                            lambda b, j, pt: (pt[b, j * SCORE_PAGES + m], 0, 0))

    grid_spec = pltpu.PrefetchScalarGridSpec(
        num_scalar_prefetch=1,
        grid=(nseq, steps),
        in_specs=[pl.BlockSpec((1, rows, IDX_DIM), lambda b, j, pt: (b, 0, 0)),
                  pl.BlockSpec((1, rows, 1), lambda b, j, pt: (b, 0, 0))]
                 + [page_spec(m) for m in range(SCORE_PAGES)],
        out_specs=pl.BlockSpec((1, dec_seq, SCORE_PAGES * PAGE_SIZE), lambda b, j, pt: (b, 0, j)),
    )
    return pl.pallas_call(
        functools.partial(_sscore_body, dec_seq=dec_seq),
        grid_spec=grid_spec,
        out_shape=jax.ShapeDtypeStruct((nseq, dec_seq, n_pages * PAGE_SIZE), F32),
        compiler_params=pltpu.CompilerParams(dimension_semantics=("arbitrary", "arbitrary"),
                                             vmem_limit_bytes=VMEM_LIMIT),
        name="sample_scores",
    )(page_table, iq_rows, w_rows, *([cache_idx] * SCORE_PAGES))


def _sattn_body(pt_ref, rb_ref, sc_in_ref, iq_ref, w_ref, iknew_ref, qc_ref, knew_ref, vnew_ref,
                *rest, dec_seq, past, topk):
    del pt_ref
    kpages = rest[:ATTN_PAGES]
    vpages = rest[ATTN_PAGES:2 * ATTN_PAGES]
    o_ref = rest[2 * ATTN_PAGES]
    sc_ref, sel_ref, expand_ref, m_ref, l_ref, acc_ref = rest[2 * ATTN_PAGES + 1:]
    j = pl.program_id(1)
    n_steps = pl.num_programs(1)
    rows = dec_seq * ATT_HEADS
    n_chunks = past // K_CHUNK + 1
    row_t = lax.broadcasted_iota(jnp.int32, (rows, 1), 0) // ATT_HEADS
    row_h = lax.broadcasted_iota(jnp.int32, (rows, 1), 0) % ATT_HEADS

    @pl.when(j == 0)
    def _():
        s = lax.dot_general(iq_ref[0], iknew_ref[0], NT_DIMS, preferred_element_type=F32)
        y = jnp.maximum(s, 0.0) * w_ref[0]
        new_rows = [jnp.sum(y[IDX_HEADS * t:IDX_HEADS * (t + 1)], axis=0, keepdims=True)
                    for t in range(dec_seq)]
        new_sc = jnp.concatenate(new_rows, axis=0)
        t_idx = lax.broadcasted_iota(jnp.int32, (dec_seq, K_CHUNK), 0)
        k_idx = lax.broadcasted_iota(jnp.int32, (dec_seq, K_CHUNK), 1)
        padded = jnp.concatenate(
            [new_sc, jnp.zeros((dec_seq, K_CHUNK - PAGE_SIZE), F32)], axis=1)
        sc_ref[:, 0:past] = sc_in_ref[0]
        sc_ref[:, past:past + K_CHUNK] = jnp.where(k_idx <= t_idx, padded, -jnp.inf)
        n_allowed = (past + 1 + lax.broadcasted_iota(jnp.int32, (dec_seq, 1), 0)).astype(F32)
        _select_topk(sc_ref, dec_seq, n_chunks, n_allowed, topk, keys_axis=1)
        for t in range(dec_seq):
            picked = jnp.where(sc_ref[t:t + 1, :] == 0.0, 1.0, 0.0).astype(BF16)
            sel_ref[ATT_HEADS * t:ATT_HEADS * (t + 1), :] = jnp.broadcast_to(
                picked, (ATT_HEADS, past + K_CHUNK))
        ek = lax.broadcasted_iota(jnp.int32, (PAGE_SIZE, PAGE_ROWS), 0)
        er = lax.broadcasted_iota(jnp.int32, (PAGE_SIZE, PAGE_ROWS), 1) // ATT_HEADS
        expand_ref[...] = (ek == er).astype(BF16)
        m_ref[...] = jnp.full(m_ref.shape, -1e30, F32)
        l_ref[...] = jnp.zeros(l_ref.shape, F32)
        acc_ref[...] = jnp.zeros(acc_ref.shape, F32)

    far = jnp.zeros((rows, 1), F32)
    for h in range(ATT_HEADS):
        far = jnp.where(row_h == h, rb_ref[REL_BUCKETS - 1, h], far)
    qc = qc_ref[0]
    col = lax.broadcasted_iota(jnp.int32, (rows, PAGE_ROWS), 1)
    head_match = (col % ATT_HEADS) == row_h
    col_key = col // ATT_HEADS

    def near_bias(dist):
        bucket = _rel_bucket(dist)
        acc = jnp.zeros(dist.shape, F32)
        for bk in range(REL_BUCKETS):
            per_row = jnp.zeros((rows, 1), F32)
            for h in range(ATT_HEADS):
                per_row = jnp.where(row_h == h, rb_ref[bk, h], per_row)
            acc = jnp.where(bucket == bk, per_row, acc)
        return acc

    def attend_page(k_rows, v_rows, sel_pg, bias):
        s = lax.dot_general(qc, k_rows, NT_DIMS, preferred_element_type=F32)
        picked = jnp.dot(sel_pg, expand_ref[...], preferred_element_type=F32)
        logit = jnp.where(head_match & (picked > 0.5), s + bias, -jnp.inf)
        m_old = m_ref[...]
        m_new = jnp.maximum(m_old, jnp.max(logit, axis=1, keepdims=True))
        alpha = jnp.exp(m_old - m_new)
        p = jnp.exp(logit - m_new)
        l_ref[...] = alpha * l_ref[...] + jnp.sum(p, axis=1, keepdims=True)
        acc_ref[...] = alpha * acc_ref[...] + jnp.dot(p.astype(BF16), v_rows,
                                                     preferred_element_type=F32)
        m_ref[...] = m_new

    is_last = j == n_steps - 1
    for m in range(ATTN_PAGES):
        k0 = pl.multiple_of((j * ATTN_PAGES + m) * PAGE_SIZE, PAGE_SIZE)
        if m == ATTN_PAGES - 1:
            bias = lax.cond(is_last,
                            lambda: near_bias(PAGE_SIZE + row_t - col_key),
                            lambda: jnp.broadcast_to(far, (rows, PAGE_ROWS)))
        else:
            bias = far
        attend_page(kpages[m][0].astype(BF16), vpages[m][0].astype(BF16),
                    sel_ref[:, pl.ds(k0, PAGE_SIZE)], bias)

    @pl.when(is_last)
    def _():
        attend_page(knew_ref[0], vnew_ref[0], sel_ref[:, past:past + PAGE_SIZE],
                    near_bias(row_t - col_key))
        o_ref[0] = (acc_ref[...] / l_ref[...]).astype(BF16)


def _sample_attention(page_table, rel_bias, scores, iq_rows, w_rows, iknew, qc_rows, knew, vnew,
                      cache_k, cache_v, dec_seq, topk):
    nseq, n_pages = page_table.shape
    past = n_pages * PAGE_SIZE
    steps = n_pages // ATTN_PAGES
    rows = dec_seq * ATT_HEADS

    def page_spec(m):
        return pl.BlockSpec((1, PAGE_ROWS, ATT_HEAD_DIM),
                            lambda b, j, pt: (pt[b, j * ATTN_PAGES + m], 0, 0))

    per_seq = lambda shape: pl.BlockSpec((1,) + shape, lambda b, j, pt: (b, 0, 0))
    grid_spec = pltpu.PrefetchScalarGridSpec(
        num_scalar_prefetch=1,
        grid=(nseq, steps),
        in_specs=[pl.BlockSpec(memory_space=pltpu.SMEM),
                  per_seq((dec_seq, past)), per_seq((rows, IDX_DIM)), per_seq((rows, 1)),
                  per_seq((PAGE_SIZE, IDX_DIM)), per_seq((rows, ATT_HEAD_DIM)),
                  per_seq((PAGE_ROWS, ATT_HEAD_DIM)), per_seq((PAGE_ROWS, ATT_HEAD_DIM))]
                 + [page_spec(m) for m in range(ATTN_PAGES)] * 2,
        out_specs=per_seq((rows, ATT_HEAD_DIM)),
        scratch_shapes=[
            pltpu.VMEM((dec_seq, past + K_CHUNK), F32),
            pltpu.VMEM((rows, past + K_CHUNK), BF16),
            pltpu.VMEM((PAGE_SIZE, PAGE_ROWS), BF16),
            pltpu.VMEM((rows, 1), F32),
            pltpu.VMEM((rows, 1), F32),
            pltpu.VMEM((rows, ATT_HEAD_DIM), F32),
        ],
    )
    return pl.pallas_call(
        functools.partial(_sattn_body, dec_seq=dec_seq, past=past, topk=topk),
        grid_spec=grid_spec,
        out_shape=jax.ShapeDtypeStruct((nseq, rows, ATT_HEAD_DIM), BF16),
        compiler_params=pltpu.CompilerParams(dimension_semantics=("arbitrary", "arbitrary"),
                                             vmem_limit_bytes=VMEM_LIMIT),
        name="sample_attention",
    )(page_table, rel_bias, scores, iq_rows, w_rows, iknew, qc_rows, knew, vnew,
      *([cache_k] * ATTN_PAGES), *([cache_v] * ATTN_PAGES))


FF_CHUNK = 256


def _post_body(att_ref, gla_ref, x_ref, p_ref, wo_ref, gpm_ref, gpf_ref, wg_ref, wu_ref, wd_ref,
               gpo_ref, wpg_ref, wpp_ref, o_ref, *, d_ff):
    wo = wo_ref
    mixed = jnp.dot(att_ref[...], wo[0:ATT_WIDTH, :], preferred_element_type=F32)
    mixed = mixed + jnp.dot(gla_ref[...], wo[ATT_WIDTH:ATT_WIDTH + GLA_WIDTH, :],
                            preferred_element_type=F32)
    h = x_ref[...] + _rms(mixed, gpm_ref[...])
    f = _rms(h, gpf_ref[...]).astype(BF16)
    ff = jnp.zeros(h.shape, F32)
    for c in range(d_ff // FF_CHUNK):
        cs = slice(c * FF_CHUNK, (c + 1) * FF_CHUNK)
        gate = jnp.dot(f, wg_ref[:, cs], preferred_element_type=F32)
        up = jnp.dot(f, wu_ref[:, cs], preferred_element_type=F32)
        act = (gate * jax.nn.sigmoid(gate) * up).astype(BF16)
        ff = ff + jnp.dot(act, wd_ref[cs, :], preferred_element_type=F32)
    h = h + _rms(ff, gpo_ref[...])
    gate = jax.nn.sigmoid(jnp.dot(h.astype(BF16), wpg_ref[...], preferred_element_type=F32))
    ple = jnp.dot(p_ref[...].astype(BF16), wpp_ref[...], preferred_element_type=F32)
    o_ref[...] = h + gate * ple


def _post(att, gla, x, p, wo, gpm, gpf, wg, wu, wd, gpo, wpg, wpp, tm):
    t = x.shape[0]
    d_ff = wg.shape[1]
    row = lambda w: pl.BlockSpec((tm, w), lambda i: (i, 0))
    const = lambda a: pl.BlockSpec(a.shape, lambda i: (0, 0), pipeline_mode=pl.Buffered(1))
    return pl.pallas_call(
        functools.partial(_post_body, d_ff=d_ff),
        grid=(t // tm,),
        in_specs=[row(ATT_WIDTH), row(GLA_WIDTH), row(D_MODEL), row(p.shape[1]),
                  const(wo), const(gpm), const(gpf), const(wg), const(wu), const(wd),
                  const(gpo), const(wpg), const(wpp)],
        out_specs=row(D_MODEL),
        out_shape=jax.ShapeDtypeStruct((t, D_MODEL), F32),
        compiler_params=pltpu.CompilerParams(dimension_semantics=("arbitrary",),
                                             vmem_limit_bytes=VMEM_LIMIT),
        name="post",
    )(att, gla, x, p, wo, gpm, gpf, wg, wu, wd, gpo, wpg, wpp)


def _pack_w_in(w_in):
    offs = [0]
    for wdt in (ATT_WIDTH, ATT_WIDTH, ATT_WIDTH, IDX_HEADS * IDX_DIM, IDX_DIM, IDX_HEADS,
                GLA_KEY_WIDTH, GLA_KEY_WIDTH, GLA_WIDTH, GLA_GATE_RANK, GLA_WIDTH):
        offs.append(offs[-1] + wdt)
    aq, ak, av, iq, ik, iw, gq, gk, gv, glr, gout = [w_in[:, offs[n]:offs[n + 1]] for n in range(11)]
    d = w_in.shape[0]
    zeros = lambda n: jnp.zeros((d, n), w_in.dtype)
    packed = jnp.concatenate([
        aq * (ATT_HEAD_DIM ** -0.5), ak, av, iq * (IDX_DIM ** -0.5),
        ik, iw, zeros(LANE - IDX_DIM - IDX_HEADS),
        gq * (GLA_DK ** -0.5), gk, gv, glr, zeros(LANE - GLA_GATE_RANK), gout], axis=1)
    return packed.astype(BF16)


def _state_to_t(s):
    n = s.shape[0]
    return jnp.transpose(s, (0, 3, 1, 2)).reshape(n, GLA_DV, GLA_KEY_WIDTH)


def _state_from_t(st):
    n = st.shape[0]
    return jnp.transpose(st.reshape(n, GLA_DV, GLA_HEADS, GLA_DK), (0, 2, 3, 1))


def _layer(x_prompt, x_sample, p_prompt, p_sample, cache_k, cache_v, cache_idx_k, state_gla,
           page_table, rel_bias, g_pre_mix, w_in, w_gate_up, b_gate, g_gla_out, w_out,
           g_post_mix, g_pre_ffn, w_ff_gate, w_ff_up, w_ff_down, g_post_ffn, w_ple_gate,
           w_ple_proj):
    bsz, seq, d = x_prompt.shape
    nseq, dec_seq, _ = x_sample.shape
    n_pool = cache_k.shape[0]
    past = page_table.shape[1] * PAGE_SIZE
    row2 = lambda g: g.reshape(1, -1)

    w_packed = _pack_w_in(w_in)
    wv_t = jnp.transpose(w_packed[:, SEG_V[0]:SEG_V[1]])
    wgu = jnp.concatenate(
        [w_gate_up, jnp.zeros((LANE - GLA_GATE_RANK, GLA_KEY_WIDTH), w_gate_up.dtype)],
        axis=0).astype(BF16)
    post_w = (w_out.astype(BF16), row2(g_post_mix), row2(g_pre_ffn), w_ff_gate.astype(BF16),
              w_ff_up.astype(BF16), w_ff_down.astype(BF16), row2(g_post_ffn),
              w_ple_gate.astype(BF16), w_ple_proj.astype(BF16))
    gla_w = (wgu, row2(b_gate), row2(g_gla_out))

    tp = bsz * seq
    q, kf, vf, kb, _, vt, iq, ikw, ikb, gin = _inproj(
        x_prompt.reshape(tp, d), row2(g_pre_mix), w_packed, wv_t, min(512, seq), seq)
    r3 = lambda a: a.reshape(bsz, seq, a.shape[-1])
    att = _prompt_attention(rel_bias, r3(q), r3(iq), r3(ikw), r3(kb), vt, r3(ikb),
                            min(TOPK_MAX, seq // 4))
    gla_chunk = min(64, seq)
    gla_o, st_p = _gla(r3(gin), jnp.zeros((bsz, GLA_DV, GLA_KEY_WIDTH), F32), *gla_w,
                       tile=min(512, seq), chunk=gla_chunk, n_valid=gla_chunk)
    y_p = _post(att.reshape(tp, ATT_WIDTH), gla_o.reshape(tp, GLA_WIDTH), x_prompt.reshape(tp, d),
                p_prompt.reshape(tp, -1), *post_w, min(256, tp))
    outs_p = (y_p.reshape(bsz, seq, d),
              kf.reshape(bsz, seq, ATT_HEADS, ATT_HEAD_DIM),
              vf.reshape(bsz, seq, ATT_HEADS, ATT_HEAD_DIM),
              r3(ikw)[..., :IDX_DIM],
              _state_from_t(st_p))

    ts = nseq * dec_seq
    q, kf, vf, kb, vb, _, iq, ikw, ikb, gin = _inproj(
        x_sample.reshape(ts, d), row2(g_pre_mix), w_packed, wv_t, min(512, ts), ts)
    rows = dec_seq * ATT_HEADS
    iq_rows = iq.reshape(nseq, rows, IDX_DIM)
    w_rows = ikw[:, IDX_DIM:IDX_DIM + IDX_HEADS].reshape(nseq, rows, 1)
    qc_rows = q.reshape(nseq, rows, ATT_HEAD_DIM)
    pad_to = lambda a, n: jnp.pad(a.reshape(nseq, dec_seq, a.shape[-1]),
                                  ((0, 0), (0, n - dec_seq), (0, 0)))
    new_page = lambda a: pad_to(a, PAGE_SIZE).reshape(nseq, PAGE_ROWS, ATT_HEAD_DIM)
    scores = _sample_scores(page_table, iq_rows, w_rows, cache_idx_k, dec_seq)
    att_s = _sample_attention(
        page_table, rel_bias, scores, iq_rows, w_rows, pad_to(ikb[:, :IDX_DIM], PAGE_SIZE),
        qc_rows, new_page(kb), new_page(vb),
        cache_k.reshape(n_pool, PAGE_ROWS, ATT_HEAD_DIM),
        cache_v.reshape(n_pool, PAGE_ROWS, ATT_HEAD_DIM), dec_seq,
        min(TOPK_MAX, (past + dec_seq) // 4))
    gla_s, st_s = _gla(pad_to(gin, NEW_PAD), _state_to_t(state_gla), *gla_w,
                       tile=NEW_PAD, chunk=NEW_PAD, n_valid=dec_seq)
    y_s = _post(att_s.reshape(ts, ATT_WIDTH), gla_s[:, :dec_seq].reshape(ts, GLA_WIDTH),
                x_sample.reshape(ts, d), p_sample.reshape(ts, -1), *post_w, min(256, ts))
    outs_s = (y_s.reshape(nseq, dec_seq, d),
              kf.reshape(nseq, dec_seq, ATT_HEADS, ATT_HEAD_DIM),
              vf.reshape(nseq, dec_seq, ATT_HEADS, ATT_HEAD_DIM),
              ikw[:, :IDX_DIM].reshape(nseq, dec_seq, IDX_DIM),
              _state_from_t(st_s))
    return outs_p, outs_s


def kernel(x_prompt, x_sample, p_prompt, p_sample, cache_k, cache_v, cache_idx_k, state_gla,
           page_table, rel_bias, g_pre_mix, w_in, w_gate_up, b_gate, g_gla_out, w_out,
           g_post_mix, g_pre_ffn, w_ff_gate, w_ff_up, w_ff_down, g_post_ffn, w_ple_gate,
           w_ple_proj):
    depth = w_in.shape[0]
    h_p, h_s = x_prompt, x_sample
    per_layer = []
    for i in range(depth):
        outs_p, outs_s = _layer(
            h_p, h_s, p_prompt[i], p_sample[i], cache_k[i], cache_v[i], cache_idx_k[i],
            state_gla[i], page_table, rel_bias, g_pre_mix[i], w_in[i], w_gate_up[i], b_gate[i],
            g_gla_out[i], w_out[i], g_post_mix[i], g_pre_ffn[i], w_ff_gate[i], w_ff_up[i],
            w_ff_down[i], g_post_ffn[i], w_ple_gate[i], w_ple_proj[i])
        h_p, h_s = outs_p[0], outs_s[0]
        per_layer.append(outs_p[1:] + outs_s[1:])
    stacked = [jnp.stack([lay[n] for lay in per_layer]) for n in range(8)]
    return (h_p, h_s, stacked[0], stacked[1], stacked[2], stacked[3],
            stacked[4], stacked[5], stacked[6], stacked[7])
```

```python
import functools
import math

import jax
import jax.numpy as jnp
from jax import lax
from jax.experimental import pallas as pl
from jax.experimental.pallas import tpu as pltpu

F32 = jnp.float32
BF16 = jnp.bfloat16

D_MODEL = 1024
ATT_HEADS = 8
ATT_HEAD_DIM = 64
ATT_WIDTH = ATT_HEADS * ATT_HEAD_DIM
IDX_HEADS = 8
IDX_DIM = 64
TOPK_MAX = 256
PAGE_SIZE = 128
GLA_HEADS = 4
GLA_DK = 64
GLA_DV = 128
GLA_KEY_WIDTH = GLA_HEADS * GLA_DK
GLA_WIDTH = GLA_HEADS * GLA_DV
GLA_GATE_RANK = 16
GLA_GATE_TAU = 16.0
REL_BUCKETS = 32
REL_MAX_DIST = 128
RMS_EPS = 1e-6
LOG2E = math.log2(math.e)

LANE = 128
SUBLANE = 8
BF16_ROWS = 16
MXU_WIDTH = 256
Q_TILE = MXU_WIDTH
K_CHUNK = MXU_WIDTH
HEAD_GROUP = MXU_WIDTH // ATT_HEAD_DIM
ACC_ROWS = ATT_HEAD_DIM + BF16_ROWS
VMEM_LIMIT = 56 * 1024 * 1024

SEG_Q = (0, 512)
SEG_K = (512, 1024)
SEG_V = (1024, 1536)
SEG_IQ = (1536, 2048)
SEG_IKW = (2048, 2176)
SEG_GLA = (2176, 3840)
PROJ_PACKED = 3840
GLA_IN_WIDTH = SEG_GLA[1] - SEG_GLA[0]

NT_DIMS = (((1,), (1,)), ((), ()))
TN_DIMS = (((0,), (0,)), ((), ()))


def _rms(x, g):
    return x * lax.rsqrt(jnp.mean(x * x, axis=-1, keepdims=True) + RMS_EPS) * g


def _inproj_body(x_ref, g_ref, w_ref, wvt_ref, q_ref, k_ref, v_ref, kb_ref, vb_ref, vt_ref, iq_ref,
                 ikw_ref, ikb_ref, gla_ref):
    a = _rms(x_ref[...], g_ref[...]).astype(BF16)

    def proj(seg):
        return jnp.dot(a, w_ref[:, seg[0]:seg[1]], preferred_element_type=F32)

    q_ref[...] = proj(SEG_Q).astype(BF16)
    kf = proj(SEG_K)
    k_ref[...] = kf
    kb_ref[...] = kf.astype(BF16)
    vf = proj(SEG_V)
    v_ref[...] = vf
    vb_ref[...] = vf.astype(BF16)
    vt_ref[0] = lax.dot_general(wvt_ref[...], a, NT_DIMS, preferred_element_type=F32).astype(BF16)
    iq_ref[...] = proj(SEG_IQ).astype(BF16)
    lane = lax.broadcasted_iota(jnp.int32, (1, LANE), 1)
    is_w = (lane >= IDX_DIM) & (lane < IDX_DIM + IDX_HEADS)
    ikw = proj(SEG_IKW) * jnp.where(is_w, IDX_HEADS ** -0.5, 1.0)
    ikw_ref[...] = ikw
    ikb_ref[...] = ikw.astype(BF16)
    gla_ref[...] = proj(SEG_GLA)


def _inproj(x, g, w_packed, wv_t, tm, seq):
    t = x.shape[0]
    tiles_per_seq = seq // tm
    row = lambda w: pl.BlockSpec((tm, w), lambda i: (i, 0))
    const = lambda s: pl.BlockSpec(s, lambda i: (0, 0))
    out_shape = (
        jax.ShapeDtypeStruct((t, ATT_WIDTH), BF16),
        jax.ShapeDtypeStruct((t, ATT_WIDTH), F32),
        jax.ShapeDtypeStruct((t, ATT_WIDTH), F32),
        jax.ShapeDtypeStruct((t, ATT_WIDTH), BF16),
        jax.ShapeDtypeStruct((t, ATT_WIDTH), BF16),
        jax.ShapeDtypeStruct((t // seq, ATT_WIDTH, seq), BF16),
        jax.ShapeDtypeStruct((t, ATT_WIDTH), BF16),
        jax.ShapeDtypeStruct((t, LANE), F32),
        jax.ShapeDtypeStruct((t, LANE), BF16),
        jax.ShapeDtypeStruct((t, GLA_IN_WIDTH), F32),
    )
    vt_spec = pl.BlockSpec((1, ATT_WIDTH, tm), lambda i: (i // tiles_per_seq, 0, i % tiles_per_seq))
    return pl.pallas_call(
        _inproj_body,
        grid=(t // tm,),
        in_specs=[row(D_MODEL), const((1, D_MODEL)), const((D_MODEL, PROJ_PACKED)),
                  const((ATT_WIDTH, D_MODEL))],
        out_specs=(row(ATT_WIDTH), row(ATT_WIDTH), row(ATT_WIDTH), row(ATT_WIDTH), row(ATT_WIDTH),
                   vt_spec, row(ATT_WIDTH), row(LANE), row(LANE), row(GLA_IN_WIDTH)),
        out_shape=out_shape,
        compiler_params=pltpu.CompilerParams(dimension_semantics=("arbitrary",),
                                             vmem_limit_bytes=VMEM_LIMIT),
        name="inproj",
    )(x, g, w_packed, wv_t)


def _log_sigmoid(z):
    return jnp.minimum(z, 0.0) - jnp.log1p(jnp.exp(-jnp.abs(z)))


def _gla_body(gin_ref, s0_ref, wgu_ref, bg_ref, gg_ref, o_ref, sout_ref, st_ref, *, chunk, n_valid):
    @pl.when(pl.program_id(1) == 0)
    def _():
        st_ref[...] = s0_ref[0]

    n_chunks = gin_ref.shape[1] // chunk
    ri = lax.broadcasted_iota(jnp.int32, (chunk, chunk), 0)
    ci = lax.broadcasted_iota(jnp.int32, (chunk, chunk), 1)
    causal = ri >= ci
    tri = causal.astype(F32)
    row_valid = lax.broadcasted_iota(jnp.int32, (chunk, 1), 0) < n_valid

    def one_chunk(c, carry):
        r0 = pl.multiple_of(c * chunk, chunk)
        blk = gin_ref[0, pl.ds(r0, chunk), :]
        gq = blk[:, 0:256]
        gk = blk[:, 256:512]
        gv = blk[:, 512:1024]
        glr = blk[:, 1024:1152]
        gout = blk[:, 1152:1664]
        z = jnp.dot(glr.astype(BF16), wgu_ref[...], preferred_element_type=F32) + bg_ref[...]
        la = _log_sigmoid(z) / GLA_GATE_TAU
        if n_valid < chunk:
            la = jnp.where(row_valid, la, 0.0)
        b = jnp.dot(tri, la, precision=lax.Precision.HIGHEST, preferred_element_type=F32)
        b_last = b[chunk - 1:chunk, :]
        qt = gq * jnp.exp(b)
        kt = gk * jnp.exp(-b)
        kd = gk * jnp.exp(b_last - b)
        dec = jnp.exp(b_last)
        st = st_ref[...]
        o_parts = []
        st_parts = []
        for h in range(GLA_HEADS):
            ks = slice(GLA_DK * h, GLA_DK * (h + 1))
            vs = slice(GLA_DV * h, GLA_DV * (h + 1))
            qh = qt[:, ks].astype(BF16)
            kh = kt[:, ks].astype(BF16)
            kdh = kd[:, ks].astype(BF16)
            vh = gv[:, vs].astype(BF16)
            sth = st[:, ks]
            attn = lax.dot_general(qh, kh, NT_DIMS, preferred_element_type=F32)
            attn = jnp.where(causal, attn, 0.0)
            o = jnp.dot(attn.astype(BF16), vh, preferred_element_type=F32)
            o = o + lax.dot_general(qh, sth.astype(BF16), NT_DIMS, preferred_element_type=F32)
            upd = lax.dot_general(vh, kdh, TN_DIMS, preferred_element_type=F32)
            st_parts.append(dec[:, ks] * sth + upd)
            o_parts.append(_rms(o, gg_ref[...]))
        st_ref[...] = jnp.concatenate(st_parts, axis=1)
        o_all = jnp.concatenate(o_parts, axis=1)
        gate = gout * jax.nn.sigmoid(gout)
        o_ref[0, pl.ds(r0, chunk), :] = (o_all * gate).astype(BF16)
        return carry

    lax.fori_loop(0, n_chunks, one_chunk, 0)
    sout_ref[0] = st_ref[...]


def _gla(gin, s0_t, wgu, bg, gg, *, tile, chunk, n_valid):
    nseq, t, _ = gin.shape
    body = functools.partial(_gla_body, chunk=chunk, n_valid=n_valid)
    return pl.pallas_call(
        body,
        grid=(nseq, t // tile),
        in_specs=[
            pl.BlockSpec((1, tile, GLA_IN_WIDTH), lambda b, j: (b, j, 0)),
            pl.BlockSpec((1, GLA_DV, GLA_KEY_WIDTH), lambda b, j: (b, 0, 0)),
            pl.BlockSpec((LANE, GLA_KEY_WIDTH), lambda b, j: (0, 0)),
            pl.BlockSpec((1, GLA_KEY_WIDTH), lambda b, j: (0, 0)),
            pl.BlockSpec((1, GLA_DV), lambda b, j: (0, 0)),
        ],
        out_specs=(
            pl.BlockSpec((1, tile, GLA_WIDTH), lambda b, j: (b, j, 0)),
            pl.BlockSpec((1, GLA_DV, GLA_KEY_WIDTH), lambda b, j: (b, 0, 0)),
        ),
        out_shape=(
            jax.ShapeDtypeStruct((nseq, t, GLA_WIDTH), BF16),
            jax.ShapeDtypeStruct((nseq, GLA_DV, GLA_KEY_WIDTH), F32),
        ),
        scratch_shapes=[pltpu.VMEM((GLA_DV, GLA_KEY_WIDTH), F32)],
        compiler_params=pltpu.CompilerParams(dimension_semantics=("arbitrary", "arbitrary"),
                                             vmem_limit_bytes=VMEM_LIMIT),
        name="gla",
    )(gin, s0_t, wgu, bg, gg)


def _rel_bucket(dist):
    n = jnp.maximum(dist, 0)
    exact = REL_BUCKETS // 2
    nf = jnp.maximum(n, 1).astype(F32)
    large = exact + (jnp.log(nf / exact) / math.log(REL_MAX_DIST / exact)
                     * (REL_BUCKETS - exact)).astype(jnp.int32)
    large = jnp.minimum(large, REL_BUCKETS - 1)
    return jnp.where(n < exact, n, large)


def _select_topk(sc_ref, n_queries, n_chunks, n_allowed, topk, keys_axis, max_iter=18):
    neg_inf = -jnp.inf
    n_acc = 4 if keys_axis == 0 else K_CHUNK // LANE
    if keys_axis == 1:
        part_shape = (n_queries, LANE)
        q_shape = (n_queries, 1)
        chunk_shape = (n_queries, K_CHUNK)
        load = lambda k0: sc_ref[:, pl.ds(k0, K_CHUNK)]
        pieces = lambda x: [x[:, j * LANE:(j + 1) * LANE] for j in range(K_CHUNK // LANE)]
        last = lambda r: r[:, K_CHUNK - 1:K_CHUNK]
    else:
        part_shape = (SUBLANE, n_queries)
        q_shape = (1, n_queries)
        chunk_shape = (K_CHUNK, n_queries)
        load = lambda k0: sc_ref[pl.ds(k0, K_CHUNK), :]
        pieces = lambda x: [x[j * SUBLANE:(j + 1) * SUBLANE] for j in range(K_CHUNK // SUBLANE)]
        last = lambda r: r[K_CHUNK - 1:K_CHUNK, :]

    def store(k0, val):
        if keys_axis == 1:
            sc_ref[:, pl.ds(k0, K_CHUNK)] = val
        else:
            sc_ref[pl.ds(k0, K_CHUNK), :] = val

    def sweep(fns, inits, reds, combines):
        n_stat = len(fns)

        def body(c, accs):
            x = load(pl.multiple_of(c * K_CHUNK, K_CHUNK))
            accs = [list(a) for a in accs]
            for n, piece in enumerate(pieces(x)):
                for s in range(n_stat):
                    accs[s][n % n_acc] = fns[s](accs[s][n % n_acc], piece)
            return tuple(tuple(a) for a in accs)

        init = tuple(tuple(jnp.full(part_shape, inits[s], F32) for _ in range(n_acc))
                     for s in range(n_stat))
        accs = lax.fori_loop(0, n_chunks, body, init)
        out = []
        for s in range(n_stat):
            acc = accs[s][0]
            for other in accs[s][1:]:
                acc = combines[s](acc, other)
            out.append(reds[s](acc, axis=keys_axis, keepdims=True))
        return out

    def count_ge(t):
        tb = jnp.broadcast_to(t, part_shape)
        return sweep([lambda a, x: a + jnp.where(x >= tb, 1.0, 0.0)], [0.0], [jnp.sum],
                     [jnp.add])[0]

    def count_gt(t):
        tb = jnp.broadcast_to(t, part_shape)
        return sweep([lambda a, x: a + jnp.where(x > tb, 1.0, 0.0)], [0.0], [jnp.sum],
                     [jnp.add])[0]

    def max_below(t):
        tb = jnp.broadcast_to(t, part_shape)
        return sweep([lambda a, x: jnp.maximum(a, jnp.where(x < tb, x, neg_inf))], [neg_inf],
                     [jnp.max], [jnp.maximum])[0]

    row_max, row_min = sweep(
        [jnp.maximum, lambda a, x: jnp.minimum(a, jnp.where(x == neg_inf, jnp.inf, x))],
        [neg_inf, jnp.inf], [jnp.max, jnp.min], [jnp.maximum, jnp.minimum])

    k = float(topk)
    take_all = n_allowed <= k
    lo0 = row_min
    hi0 = row_max + jnp.maximum(jnp.abs(row_max) * 1e-6, 1e-30)
    t0 = jnp.full(q_shape, jnp.finfo(F32).min, F32)
    done0 = jnp.where(take_all, 1.0, 0.0)

    def bis_cond(st):
        it, lo, hi, t, done, stuck = st
        return (it < max_iter) & (jnp.min(jnp.maximum(done, stuck)) < 0.5)

    def bis_body(st):
        it, lo, hi, t, done, stuck = st
        mid = lo + (hi - lo) * 0.5
        no_room = (mid <= lo) | (mid >= hi)
        c = count_ge(mid)
        hit = (c == k) & (done < 0.5)
        t = jnp.where(hit, mid, t)
        done = jnp.where(hit, 1.0, done)
        lo = jnp.where(c > k, mid, lo)
        hi = jnp.where(c < k, mid, hi)
        stuck = jnp.where(no_room, 1.0, stuck)
        return it + 1, lo, hi, t, done, stuck

    _, lo, hi, t, done, _ = lax.while_loop(
        bis_cond, bis_body, (jnp.int32(0), lo0, hi0, t0, done0, jnp.zeros(q_shape, F32)))

    all_done = jnp.min(done) > 0.5

    def write(fn):
        def body(c, carry):
            k0 = pl.multiple_of(c * K_CHUNK, K_CHUNK)
            sel, carry = fn(load(k0), carry)
            store(k0, jnp.where(sel, 0.0, neg_inf))
            return carry
        return body

    @pl.when(all_done)
    def _():
        tb = jnp.broadcast_to(t, chunk_shape)
        lax.fori_loop(0, n_chunks, write(lambda x, cr: (x >= tb, cr)), 0)

    @pl.when(jnp.logical_not(all_done))
    def _():
        def snap_cond(st):
            return jnp.min(st[2]) < 0.5

        def snap_body(st):
            hi_, t_, done_ = st
            v = max_below(hi_)
            c = count_ge(v)
            ok = (c >= k) & (done_ < 0.5)
            t_ = jnp.where(ok, v, t_)
            hi_ = jnp.where((done_ < 0.5) & jnp.logical_not(ok), v, hi_)
            done_ = jnp.where(ok, 1.0, done_)
            return hi_, t_, done_

        _, tv, _ = lax.while_loop(snap_cond, snap_body, (hi, t, done))
        quota = k - count_gt(tv)
        tb = jnp.broadcast_to(tv, chunk_shape)
        qb = jnp.broadcast_to(quota, chunk_shape)
        ui = lax.broadcasted_iota(jnp.int32, (K_CHUNK, K_CHUNK), 0)
        uj = lax.broadcasted_iota(jnp.int32, (K_CHUNK, K_CHUNK), 1)
        prefix = ((ui <= uj) if keys_axis == 1 else (ui >= uj)).astype(BF16)

        def tie_fn(x, seen):
            tie = x == tb
            if keys_axis == 1:
                rank = jnp.dot(tie.astype(BF16), prefix, preferred_element_type=F32) + seen
            else:
                rank = jnp.dot(prefix, tie.astype(BF16), preferred_element_type=F32) + seen
            sel = (x > tb) | (tie & (rank <= qb))
            return sel, last(rank)

        lax.fori_loop(0, n_chunks, write(tie_fn), jnp.zeros(q_shape, F32))


def _head_lane_mask(h, width):
    lane = lax.broadcasted_iota(jnp.int32, (1, width), 1)
    lo = (h % HEAD_GROUP) * ATT_HEAD_DIM
    return (lane >= lo) & (lane < lo + ATT_HEAD_DIM)


def _pattn_body(rb_ref, q_ref, iq_ref, w_ref, k_ref, vt_ref, ik_ref, o_ref,
                sc_ref, qp_ref, iqp_ref, bias_ref, m_ref, alpha_ref, acc_ref, x_ref, p_ref,
                *, topk):
    i = pl.program_id(1)
    n_chunks = i + 1

    @pl.when((pl.program_id(0) == 0) & (i == 0))
    def _():
        kk = lax.broadcasted_iota(jnp.int32, (K_CHUNK, Q_TILE), 0)
        qq = lax.broadcasted_iota(jnp.int32, (K_CHUNK, Q_TILE), 1)
        for delta in range(2):
            bucket = _rel_bucket(qq - kk + Q_TILE * delta)
            for h in range(ATT_HEADS):
                acc = jnp.zeros((K_CHUNK, Q_TILE), F32)
                for bk in range(REL_BUCKETS - 1):
                    acc = jnp.where(bucket == bk, rb_ref[bk, h] - rb_ref[REL_BUCKETS - 1, h], acc)
                bias_ref[delta, h] = acc * LOG2E

    q = q_ref[0]
    iq = iq_ref[0]
    for h in range(ATT_HEADS):
        g = h // HEAD_GROUP
        qp_ref[h] = jnp.where(_head_lane_mask(h, MXU_WIDTH),
                              q[:, MXU_WIDTH * g:MXU_WIDTH * (g + 1)], 0)
        iqp_ref[h] = jnp.concatenate(
            [iq[:, IDX_DIM * h:IDX_DIM * (h + 1)], jnp.zeros((Q_TILE, LANE - IDX_DIM), BF16)], axis=1)
    w_t = jnp.transpose(w_ref[0])[IDX_DIM:IDX_DIM + IDX_HEADS, :]

    q_pos = i * Q_TILE + lax.broadcasted_iota(jnp.int32, (1, Q_TILE), 1)

    def score_chunk(c, carry, *, diagonal):
        k0 = pl.multiple_of(c * K_CHUNK, K_CHUNK)
        ikc = ik_ref[0, pl.ds(k0, K_CHUNK), :]
        acc = jnp.zeros((K_CHUNK, Q_TILE), F32)
        for h in range(IDX_HEADS):
            s = lax.dot_general(ikc, iqp_ref[h], NT_DIMS, preferred_element_type=F32)
            acc = acc + jnp.maximum(s, 0.0) * w_t[h:h + 1, :]
        if diagonal:
            k_pos = k0 + lax.broadcasted_iota(jnp.int32, (K_CHUNK, 1), 0)
            acc = jnp.where(k_pos <= q_pos, acc, -jnp.inf)
        sc_ref[pl.ds(k0, K_CHUNK), :] = acc
        return carry

    lax.fori_loop(0, i, functools.partial(score_chunk, diagonal=False), 0)
    score_chunk(i, 0, diagonal=True)

    _select_topk(sc_ref, Q_TILE, n_chunks, (q_pos + 1).astype(F32), topk, keys_axis=0)

    m_ref[...] = jnp.full(m_ref.shape, -1e30, F32)
    acc_ref[...] = jnp.zeros(acc_ref.shape, F32)
    ones_rows = jnp.ones((BF16_ROWS, K_CHUNK), BF16)
    row_block = 4 * SUBLANE

    def attend_chunk(c, carry, *, near):
        k0 = pl.multiple_of(c * K_CHUNK, K_CHUNK)
        mb = sc_ref[pl.ds(k0, K_CHUNK), :]
        for h in range(ATT_HEADS):
            g = h // HEAD_GROUP
            kc = k_ref[0, pl.ds(k0, K_CHUNK), MXU_WIDTH * g:MXU_WIDTH * (g + 1)]
            s = lax.dot_general(kc, qp_ref[h], NT_DIMS, preferred_element_type=F32)
            mx = jnp.full((SUBLANE, Q_TILE), -jnp.inf, F32)
            for r in range(K_CHUNK // row_block):
                rs = slice(r * row_block, (r + 1) * row_block)
                xb = s[rs] + mb[rs]
                if near:
                    xb = xb + bias_ref[i - c, h, rs, :]
                x_ref[h, rs, :] = xb
                for u in range(row_block // SUBLANE):
                    mx = jnp.maximum(mx, xb[u * SUBLANE:(u + 1) * SUBLANE])
            m_old = m_ref[h]
            m_new = jnp.maximum(m_old, jnp.max(mx, axis=0, keepdims=True))
            alpha_ref[h] = jnp.exp2(m_old - m_new)
            m_ref[h] = m_new
        for h in range(ATT_HEADS):
            p_ref[h] = jnp.exp2(x_ref[h] - m_ref[h]).astype(BF16)
        for h in range(ATT_HEADS):
            vt = vt_ref[0, ATT_HEAD_DIM * h:ATT_HEAD_DIM * (h + 1), pl.ds(k0, K_CHUNK)]
            lhs = jnp.concatenate([vt, ones_rows], axis=0)
            acc_ref[h] = alpha_ref[h] * acc_ref[h] + jnp.dot(lhs, p_ref[h],
                                                             preferred_element_type=F32)
        return carry

    n_far = jnp.maximum(i - 1, 0)
    lax.fori_loop(0, n_far, functools.partial(attend_chunk, near=False), 0)
    lax.fori_loop(n_far, n_chunks, functools.partial(attend_chunk, near=True), 0)

    out_t = jnp.concatenate(
        [acc_ref[h, 0:ATT_HEAD_DIM, :] / acc_ref[h, ATT_HEAD_DIM:ATT_HEAD_DIM + 1, :]
         for h in range(ATT_HEADS)], axis=0)
    o_ref[0] = jnp.transpose(out_t).astype(BF16)


def _prompt_attention(rel_bias, q, iq, ikw, kb, vt, ikb, topk):
    bsz, s, _ = q.shape
    tile = lambda w: pl.BlockSpec((1, Q_TILE, w), lambda b, i: (b, i, 0))
    per_batch = lambda shape: pl.BlockSpec((1,) + shape, lambda b, i: (b, 0, 0),
                                           pipeline_mode=pl.Buffered(1))
    body = functools.partial(_pattn_body, topk=topk)
    return pl.pallas_call(
        body,
        grid=(bsz, s // Q_TILE),
        in_specs=[pl.BlockSpec(memory_space=pltpu.SMEM),
                  tile(ATT_WIDTH), tile(ATT_WIDTH), tile(LANE),
                  per_batch((s, ATT_WIDTH)), per_batch((ATT_WIDTH, s)), per_batch((s, LANE))],
        out_specs=tile(ATT_WIDTH),
        out_shape=jax.ShapeDtypeStruct((bsz, s, ATT_WIDTH), BF16),
        scratch_shapes=[
            pltpu.VMEM((s, Q_TILE), F32),
            pltpu.VMEM((ATT_HEADS, Q_TILE, MXU_WIDTH), BF16),
            pltpu.VMEM((IDX_HEADS, Q_TILE, LANE), BF16),
            pltpu.VMEM((2, ATT_HEADS, K_CHUNK, Q_TILE), F32),
            pltpu.VMEM((ATT_HEADS, 1, Q_TILE), F32),
            pltpu.VMEM((ATT_HEADS, 1, Q_TILE), F32),
            pltpu.VMEM((ATT_HEADS, ACC_ROWS, Q_TILE), F32),
            pltpu.VMEM((ATT_HEADS, K_CHUNK, Q_TILE), F32),
            pltpu.VMEM((ATT_HEADS, K_CHUNK, Q_TILE), BF16),
        ],
        compiler_params=pltpu.CompilerParams(dimension_semantics=("arbitrary", "arbitrary"),
                                             vmem_limit_bytes=VMEM_LIMIT),
        name="prompt_attention",
    )(rel_bias, q, iq, ikw, kb, vt, ikb)


SCORE_PAGES = 16
ATTN_PAGES = 8
NEW_PAD = 16


def _head_sums(y, dec_seq):
    return jnp.concatenate(
        [jnp.sum(y[IDX_HEADS * t:IDX_HEADS * (t + 1)], axis=0, keepdims=True)
         for t in range(dec_seq)], axis=0)


def _sscore_body(pt_ref, iq_ref, w_ref, *rest, dec_seq):
    del pt_ref
    pages = rest[:SCORE_PAGES]
    out_ref = rest[SCORE_PAGES]
    iq = iq_ref[0]
    w = w_ref[0]
    for m in range(SCORE_PAGES):
        s = jnp.dot(iq, pages[m][0].astype(BF16), preferred_element_type=F32)
        out_ref[0, :, m * PAGE_SIZE:(m + 1) * PAGE_SIZE] = _head_sums(jnp.maximum(s, 0.0) * w,
                                                                      dec_seq)


def _sample_scores(page_table, iq_rows, w_rows, cache_idx_t, dec_seq):
    nseq, n_pages = page_table.shape
    steps = n_pages // SCORE_PAGES
    rows = dec_seq * IDX_HEADS

    def page_spec(m):
        return pl.BlockSpec((1, IDX_DIM, PAGE_SIZE),
                            lambda b, j, pt: (pt[b, j * SCORE_PAGES + m], 0, 0))

    grid_spec = pltpu.PrefetchScalarGridSpec(
        num_scalar_prefetch=1,
        grid=(nseq, steps),
        in_specs=[pl.BlockSpec((1, rows, IDX_DIM), lambda b, j, pt: (b, 0, 0)),
                  pl.BlockSpec((1, rows, 1), lambda b, j, pt: (b, 0, 0))]
                 + [page_spec(m) for m in range(SCORE_PAGES)],
        out_specs=pl.BlockSpec((1, dec_seq, SCORE_PAGES * PAGE_SIZE), lambda b, j, pt: (b, 0, j)),
    )
    return pl.pallas_call(
        functools.partial(_sscore_body, dec_seq=dec_seq),
        grid_spec=grid_spec,
        out_shape=jax.ShapeDtypeStruct((nseq, dec_seq, n_pages * PAGE_SIZE), F32),
        compiler_params=pltpu.CompilerParams(dimension_semantics=("arbitrary", "arbitrary"),
                                             vmem_limit_bytes=VMEM_LIMIT),
        name="sample_scores",
    )(page_table, iq_rows, w_rows, *([cache_idx_t] * SCORE_PAGES))


def _sattn_body(pt_ref, rb_ref, sc_in_ref, iq_ref, w_ref, iknew_ref, qp_ref, knew_ref, vnew_ref,
                *rest, dec_seq, past, topk):
    del pt_ref
    kpages = rest[:ATTN_PAGES]
    vpages = rest[ATTN_PAGES:2 * ATTN_PAGES]
    o_ref = rest[2 * ATTN_PAGES]
    sc_ref, mb_ref, m_ref, l_ref, acc_ref = rest[2 * ATTN_PAGES + 1:]
    j = pl.program_id(1)
    n_steps = pl.num_programs(1)
    rows = dec_seq * ATT_HEADS
    n_chunks = past // K_CHUNK + 1
    row_t = lax.broadcasted_iota(jnp.int32, (rows, 1), 0) // ATT_HEADS
    row_h = lax.broadcasted_iota(jnp.int32, (rows, 1), 0) % ATT_HEADS

    @pl.when(j == 0)
    def _():
        s = jnp.dot(iq_ref[0], iknew_ref[0], preferred_element_type=F32)
        new_sc = _head_sums(jnp.maximum(s, 0.0) * w_ref[0], dec_seq)
        t_idx = lax.broadcasted_iota(jnp.int32, (dec_seq, K_CHUNK), 0)
        k_idx = lax.broadcasted_iota(jnp.int32, (dec_seq, K_CHUNK), 1)
        padded = jnp.concatenate(
            [new_sc, jnp.zeros((dec_seq, K_CHUNK - PAGE_SIZE), F32)], axis=1)
        sc_ref[:, 0:past] = sc_in_ref[0]
        sc_ref[:, past:past + K_CHUNK] = jnp.where(k_idx <= t_idx, padded, -jnp.inf)
        n_allowed = (past + 1 + lax.broadcasted_iota(jnp.int32, (dec_seq, 1), 0)).astype(F32)
        _select_topk(sc_ref, dec_seq, n_chunks, n_allowed, topk, keys_axis=1)
        for t in range(dec_seq):
            mb_ref[ATT_HEADS * t:ATT_HEADS * (t + 1), :] = jnp.broadcast_to(
                sc_ref[t:t + 1, :], (ATT_HEADS, past + K_CHUNK))
        m_ref[...] = jnp.full(m_ref.shape, -1e30, F32)
        l_ref[...] = jnp.zeros(l_ref.shape, F32)
        acc_ref[...] = jnp.zeros(acc_ref.shape, F32)

    qp = qp_ref[0]

    def near_bias(dist):
        bucket = _rel_bucket(dist)
        acc = jnp.zeros(dist.shape, F32)
        for bk in range(REL_BUCKETS - 1):
            per_row = jnp.zeros((rows, 1), F32)
            for h in range(ATT_HEADS):
                per_row = jnp.where(row_h == h, rb_ref[bk, h] - rb_ref[REL_BUCKETS - 1, h], per_row)
            acc = jnp.where(bucket == bk, per_row, acc)
        return acc * LOG2E

    def attend_pages(k_ts, v_ts, mb, last_bias):
        logit = jnp.concatenate(
            [jnp.dot(qp, k_t, preferred_element_type=F32) for k_t in k_ts], axis=1) + mb
        n = len(k_ts)
        if n > 1:
            last_bias = jnp.concatenate(
                [jnp.zeros((rows, (n - 1) * PAGE_SIZE), F32), last_bias], axis=1)
        logit = logit + last_bias
        m_old = m_ref[...]
        m_new = jnp.maximum(m_old, jnp.max(logit, axis=1, keepdims=True))
        alpha = jnp.exp2(m_old - m_new)
        p = jnp.exp2(logit - m_new).astype(BF16)
        l_ref[...] = alpha * l_ref[...] + jnp.sum(p.astype(F32), axis=1, keepdims=True)
        acc = alpha * acc_ref[...]
        for m, v_t in enumerate(v_ts):
            acc = acc + lax.dot_general(p[:, m * PAGE_SIZE:(m + 1) * PAGE_SIZE], v_t, NT_DIMS,
                                        preferred_element_type=F32)
        acc_ref[...] = acc
        m_ref[...] = m_new

    is_last = j == n_steps - 1
    off = lax.broadcasted_iota(jnp.int32, (rows, PAGE_SIZE), 1)
    k0 = pl.multiple_of(j * (ATTN_PAGES * PAGE_SIZE), ATTN_PAGES * PAGE_SIZE)
    last_bias = lax.cond(is_last, lambda: near_bias(PAGE_SIZE + row_t - off),
                         lambda: jnp.zeros((rows, PAGE_SIZE), F32))
    attend_pages([kp[0].astype(BF16) for kp in kpages], [vp[0].astype(BF16) for vp in vpages],
                 mb_ref[:, pl.ds(k0, ATTN_PAGES * PAGE_SIZE)], last_bias)

    @pl.when(is_last)
    def _():
        attend_pages([knew_ref[0]], [vnew_ref[0]], mb_ref[:, past:past + PAGE_SIZE],
                     near_bias(row_t - off))
        res = acc_ref[...] / l_ref[...]
        lane_h = lax.broadcasted_iota(jnp.int32, (rows, ATT_WIDTH), 1) // ATT_HEAD_DIM
        res = jnp.where(lane_h == row_h, res, 0.0)
        o_ref[0] = _head_sums(res, dec_seq).astype(BF16)


def _sample_attention(page_table, rel_bias, scores, iq_rows, w_rows, iknew_t, qp_rows, knew_t,
                      vnew_t, cache_k_t, cache_v_t, dec_seq, topk):
    nseq, n_pages = page_table.shape
    past = n_pages * PAGE_SIZE
    steps = n_pages // ATTN_PAGES
    rows = dec_seq * ATT_HEADS

    def page_spec(m):
        return pl.BlockSpec((1, ATT_WIDTH, PAGE_SIZE),
                            lambda b, j, pt: (pt[b, j * ATTN_PAGES + m], 0, 0))

    per_seq = lambda shape: pl.BlockSpec((1,) + shape, lambda b, j, pt: (b, 0, 0))
    grid_spec = pltpu.PrefetchScalarGridSpec(
        num_scalar_prefetch=1,
        grid=(nseq, steps),
        in_specs=[pl.BlockSpec(memory_space=pltpu.SMEM),
                  per_seq((dec_seq, past)), per_seq((rows, IDX_DIM)), per_seq((rows, 1)),
                  per_seq((IDX_DIM, PAGE_SIZE)), per_seq((rows, ATT_WIDTH)),
                  per_seq((ATT_WIDTH, PAGE_SIZE)), per_seq((ATT_WIDTH, PAGE_SIZE))]
                 + [page_spec(m) for m in range(ATTN_PAGES)] * 2,
        out_specs=per_seq((dec_seq, ATT_WIDTH)),
        scratch_shapes=[
            pltpu.VMEM((dec_seq, past + K_CHUNK), F32),
            pltpu.VMEM((rows, past + K_CHUNK), F32),
            pltpu.VMEM((rows, 1), F32),
            pltpu.VMEM((rows, 1), F32),
            pltpu.VMEM((rows, ATT_WIDTH), F32),
        ],
    )
    return pl.pallas_call(
        functools.partial(_sattn_body, dec_seq=dec_seq, past=past, topk=topk),
        grid_spec=grid_spec,
        out_shape=jax.ShapeDtypeStruct((nseq, dec_seq, ATT_WIDTH), BF16),
        compiler_params=pltpu.CompilerParams(dimension_semantics=("arbitrary", "arbitrary"),
                                             vmem_limit_bytes=VMEM_LIMIT),
        name="sample_attention",
    )(page_table, rel_bias, scores, iq_rows, w_rows, iknew_t, qp_rows, knew_t, vnew_t,
      *([cache_k_t] * ATTN_PAGES), *([cache_v_t] * ATTN_PAGES))


FF_CHUNK = 256


def _post_body(att_ref, gla_ref, x_ref, p_ref, wo_ref, gpm_ref, gpf_ref, wg_ref, wu_ref, wd_ref,
               gpo_ref, wpg_ref, wpp_ref, o_ref, *, d_ff):
    wo = wo_ref
    mixed = jnp.dot(att_ref[...], wo[0:ATT_WIDTH, :], preferred_element_type=F32)
    mixed = mixed + jnp.dot(gla_ref[...], wo[ATT_WIDTH:ATT_WIDTH + GLA_WIDTH, :],
                            preferred_element_type=F32)
    h = x_ref[...] + _rms(mixed, gpm_ref[...])
    f = _rms(h, gpf_ref[...]).astype(BF16)
    ff = jnp.zeros(h.shape, F32)
    for c in range(d_ff // FF_CHUNK):
        cs = slice(c * FF_CHUNK, (c + 1) * FF_CHUNK)
        gate = jnp.dot(f, wg_ref[:, cs], preferred_element_type=F32)
        up = jnp.dot(f, wu_ref[:, cs], preferred_element_type=F32)
        act = (gate * jax.nn.sigmoid(gate) * up).astype(BF16)
        ff = ff + jnp.dot(act, wd_ref[cs, :], preferred_element_type=F32)
    h = h + _rms(ff, gpo_ref[...])
    gate = jax.nn.sigmoid(jnp.dot(h.astype(BF16), wpg_ref[...], preferred_element_type=F32))
    ple = jnp.dot(p_ref[...].astype(BF16), wpp_ref[...], preferred_element_type=F32)
    o_ref[...] = h + gate * ple


def _post(att, gla, x, p, wo, gpm, gpf, wg, wu, wd, gpo, wpg, wpp, tm):
    t = x.shape[0]
    d_ff = wg.shape[1]
    row = lambda w: pl.BlockSpec((tm, w), lambda i: (i, 0))
    const = lambda a: pl.BlockSpec(a.shape, lambda i: (0, 0), pipeline_mode=pl.Buffered(1))
    return pl.pallas_call(
        functools.partial(_post_body, d_ff=d_ff),
        grid=(t // tm,),
        in_specs=[row(ATT_WIDTH), row(GLA_WIDTH), row(D_MODEL), row(p.shape[1]),
                  const(wo), const(gpm), const(gpf), const(wg), const(wu), const(wd),
                  const(gpo), const(wpg), const(wpp)],
        out_specs=row(D_MODEL),
        out_shape=jax.ShapeDtypeStruct((t, D_MODEL), F32),
        compiler_params=pltpu.CompilerParams(dimension_semantics=("arbitrary",),
                                             vmem_limit_bytes=VMEM_LIMIT),
        name="post",
    )(att, gla, x, p, wo, gpm, gpf, wg, wu, wd, gpo, wpg, wpp)


def _pack_w_in(w_in):
    offs = [0]
    for wdt in (ATT_WIDTH, ATT_WIDTH, ATT_WIDTH, IDX_HEADS * IDX_DIM, IDX_DIM, IDX_HEADS,
                GLA_KEY_WIDTH, GLA_KEY_WIDTH, GLA_WIDTH, GLA_GATE_RANK, GLA_WIDTH):
        offs.append(offs[-1] + wdt)
    aq, ak, av, iq, ik, iw, gq, gk, gv, glr, gout = [w_in[:, offs[n]:offs[n + 1]] for n in range(11)]
    d = w_in.shape[0]
    zeros = lambda n: jnp.zeros((d, n), w_in.dtype)
    packed = jnp.concatenate([
        aq * (ATT_HEAD_DIM ** -0.5 * LOG2E), ak, av, iq * (IDX_DIM ** -0.5),
        ik, iw, zeros(LANE - IDX_DIM - IDX_HEADS),
        gq * (GLA_DK ** -0.5), gk, gv, glr, zeros(LANE - GLA_GATE_RANK), gout], axis=1)
    return packed.astype(BF16)


def _state_to_t(s):
    n = s.shape[0]
    return jnp.transpose(s, (0, 3, 1, 2)).reshape(n, GLA_DV, GLA_KEY_WIDTH)


def _state_from_t(st):
    n = st.shape[0]
    return jnp.transpose(st.reshape(n, GLA_DV, GLA_HEADS, GLA_DK), (0, 2, 3, 1))


def _layer(x_prompt, x_sample, p_prompt, p_sample, cache_k, cache_v, cache_idx_k, state_gla,
           page_table, rel_bias, g_pre_mix, w_in, w_gate_up, b_gate, g_gla_out, w_out,
           g_post_mix, g_pre_ffn, w_ff_gate, w_ff_up, w_ff_down, g_post_ffn, w_ple_gate,
           w_ple_proj):
    bsz, seq, d = x_prompt.shape
    nseq, dec_seq, _ = x_sample.shape
    n_pool = cache_k.shape[0]
    past = page_table.shape[1] * PAGE_SIZE
    row2 = lambda g: g.reshape(1, -1)

    w_packed = _pack_w_in(w_in)
    wv_t = jnp.transpose(w_packed[:, SEG_V[0]:SEG_V[1]])
    wgu = jnp.concatenate(
        [w_gate_up, jnp.zeros((LANE - GLA_GATE_RANK, GLA_KEY_WIDTH), w_gate_up.dtype)],
        axis=0).astype(BF16)
    post_w = (w_out.astype(BF16), row2(g_post_mix), row2(g_pre_ffn), w_ff_gate.astype(BF16),
              w_ff_up.astype(BF16), w_ff_down.astype(BF16), row2(g_post_ffn),
              w_ple_gate.astype(BF16), w_ple_proj.astype(BF16))
    gla_w = (wgu, row2(b_gate), row2(g_gla_out))

    tp = bsz * seq
    q, kf, vf, kb, _, vt, iq, ikw, ikb, gin = _inproj(
        x_prompt.reshape(tp, d), row2(g_pre_mix), w_packed, wv_t, min(512, seq), seq)
    r3 = lambda a: a.reshape(bsz, seq, a.shape[-1])
    att = _prompt_attention(rel_bias, r3(q), r3(iq), r3(ikw), r3(kb), vt, r3(ikb),
                            min(TOPK_MAX, seq // 4))
    gla_chunk = min(64, seq)
    gla_o, st_p = _gla(r3(gin), jnp.zeros((bsz, GLA_DV, GLA_KEY_WIDTH), F32), *gla_w,
                       tile=min(512, seq), chunk=gla_chunk, n_valid=gla_chunk)
    y_p = _post(att.reshape(tp, ATT_WIDTH), gla_o.reshape(tp, GLA_WIDTH), x_prompt.reshape(tp, d),
                p_prompt.reshape(tp, -1), *post_w, min(256, tp))
    outs_p = (y_p.reshape(bsz, seq, d),
              kf.reshape(bsz, seq, ATT_HEADS, ATT_HEAD_DIM),
              vf.reshape(bsz, seq, ATT_HEADS, ATT_HEAD_DIM),
              r3(ikw)[..., :IDX_DIM],
              _state_from_t(st_p))

    ts = nseq * dec_seq
    q, kf, vf, kb, vb, _, iq, ikw, ikb, gin = _inproj(
        x_sample.reshape(ts, d), row2(g_pre_mix), w_packed, wv_t, min(512, ts), ts)
    rows = dec_seq * ATT_HEADS
    iq_rows = iq.reshape(nseq, rows, IDX_DIM)
    w_rows = ikw[:, IDX_DIM:IDX_DIM + IDX_HEADS].reshape(nseq, rows, 1)
    head_of_col = jnp.arange(ATT_WIDTH, dtype=jnp.int32) // ATT_HEAD_DIM
    head_mask = head_of_col[None, :] == jnp.arange(ATT_HEADS, dtype=jnp.int32)[:, None]
    qp_rows = jnp.where(head_mask[None, None], q.reshape(nseq, dec_seq, 1, ATT_WIDTH),
                        jnp.zeros((), BF16)).reshape(nseq, rows, ATT_WIDTH)
    per_seq = lambda a: a.reshape(nseq, dec_seq, a.shape[-1])
    new_page_t = lambda a: jnp.pad(jnp.swapaxes(per_seq(a), 1, 2),
                                   ((0, 0), (0, 0), (0, PAGE_SIZE - dec_seq)))
    cache_idx_t = jnp.swapaxes(cache_idx_k, 1, 2)
    cache_k_t = jnp.transpose(cache_k, (0, 2, 3, 1)).reshape(n_pool, ATT_WIDTH, PAGE_SIZE)
    cache_v_t = jnp.transpose(cache_v, (0, 2, 3, 1)).reshape(n_pool, ATT_WIDTH, PAGE_SIZE)
    scores = _sample_scores(page_table, iq_rows, w_rows, cache_idx_t, dec_seq)
    att_s = _sample_attention(
        page_table, rel_bias, scores, iq_rows, w_rows, new_page_t(ikb[:, :IDX_DIM]), qp_rows,
        new_page_t(kb), new_page_t(vb), cache_k_t, cache_v_t, dec_seq,
        min(TOPK_MAX, (past + dec_seq) // 4))
    gin_s = jnp.pad(per_seq(gin), ((0, 0), (0, NEW_PAD - dec_seq), (0, 0)))
    gla_s, st_s = _gla(gin_s, _state_to_t(state_gla), *gla_w,
                       tile=NEW_PAD, chunk=NEW_PAD, n_valid=dec_seq)
    y_s = _post(att_s.reshape(ts, ATT_WIDTH), gla_s[:, :dec_seq].reshape(ts, GLA_WIDTH),
                x_sample.reshape(ts, d), p_sample.reshape(ts, -1), *post_w, min(256, ts))
    outs_s = (y_s.reshape(nseq, dec_seq, d),
              kf.reshape(nseq, dec_seq, ATT_HEADS, ATT_HEAD_DIM),
              vf.reshape(nseq, dec_seq, ATT_HEADS, ATT_HEAD_DIM),
              ikw[:, :IDX_DIM].reshape(nseq, dec_seq, IDX_DIM),
              _state_from_t(st_s))
    return outs_p, outs_s


def kernel(x_prompt, x_sample, p_prompt, p_sample, cache_k, cache_v, cache_idx_k, state_gla,
           page_table, rel_bias, g_pre_mix, w_in, w_gate_up, b_gate, g_gla_out, w_out,
           g_post_mix, g_pre_ffn, w_ff_gate, w_ff_up, w_ff_down, g_post_ffn, w_ple_gate,
           w_ple_proj):
    depth = w_in.shape[0]
    h_p, h_s = x_prompt, x_sample
    per_layer = []
    for i in range(depth):
        outs_p, outs_s = _layer(
            h_p, h_s, p_prompt[i], p_sample[i], cache_k[i], cache_v[i], cache_idx_k[i],
            state_gla[i], page_table, rel_bias, g_pre_mix[i], w_in[i], w_gate_up[i], b_gate[i],
            g_gla_out[i], w_out[i], g_post_mix[i], g_pre_ffn[i], w_ff_gate[i], w_ff_up[i],
            w_ff_down[i], g_post_ffn[i], w_ple_gate[i], w_ple_proj[i])
        h_p, h_s = outs_p[0], outs_s[0]
        per_layer.append(outs_p[1:] + outs_s[1:])
    stacked = [jnp.stack([lay[n] for lay in per_layer]) for n in range(8)]
    return (h_p, h_s, stacked[0], stacked[1], stacked[2], stacked[3],
            stacked[4], stacked[5], stacked[6], stacked[7])
```

```python
import functools
import math

import jax
import jax.numpy as jnp
from jax import lax
from jax.experimental import pallas as pl
from jax.experimental.pallas import tpu as pltpu

F32 = jnp.float32
BF16 = jnp.bfloat16

D_MODEL = 1024
ATT_HEADS = 8
ATT_HEAD_DIM = 64
ATT_WIDTH = ATT_HEADS * ATT_HEAD_DIM
IDX_HEADS = 8
IDX_DIM = 64
TOPK_MAX = 256
PAGE_SIZE = 128
GLA_HEADS = 4
GLA_DK = 64
GLA_DV = 128
GLA_KEY_WIDTH = GLA_HEADS * GLA_DK
GLA_WIDTH = GLA_HEADS * GLA_DV
GLA_GATE_RANK = 16
GLA_GATE_TAU = 16.0
REL_BUCKETS = 32
REL_MAX_DIST = 128
RMS_EPS = 1e-6
LOG2E = math.log2(math.e)

LANE = 128
SUBLANE = 8
BF16_ROWS = 16
MXU_WIDTH = 256
Q_TILE = MXU_WIDTH
K_CHUNK = MXU_WIDTH
HEAD_GROUP = MXU_WIDTH // ATT_HEAD_DIM
ACC_ROWS = ATT_HEAD_DIM + BF16_ROWS
VMEM_LIMIT = 56 * 1024 * 1024

SEG_Q = (0, 512)
SEG_K = (512, 1024)
SEG_V = (1024, 1536)
SEG_IQ = (1536, 2048)
SEG_IKW = (2048, 2176)
SEG_GLA = (2176, 3840)
PROJ_PACKED = 3840
GLA_IN_WIDTH = SEG_GLA[1] - SEG_GLA[0]

NT_DIMS = (((1,), (1,)), ((), ()))
TN_DIMS = (((0,), (0,)), ((), ()))


def _rms(x, g):
    return x * lax.rsqrt(jnp.mean(x * x, axis=-1, keepdims=True) + RMS_EPS) * g


def _inproj_body(x_ref, g_ref, w_ref, *rest, tokens_minor):
    a = _rms(x_ref[...], g_ref[...]).astype(BF16)

    def proj(seg):
        return jnp.dot(a, w_ref[:, seg[0]:seg[1]], preferred_element_type=F32)

    if tokens_minor:
        wt_ref, q_ref, kb_ref, iq_ref, ikw_ref, ikb_ref, gla_ref, kt_ref, vt_ref, vtb_ref, ikt_ref = rest
    else:
        q_ref, kb_ref, iq_ref, ikw_ref, ikb_ref, gla_ref, k_ref, v_ref, vb_ref = rest

    q_ref[...] = proj(SEG_Q).astype(BF16)
    kf = proj(SEG_K)
    kb_ref[...] = kf.astype(BF16)
    iq_ref[...] = proj(SEG_IQ).astype(BF16)
    lane = lax.broadcasted_iota(jnp.int32, (1, LANE), 1)
    is_w = (lane >= IDX_DIM) & (lane < IDX_DIM + IDX_HEADS)
    ikw = proj(SEG_IKW) * jnp.where(is_w, IDX_HEADS ** -0.5, 1.0)
    ikw_ref[...] = ikw
    ikb_ref[...] = ikw.astype(BF16)
    gla_ref[...] = proj(SEG_GLA)
    if tokens_minor:
        def proj_t(lo, hi):
            return lax.dot_general(wt_ref[lo:hi, :], a, NT_DIMS, preferred_element_type=F32)

        kt_ref[0] = proj_t(0, ATT_WIDTH)
        vt = proj_t(ATT_WIDTH, 2 * ATT_WIDTH)
        vt_ref[0] = vt
        vtb_ref[0] = vt.astype(BF16)
        ikt_ref[0] = proj_t(2 * ATT_WIDTH, 2 * ATT_WIDTH + IDX_DIM)
    else:
        k_ref[...] = kf
        vf = proj(SEG_V)
        v_ref[...] = vf
        vb_ref[...] = vf.astype(BF16)


def _inproj(x, g, w_packed, w_t, tm, seq):
    t = x.shape[0]
    tokens_minor = w_t is not None
    tiles_per_seq = seq // tm
    row = lambda w: pl.BlockSpec((tm, w), lambda i: (i, 0))
    const = lambda s: pl.BlockSpec(s, lambda i: (0, 0))
    col = lambda r: pl.BlockSpec((1, r, tm), lambda i: (i // tiles_per_seq, 0, i % tiles_per_seq))
    rows_shape = lambda w, dt: jax.ShapeDtypeStruct((t, w), dt)
    cols_shape = lambda r, dt: jax.ShapeDtypeStruct((t // seq, r, seq), dt)
    out_shape = [rows_shape(ATT_WIDTH, BF16), rows_shape(ATT_WIDTH, BF16),
                 rows_shape(ATT_WIDTH, BF16), rows_shape(LANE, F32), rows_shape(LANE, BF16),
                 rows_shape(GLA_IN_WIDTH, F32)]
    out_specs = [row(ATT_WIDTH), row(ATT_WIDTH), row(ATT_WIDTH), row(LANE), row(LANE),
                 row(GLA_IN_WIDTH)]
    in_specs = [row(D_MODEL), const((1, D_MODEL)), const((D_MODEL, PROJ_PACKED))]
    operands = [x, g, w_packed]
    if tokens_minor:
        in_specs.append(const(w_t.shape))
        operands.append(w_t)
        out_shape += [cols_shape(ATT_WIDTH, F32), cols_shape(ATT_WIDTH, F32),
                      cols_shape(ATT_WIDTH, BF16), cols_shape(IDX_DIM, F32)]
        out_specs += [col(ATT_WIDTH), col(ATT_WIDTH), col(ATT_WIDTH), col(IDX_DIM)]
    else:
        out_shape += [rows_shape(ATT_WIDTH, F32), rows_shape(ATT_WIDTH, F32),
                      rows_shape(ATT_WIDTH, BF16)]
        out_specs += [row(ATT_WIDTH), row(ATT_WIDTH), row(ATT_WIDTH)]
    return pl.pallas_call(
        functools.partial(_inproj_body, tokens_minor=tokens_minor),
        grid=(t // tm,),
        in_specs=in_specs,
        out_specs=tuple(out_specs),
        out_shape=tuple(out_shape),
        compiler_params=pltpu.CompilerParams(dimension_semantics=("arbitrary",),
                                             vmem_limit_bytes=VMEM_LIMIT),
        name="inproj",
    )(*operands)


def _log_sigmoid(z):
    return jnp.minimum(z, 0.0) - jnp.log1p(jnp.exp(-jnp.abs(z)))


def _gla_body(gin_ref, s0_ref, wgu_ref, bg_ref, gg_ref, o_ref, sout_ref, st_ref, *, chunk, n_valid):
    @pl.when(pl.program_id(1) == 0)
    def _():
        st_ref[...] = s0_ref[0]

    n_chunks = gin_ref.shape[1] // chunk
    ri = lax.broadcasted_iota(jnp.int32, (chunk, chunk), 0)
    ci = lax.broadcasted_iota(jnp.int32, (chunk, chunk), 1)
    causal = ri >= ci
    tri = causal.astype(F32)
    row_valid = lax.broadcasted_iota(jnp.int32, (chunk, 1), 0) < n_valid

    def one_chunk(c, carry):
        r0 = pl.multiple_of(c * chunk, chunk)
        blk = gin_ref[0, pl.ds(r0, chunk), :]
        gq = blk[:, 0:256]
        gk = blk[:, 256:512]
        gv = blk[:, 512:1024]
        glr = blk[:, 1024:1152]
        gout = blk[:, 1152:1664]
        z = jnp.dot(glr.astype(BF16), wgu_ref[...], preferred_element_type=F32) + bg_ref[...]
        la = _log_sigmoid(z) / GLA_GATE_TAU
        if n_valid < chunk:
            la = jnp.where(row_valid, la, 0.0)
        b = jnp.dot(tri, la, precision=lax.Precision.HIGHEST, preferred_element_type=F32)
        b_last = b[chunk - 1:chunk, :]
        qt = gq * jnp.exp(b)
        kt = gk * jnp.exp(-b)
        kd = gk * jnp.exp(b_last - b)
        dec = jnp.exp(b_last)
        st = st_ref[...]
        o_parts = []
        st_parts = []
        for h in range(GLA_HEADS):
            ks = slice(GLA_DK * h, GLA_DK * (h + 1))
            vs = slice(GLA_DV * h, GLA_DV * (h + 1))
            qh = qt[:, ks].astype(BF16)
            kh = kt[:, ks].astype(BF16)
            kdh = kd[:, ks].astype(BF16)
            vh = gv[:, vs].astype(BF16)
            sth = st[:, ks]
            attn = lax.dot_general(qh, kh, NT_DIMS, preferred_element_type=F32)
            attn = jnp.where(causal, attn, 0.0)
            o = jnp.dot(attn.astype(BF16), vh, preferred_element_type=F32)
            o = o + lax.dot_general(qh, sth.astype(BF16), NT_DIMS, preferred_element_type=F32)
            upd = lax.dot_general(vh, kdh, TN_DIMS, preferred_element_type=F32)
            st_parts.append(dec[:, ks] * sth + upd)
            o_parts.append(_rms(o, gg_ref[...]))
        st_ref[...] = jnp.concatenate(st_parts, axis=1)
        o_all = jnp.concatenate(o_parts, axis=1)
        gate = gout * jax.nn.sigmoid(gout)
        o_ref[0, pl.ds(r0, chunk), :] = (o_all * gate).astype(BF16)
        return carry

    lax.fori_loop(0, n_chunks, one_chunk, 0)
    sout_ref[0] = st_ref[...]


def _gla(gin, s0_t, wgu, bg, gg, *, tile, chunk, n_valid):
    nseq, t, _ = gin.shape
    body = functools.partial(_gla_body, chunk=chunk, n_valid=n_valid)
    return pl.pallas_call(
        body,
        grid=(nseq, t // tile),
        in_specs=[
            pl.BlockSpec((1, tile, GLA_IN_WIDTH), lambda b, j: (b, j, 0)),
            pl.BlockSpec((1, GLA_DV, GLA_KEY_WIDTH), lambda b, j: (b, 0, 0)),
            pl.BlockSpec((LANE, GLA_KEY_WIDTH), lambda b, j: (0, 0)),
            pl.BlockSpec((1, GLA_KEY_WIDTH), lambda b, j: (0, 0)),
            pl.BlockSpec((1, GLA_DV), lambda b, j: (0, 0)),
        ],
        out_specs=(
            pl.BlockSpec((1, tile, GLA_WIDTH), lambda b, j: (b, j, 0)),
            pl.BlockSpec((1, GLA_DV, GLA_KEY_WIDTH), lambda b, j: (b, 0, 0)),
        ),
        out_shape=(
            jax.ShapeDtypeStruct((nseq, t, GLA_WIDTH), BF16),
            jax.ShapeDtypeStruct((nseq, GLA_DV, GLA_KEY_WIDTH), F32),
        ),
        scratch_shapes=[pltpu.VMEM((GLA_DV, GLA_KEY_WIDTH), F32)],
        compiler_params=pltpu.CompilerParams(dimension_semantics=("arbitrary", "arbitrary"),
                                             vmem_limit_bytes=VMEM_LIMIT),
        name="gla",
    )(gin, s0_t, wgu, bg, gg)


def _rel_bucket(dist):
    n = jnp.maximum(dist, 0)
    exact = REL_BUCKETS // 2
    nf = jnp.maximum(n, 1).astype(F32)
    large = exact + (jnp.log(nf / exact) / math.log(REL_MAX_DIST / exact)
                     * (REL_BUCKETS - exact)).astype(jnp.int32)
    large = jnp.minimum(large, REL_BUCKETS - 1)
    return jnp.where(n < exact, n, large)


def _select_topk(sc_ref, n_queries, n_chunks, n_allowed, topk, keys_axis, max_iter=18):
    neg_inf = -jnp.inf
    n_acc = 4 if keys_axis == 0 else K_CHUNK // LANE
    if keys_axis == 1:
        part_shape = (n_queries, LANE)
        q_shape = (n_queries, 1)
        chunk_shape = (n_queries, K_CHUNK)
        load = lambda k0: sc_ref[:, pl.ds(k0, K_CHUNK)]
        pieces = lambda x: [x[:, j * LANE:(j + 1) * LANE] for j in range(K_CHUNK // LANE)]
        last = lambda r: r[:, K_CHUNK - 1:K_CHUNK]
    else:
        part_shape = (SUBLANE, n_queries)
        q_shape = (1, n_queries)
        chunk_shape = (K_CHUNK, n_queries)
        load = lambda k0: sc_ref[pl.ds(k0, K_CHUNK), :]
        pieces = lambda x: [x[j * SUBLANE:(j + 1) * SUBLANE] for j in range(K_CHUNK // SUBLANE)]
        last = lambda r: r[K_CHUNK - 1:K_CHUNK, :]

    def store(k0, val):
        if keys_axis == 1:
            sc_ref[:, pl.ds(k0, K_CHUNK)] = val
        else:
            sc_ref[pl.ds(k0, K_CHUNK), :] = val

    def sweep(fns, inits, reds, combines):
        n_stat = len(fns)

        def body(c, accs):
            x = load(pl.multiple_of(c * K_CHUNK, K_CHUNK))
            accs = [list(a) for a in accs]
            for n, piece in enumerate(pieces(x)):
                for s in range(n_stat):
                    accs[s][n % n_acc] = fns[s](accs[s][n % n_acc], piece)
            return tuple(tuple(a) for a in accs)

        init = tuple(tuple(jnp.full(part_shape, inits[s], F32) for _ in range(n_acc))
                     for s in range(n_stat))
        accs = lax.fori_loop(0, n_chunks, body, init)
        out = []
        for s in range(n_stat):
            acc = accs[s][0]
            for other in accs[s][1:]:
                acc = combines[s](acc, other)
            out.append(reds[s](acc, axis=keys_axis, keepdims=True))
        return out

    def count_ge(t):
        tb = jnp.broadcast_to(t, part_shape)
        return sweep([lambda a, x: jnp.where(x >= tb, a + 1.0, a)], [0.0], [jnp.sum],
                     [jnp.add])[0]

    def count_gt(t):
        tb = jnp.broadcast_to(t, part_shape)
        return sweep([lambda a, x: a + jnp.where(x > tb, 1.0, 0.0)], [0.0], [jnp.sum],
                     [jnp.add])[0]

    def max_below(t):
        tb = jnp.broadcast_to(t, part_shape)
        return sweep([lambda a, x: jnp.maximum(a, jnp.where(x < tb, x, neg_inf))], [neg_inf],
                     [jnp.max], [jnp.maximum])[0]

    row_max, row_min = sweep(
        [jnp.maximum, lambda a, x: jnp.minimum(a, jnp.where(x == neg_inf, jnp.inf, x))],
        [neg_inf, jnp.inf], [jnp.max, jnp.min], [jnp.maximum, jnp.minimum])

    k = float(topk)
    take_all = n_allowed <= k
    lo0 = row_min
    hi0 = row_max + jnp.maximum(jnp.abs(row_max) * 1e-6, 1e-30)
    t0 = jnp.full(q_shape, jnp.finfo(F32).min, F32)
    done0 = jnp.where(take_all, 1.0, 0.0)

    def bis_body(_, st):
        lo, hi, t, done = st
        mid = lo + (hi - lo) * 0.5
        c = count_ge(mid)
        hit = (c == k) & (done < 0.5)
        t = jnp.where(hit, mid, t)
        done = jnp.where(hit, 1.0, done)
        lo = jnp.where(c > k, mid, lo)
        hi = jnp.where(c < k, mid, hi)
        return lo, hi, t, done

    lo, hi, t, done = lax.fori_loop(0, max_iter, bis_body, (lo0, hi0, t0, done0))

    all_done = jnp.min(done) > 0.5

    def write(fn):
        def body(c, carry):
            k0 = pl.multiple_of(c * K_CHUNK, K_CHUNK)
            sel, carry = fn(load(k0), carry)
            store(k0, jnp.where(sel, 0.0, neg_inf))
            return carry
        return body

    @pl.when(all_done)
    def _():
        tb = jnp.broadcast_to(t, chunk_shape)
        lax.fori_loop(0, n_chunks, write(lambda x, cr: (x >= tb, cr)), 0)

    @pl.when(jnp.logical_not(all_done))
    def _():
        def snap_cond(st):
            return jnp.min(st[2]) < 0.5

        def snap_body(st):
            hi_, t_, done_ = st
            v = max_below(hi_)
            c = count_ge(v)
            ok = (c >= k) & (done_ < 0.5)
            t_ = jnp.where(ok, v, t_)
            hi_ = jnp.where((done_ < 0.5) & jnp.logical_not(ok), v, hi_)
            done_ = jnp.where(ok, 1.0, done_)
            return hi_, t_, done_

        _, tv, _ = lax.while_loop(snap_cond, snap_body, (hi, t, done))
        quota = k - count_gt(tv)
        tb = jnp.broadcast_to(tv, chunk_shape)
        qb = jnp.broadcast_to(quota, chunk_shape)
        ui = lax.broadcasted_iota(jnp.int32, (K_CHUNK, K_CHUNK), 0)
        uj = lax.broadcasted_iota(jnp.int32, (K_CHUNK, K_CHUNK), 1)
        prefix = ((ui <= uj) if keys_axis == 1 else (ui >= uj)).astype(BF16)

        def tie_fn(x, seen):
            tie = x == tb
            if keys_axis == 1:
                rank = jnp.dot(tie.astype(BF16), prefix, preferred_element_type=F32) + seen
            else:
                rank = jnp.dot(prefix, tie.astype(BF16), preferred_element_type=F32) + seen
            sel = (x > tb) | (tie & (rank <= qb))
            return sel, last(rank)

        lax.fori_loop(0, n_chunks, write(tie_fn), jnp.zeros(q_shape, F32))


def _head_lane_mask(h, width):
    lane = lax.broadcasted_iota(jnp.int32, (1, width), 1)
    lo = (h % HEAD_GROUP) * ATT_HEAD_DIM
    return (lane >= lo) & (lane < lo + ATT_HEAD_DIM)


def _pattn_body(rb_ref, q_ref, iq_ref, w_ref, k_ref, vt_ref, ik_ref, o_ref,
                sc_ref, qp_ref, iqp_ref, bias_ref, m_ref, alpha_ref, acc_ref, x_ref, p_ref,
                *, topk):
    i = pl.program_id(1)
    n_chunks = i + 1

    @pl.when((pl.program_id(0) == 0) & (i == 0))
    def _():
        kk = lax.broadcasted_iota(jnp.int32, (K_CHUNK, Q_TILE), 0)
        qq = lax.broadcasted_iota(jnp.int32, (K_CHUNK, Q_TILE), 1)
        for delta in range(2):
            bucket = _rel_bucket(qq - kk + Q_TILE * delta)
            for h in range(ATT_HEADS):
                acc = jnp.zeros((K_CHUNK, Q_TILE), F32)
                for bk in range(REL_BUCKETS - 1):
                    acc = jnp.where(bucket == bk, rb_ref[bk, h] - rb_ref[REL_BUCKETS - 1, h], acc)
                bias_ref[delta, h] = acc * LOG2E

    q = q_ref[0]
    iq = iq_ref[0]
    for h in range(ATT_HEADS):
        g = h // HEAD_GROUP
        qp_ref[h] = jnp.where(_head_lane_mask(h, MXU_WIDTH),
                              q[:, MXU_WIDTH * g:MXU_WIDTH * (g + 1)], 0)
        iqp_ref[h] = jnp.concatenate(
            [iq[:, IDX_DIM * h:IDX_DIM * (h + 1)], jnp.zeros((Q_TILE, LANE - IDX_DIM), BF16)], axis=1)
    w_t = jnp.transpose(w_ref[0])[IDX_DIM:IDX_DIM + IDX_HEADS, :]

    q_pos = i * Q_TILE + lax.broadcasted_iota(jnp.int32, (1, Q_TILE), 1)

    def score_chunk(c, carry, *, diagonal):
        k0 = pl.multiple_of(c * K_CHUNK, K_CHUNK)
        ikc = ik_ref[0, pl.ds(k0, K_CHUNK), :]
        acc = jnp.zeros((K_CHUNK, Q_TILE), F32)
        for h in range(IDX_HEADS):
            s = lax.dot_general(ikc, iqp_ref[h], NT_DIMS, preferred_element_type=F32)
            acc = acc + jnp.maximum(s, 0.0) * w_t[h:h + 1, :]
        if diagonal:
            k_pos = k0 + lax.broadcasted_iota(jnp.int32, (K_CHUNK, 1), 0)
            acc = jnp.where(k_pos <= q_pos, acc, -jnp.inf)
        sc_ref[pl.ds(k0, K_CHUNK), :] = acc
        return carry

    lax.fori_loop(0, i, functools.partial(score_chunk, diagonal=False), 0)
    score_chunk(i, 0, diagonal=True)

    _select_topk(sc_ref, Q_TILE, n_chunks, (q_pos + 1).astype(F32), topk, keys_axis=0)

    m_ref[...] = jnp.full(m_ref.shape, -1e30, F32)
    acc_ref[...] = jnp.zeros(acc_ref.shape, F32)
    ones_rows = jnp.ones((BF16_ROWS, K_CHUNK), BF16)
    row_block = 4 * SUBLANE

    def logits_to_probs(c, near):
        k0 = pl.multiple_of(c * K_CHUNK, K_CHUNK)
        slot = c % 2
        mb = sc_ref[pl.ds(k0, K_CHUNK), :]
        for h in range(ATT_HEADS):
            g = h // HEAD_GROUP
            kc = k_ref[0, pl.ds(k0, K_CHUNK), MXU_WIDTH * g:MXU_WIDTH * (g + 1)]
            s = lax.dot_general(kc, qp_ref[h], NT_DIMS, preferred_element_type=F32)
            mx = jnp.full((SUBLANE, Q_TILE), -jnp.inf, F32)
            for r in range(K_CHUNK // row_block):
                rs = slice(r * row_block, (r + 1) * row_block)
                xb = s[rs] + mb[rs]
                if near:
                    xb = xb + bias_ref[i - c, h, rs, :]
                x_ref[h, rs, :] = xb
                for u in range(row_block // SUBLANE):
                    mx = jnp.maximum(mx, xb[u * SUBLANE:(u + 1) * SUBLANE])
            m_old = m_ref[h]
            m_new = jnp.maximum(m_old, jnp.max(mx, axis=0, keepdims=True))
            alpha_ref[slot, h] = jnp.exp2(m_old - m_new)
            m_ref[h] = m_new
        for h in range(ATT_HEADS):
            p_ref[slot, h] = jnp.exp2(x_ref[h] - m_ref[h]).astype(BF16)

    def accumulate(c):
        k0 = pl.multiple_of(jnp.maximum(c, 0) * K_CHUNK, K_CHUNK)
        slot = (c + 2) % 2
        for h in range(ATT_HEADS):
            vt = vt_ref[0, ATT_HEAD_DIM * h:ATT_HEAD_DIM * (h + 1), pl.ds(k0, K_CHUNK)]
            lhs = jnp.concatenate([vt, ones_rows], axis=0)
            acc_ref[h] = alpha_ref[slot, h] * acc_ref[h] + jnp.dot(
                lhs, p_ref[slot, h], preferred_element_type=F32)

    def step(c, carry, *, near):
        accumulate(c - 1)
        logits_to_probs(c, near)
        return carry

    p_ref[1] = jnp.zeros(p_ref.shape[1:], BF16)
    alpha_ref[1] = jnp.ones(alpha_ref.shape[1:], F32)
    n_far = jnp.maximum(i - 1, 0)
    lax.fori_loop(0, n_far, functools.partial(step, near=False), 0)
    lax.fori_loop(n_far, n_chunks, functools.partial(step, near=True), 0)
    accumulate(i)

    out_t = jnp.concatenate(
        [acc_ref[h, 0:ATT_HEAD_DIM, :] / acc_ref[h, ATT_HEAD_DIM:ATT_HEAD_DIM + 1, :]
         for h in range(ATT_HEADS)], axis=0)
    o_ref[0] = jnp.transpose(out_t).astype(BF16)


def _prompt_attention(rel_bias, q, iq, ikw, kb, vt, ikb, topk):
    bsz, s, _ = q.shape
    tile = lambda w: pl.BlockSpec((1, Q_TILE, w), lambda b, i: (b, i, 0))
    per_batch = lambda shape: pl.BlockSpec((1,) + shape, lambda b, i: (b, 0, 0),
                                           pipeline_mode=pl.Buffered(1))
    body = functools.partial(_pattn_body, topk=topk)
    return pl.pallas_call(
        body,
        grid=(bsz, s // Q_TILE),
        in_specs=[pl.BlockSpec(memory_space=pltpu.SMEM),
                  tile(ATT_WIDTH), tile(ATT_WIDTH), tile(LANE),
                  per_batch((s, ATT_WIDTH)), per_batch((ATT_WIDTH, s)), per_batch((s, LANE))],
        out_specs=tile(ATT_WIDTH),
        out_shape=jax.ShapeDtypeStruct((bsz, s, ATT_WIDTH), BF16),
        scratch_shapes=[
            pltpu.VMEM((s, Q_TILE), F32),
            pltpu.VMEM((ATT_HEADS, Q_TILE, MXU_WIDTH), BF16),
            pltpu.VMEM((IDX_HEADS, Q_TILE, LANE), BF16),
            pltpu.VMEM((2, ATT_HEADS, K_CHUNK, Q_TILE), F32),
            pltpu.VMEM((ATT_HEADS, 1, Q_TILE), F32),
            pltpu.VMEM((2, ATT_HEADS, 1, Q_TILE), F32),
            pltpu.VMEM((ATT_HEADS, ACC_ROWS, Q_TILE), F32),
            pltpu.VMEM((ATT_HEADS, K_CHUNK, Q_TILE), F32),
            pltpu.VMEM((2, ATT_HEADS, K_CHUNK, Q_TILE), BF16),
        ],
        compiler_params=pltpu.CompilerParams(dimension_semantics=("arbitrary", "arbitrary"),
                                             vmem_limit_bytes=VMEM_LIMIT),
        name="prompt_attention",
    )(rel_bias, q, iq, ikw, kb, vt, ikb)


SCORE_PAGES = 16
ATTN_PAGES = 16
NEW_PAD = 16


def _head_sums(y, dec_seq):
    return jnp.concatenate(
        [jnp.sum(y[IDX_HEADS * t:IDX_HEADS * (t + 1)], axis=0, keepdims=True)
         for t in range(dec_seq)], axis=0)


def _sscore_body(pt_ref, iq_ref, w_ref, *rest, dec_seq):
    del pt_ref
    pages = rest[:SCORE_PAGES]
    out_ref = rest[SCORE_PAGES]
    iq = iq_ref[0]
    w = w_ref[0]
    for m in range(SCORE_PAGES):
        s = jnp.dot(iq, pages[m][0].astype(BF16), preferred_element_type=F32)
        out_ref[0, :, m * PAGE_SIZE:(m + 1) * PAGE_SIZE] = _head_sums(jnp.maximum(s, 0.0) * w,
                                                                      dec_seq)


def _sample_scores(page_table, iq_rows, w_rows, cache_idx_t, dec_seq):
    nseq, n_pages = page_table.shape
    steps = n_pages // SCORE_PAGES
    rows = dec_seq * IDX_HEADS

    def page_spec(m):
        return pl.BlockSpec((1, IDX_DIM, PAGE_SIZE),
                            lambda b, j, pt: (pt[b, j * SCORE_PAGES + m], 0, 0))

    grid_spec = pltpu.PrefetchScalarGridSpec(
        num_scalar_prefetch=1,
        grid=(nseq, steps),
        in_specs=[pl.BlockSpec((1, rows, IDX_DIM), lambda b, j, pt: (b, 0, 0)),
                  pl.BlockSpec((1, rows, 1), lambda b, j, pt: (b, 0, 0))]
                 + [page_spec(m) for m in range(SCORE_PAGES)],
        out_specs=pl.BlockSpec((1, dec_seq, SCORE_PAGES * PAGE_SIZE), lambda b, j, pt: (b, 0, j)),
    )
    return pl.pallas_call(
        functools.partial(_sscore_body, dec_seq=dec_seq),
        grid_spec=grid_spec,
        out_shape=jax.ShapeDtypeStruct((nseq, dec_seq, n_pages * PAGE_SIZE), F32),
        compiler_params=pltpu.CompilerParams(dimension_semantics=("arbitrary", "arbitrary"),
                                             vmem_limit_bytes=VMEM_LIMIT),
        name="sample_scores",
    )(page_table, iq_rows, w_rows, *([cache_idx_t] * SCORE_PAGES))


def _sattn_body(pt_ref, rb_ref, sc_in_ref, iq_ref, w_ref, iknew_ref, qp_ref, knew_ref, vnew_ref,
                *rest, dec_seq, past, topk):
    del pt_ref
    kpages = rest[:ATTN_PAGES]
    vpages = rest[ATTN_PAGES:2 * ATTN_PAGES]
    o_ref = rest[2 * ATTN_PAGES]
    sc_ref, mb_ref, m_ref, l_ref, acc_ref = rest[2 * ATTN_PAGES + 1:]
    j = pl.program_id(1)
    n_steps = pl.num_programs(1)
    rows = dec_seq * ATT_HEADS
    n_chunks = past // K_CHUNK + 1
    row_t = lax.broadcasted_iota(jnp.int32, (rows, 1), 0) // ATT_HEADS
    row_h = lax.broadcasted_iota(jnp.int32, (rows, 1), 0) % ATT_HEADS

    @pl.when(j == 0)
    def _():
        s = jnp.dot(iq_ref[0], iknew_ref[0], preferred_element_type=F32)
        new_sc = _head_sums(jnp.maximum(s, 0.0) * w_ref[0], dec_seq)
        t_idx = lax.broadcasted_iota(jnp.int32, (dec_seq, K_CHUNK), 0)
        k_idx = lax.broadcasted_iota(jnp.int32, (dec_seq, K_CHUNK), 1)
        padded = jnp.concatenate(
            [new_sc, jnp.zeros((dec_seq, K_CHUNK - PAGE_SIZE), F32)], axis=1)
        sc_ref[:, 0:past] = sc_in_ref[0]
        sc_ref[:, past:past + K_CHUNK] = jnp.where(k_idx <= t_idx, padded, -jnp.inf)
        n_allowed = (past + 1 + lax.broadcasted_iota(jnp.int32, (dec_seq, 1), 0)).astype(F32)
        _select_topk(sc_ref, dec_seq, n_chunks, n_allowed, topk, keys_axis=1)
        for t in range(dec_seq):
            mb_ref[ATT_HEADS * t:ATT_HEADS * (t + 1), :] = jnp.broadcast_to(
                sc_ref[t:t + 1, :], (ATT_HEADS, past + K_CHUNK))
        m_ref[...] = jnp.full(m_ref.shape, -1e30, F32)
        l_ref[...] = jnp.zeros(l_ref.shape, F32)
        acc_ref[...] = jnp.zeros(acc_ref.shape, F32)

    qp = qp_ref[0]

    def near_bias(dist):
        bucket = _rel_bucket(dist)
        acc = jnp.zeros(dist.shape, F32)
        for bk in range(REL_BUCKETS - 1):
            per_row = jnp.zeros((rows, 1), F32)
            for h in range(ATT_HEADS):
                per_row = jnp.where(row_h == h, rb_ref[bk, h] - rb_ref[REL_BUCKETS - 1, h], per_row)
            acc = jnp.where(bucket == bk, per_row, acc)
        return acc * LOG2E

    def attend_pages(k_ts, v_ts, mb, last_bias):
        logit = jnp.concatenate(
            [jnp.dot(qp, k_t, preferred_element_type=F32) for k_t in k_ts], axis=1) + mb
        n = len(k_ts)
        if n > 1:
            last_bias = jnp.concatenate(
                [jnp.zeros((rows, (n - 1) * PAGE_SIZE), F32), last_bias], axis=1)
        logit = logit + last_bias
        m_old = m_ref[...]
        m_new = jnp.maximum(m_old, jnp.max(logit, axis=1, keepdims=True))
        alpha = jnp.exp2(m_old - m_new)
        p = jnp.exp2(logit - m_new).astype(BF16)
        l_ref[...] = alpha * l_ref[...] + jnp.sum(p.astype(F32), axis=1, keepdims=True)
        acc = alpha * acc_ref[...]
        for m, v_t in enumerate(v_ts):
            acc = acc + lax.dot_general(p[:, m * PAGE_SIZE:(m + 1) * PAGE_SIZE], v_t, NT_DIMS,
                                        preferred_element_type=F32)
        acc_ref[...] = acc
        m_ref[...] = m_new

    is_last = j == n_steps - 1
    off = lax.broadcasted_iota(jnp.int32, (rows, PAGE_SIZE), 1)
    k0 = pl.multiple_of(j * (ATTN_PAGES * PAGE_SIZE), ATTN_PAGES * PAGE_SIZE)
    last_bias = lax.cond(is_last, lambda: near_bias(PAGE_SIZE + row_t - off),
                         lambda: jnp.zeros((rows, PAGE_SIZE), F32))
    attend_pages([kp[0].astype(BF16) for kp in kpages], [vp[0].astype(BF16) for vp in vpages],
                 mb_ref[:, pl.ds(k0, ATTN_PAGES * PAGE_SIZE)], last_bias)

    @pl.when(is_last)
    def _():
        attend_pages([knew_ref[0]], [vnew_ref[0]], mb_ref[:, past:past + PAGE_SIZE],
                     near_bias(row_t - off))
        res = acc_ref[...] / l_ref[...]
        lane_h = lax.broadcasted_iota(jnp.int32, (rows, ATT_WIDTH), 1) // ATT_HEAD_DIM
        res = jnp.where(lane_h == row_h, res, 0.0)
        o_ref[0] = _head_sums(res, dec_seq).astype(BF16)


def _sample_attention(page_table, rel_bias, scores, iq_rows, w_rows, iknew_t, qp_rows, knew_t,
                      vnew_t, cache_k_t, cache_v_t, dec_seq, topk):
    nseq, n_pages = page_table.shape
    past = n_pages * PAGE_SIZE
    steps = n_pages // ATTN_PAGES
    rows = dec_seq * ATT_HEADS

    def page_spec(m):
        return pl.BlockSpec((1, ATT_WIDTH, PAGE_SIZE),
                            lambda b, j, pt: (pt[b, j * ATTN_PAGES + m], 0, 0))

    per_seq = lambda shape: pl.BlockSpec((1,) + shape, lambda b, j, pt: (b, 0, 0))
    grid_spec = pltpu.PrefetchScalarGridSpec(
        num_scalar_prefetch=1,
        grid=(nseq, steps),
        in_specs=[pl.BlockSpec(memory_space=pltpu.SMEM),
                  per_seq((dec_seq, past)), per_seq((rows, IDX_DIM)), per_seq((rows, 1)),
                  per_seq((IDX_DIM, PAGE_SIZE)), per_seq((rows, ATT_WIDTH)),
                  per_seq((ATT_WIDTH, PAGE_SIZE)), per_seq((ATT_WIDTH, PAGE_SIZE))]
                 + [page_spec(m) for m in range(ATTN_PAGES)] * 2,
        out_specs=per_seq((dec_seq, ATT_WIDTH)),
        scratch_shapes=[
            pltpu.VMEM((dec_seq, past + K_CHUNK), F32),
            pltpu.VMEM((rows, past + K_CHUNK), F32),
            pltpu.VMEM((rows, 1), F32),
            pltpu.VMEM((rows, 1), F32),
            pltpu.VMEM((rows, ATT_WIDTH), F32),
        ],
    )
    return pl.pallas_call(
        functools.partial(_sattn_body, dec_seq=dec_seq, past=past, topk=topk),
        grid_spec=grid_spec,
        out_shape=jax.ShapeDtypeStruct((nseq, dec_seq, ATT_WIDTH), BF16),
        compiler_params=pltpu.CompilerParams(dimension_semantics=("arbitrary", "arbitrary"),
                                             vmem_limit_bytes=VMEM_LIMIT),
        name="sample_attention",
    )(page_table, rel_bias, scores, iq_rows, w_rows, iknew_t, qp_rows, knew_t, vnew_t,
      *([cache_k_t] * ATTN_PAGES), *([cache_v_t] * ATTN_PAGES))


FF_CHUNK = 256


def _post_body(att_ref, gla_ref, x_ref, p_ref, wo_ref, gpm_ref, gpf_ref, wg_ref, wu_ref, wd_ref,
               gpo_ref, wpg_ref, wpp_ref, o_ref, *, d_ff):
    wo = wo_ref
    mixed = jnp.dot(att_ref[...], wo[0:ATT_WIDTH, :], preferred_element_type=F32)
    mixed = mixed + jnp.dot(gla_ref[...], wo[ATT_WIDTH:ATT_WIDTH + GLA_WIDTH, :],
                            preferred_element_type=F32)
    h = x_ref[...] + _rms(mixed, gpm_ref[...])
    f = _rms(h, gpf_ref[...]).astype(BF16)
    ff = jnp.zeros(h.shape, F32)
    for c in range(d_ff // FF_CHUNK):
        cs = slice(c * FF_CHUNK, (c + 1) * FF_CHUNK)
        gate = jnp.dot(f, wg_ref[:, cs], preferred_element_type=F32)
        up = jnp.dot(f, wu_ref[:, cs], preferred_element_type=F32)
        act = (gate * jax.nn.sigmoid(gate) * up).astype(BF16)
        ff = ff + jnp.dot(act, wd_ref[cs, :], preferred_element_type=F32)
    h = h + _rms(ff, gpo_ref[...])
    gate = jax.nn.sigmoid(jnp.dot(h.astype(BF16), wpg_ref[...], preferred_element_type=F32))
    ple = jnp.dot(p_ref[...].astype(BF16), wpp_ref[...], preferred_element_type=F32)
    o_ref[...] = h + gate * ple


def _post(att, gla, x, p, wo, gpm, gpf, wg, wu, wd, gpo, wpg, wpp, tm):
    t = x.shape[0]
    d_ff = wg.shape[1]
    row = lambda w: pl.BlockSpec((tm, w), lambda i: (i, 0))
    const = lambda a: pl.BlockSpec(a.shape, lambda i: (0, 0), pipeline_mode=pl.Buffered(1))
    return pl.pallas_call(
        functools.partial(_post_body, d_ff=d_ff),
        grid=(t // tm,),
        in_specs=[row(ATT_WIDTH), row(GLA_WIDTH), row(D_MODEL), row(p.shape[1]),
                  const(wo), const(gpm), const(gpf), const(wg), const(wu), const(wd),
                  const(gpo), const(wpg), const(wpp)],
        out_specs=row(D_MODEL),
        out_shape=jax.ShapeDtypeStruct((t, D_MODEL), F32),
        compiler_params=pltpu.CompilerParams(dimension_semantics=("arbitrary",),
                                             vmem_limit_bytes=VMEM_LIMIT),
        name="post",
    )(att, gla, x, p, wo, gpm, gpf, wg, wu, wd, gpo, wpg, wpp)


def _pack_w_in(w_in):
    offs = [0]
    for wdt in (ATT_WIDTH, ATT_WIDTH, ATT_WIDTH, IDX_HEADS * IDX_DIM, IDX_DIM, IDX_HEADS,
                GLA_KEY_WIDTH, GLA_KEY_WIDTH, GLA_WIDTH, GLA_GATE_RANK, GLA_WIDTH):
        offs.append(offs[-1] + wdt)
    aq, ak, av, iq, ik, iw, gq, gk, gv, glr, gout = [w_in[:, offs[n]:offs[n + 1]] for n in range(11)]
    d = w_in.shape[0]
    zeros = lambda n: jnp.zeros((d, n), w_in.dtype)
    packed = jnp.concatenate([
        aq * (ATT_HEAD_DIM ** -0.5 * LOG2E), ak, av, iq * (IDX_DIM ** -0.5),
        ik, iw, zeros(LANE - IDX_DIM - IDX_HEADS),
        gq * (GLA_DK ** -0.5), gk, gv, glr, zeros(LANE - GLA_GATE_RANK), gout], axis=1)
    return packed.astype(BF16)


def _state_to_t(s):
    n = s.shape[0]
    return jnp.transpose(s, (0, 3, 1, 2)).reshape(n, GLA_DV, GLA_KEY_WIDTH)


def _state_from_t(st):
    n = st.shape[0]
    return jnp.transpose(st.reshape(n, GLA_DV, GLA_HEADS, GLA_DK), (0, 2, 3, 1))


def _layer(x_prompt, x_sample, p_prompt, p_sample, cache_k, cache_v, cache_idx_k, state_gla,
           page_table, rel_bias, g_pre_mix, w_in, w_gate_up, b_gate, g_gla_out, w_out,
           g_post_mix, g_pre_ffn, w_ff_gate, w_ff_up, w_ff_down, g_post_ffn, w_ple_gate,
           w_ple_proj):
    bsz, seq, d = x_prompt.shape
    nseq, dec_seq, _ = x_sample.shape
    n_pool = cache_k.shape[0]
    past = page_table.shape[1] * PAGE_SIZE
    row2 = lambda g: g.reshape(1, -1)

    w_packed = _pack_w_in(w_in)
    w_t = jnp.transpose(w_packed[:, SEG_K[0]:SEG_IKW[0] + IDX_DIM])
    w_t = jnp.concatenate([w_t[0:2 * ATT_WIDTH], w_t[SEG_IKW[0] - SEG_K[0]:]], axis=0)
    wgu = jnp.concatenate(
        [w_gate_up, jnp.zeros((LANE - GLA_GATE_RANK, GLA_KEY_WIDTH), w_gate_up.dtype)],
        axis=0).astype(BF16)
    post_w = (w_out.astype(BF16), row2(g_post_mix), row2(g_pre_ffn), w_ff_gate.astype(BF16),
              w_ff_up.astype(BF16), w_ff_down.astype(BF16), row2(g_post_ffn),
              w_ple_gate.astype(BF16), w_ple_proj.astype(BF16))
    gla_w = (wgu, row2(b_gate), row2(g_gla_out))

    tp = bsz * seq
    q, kb, iq, ikw, ikb, gin, k_t, v_t, vtb, ik_t = _inproj(
        x_prompt.reshape(tp, d), row2(g_pre_mix), w_packed, w_t, min(512, seq), seq)
    r3 = lambda a: a.reshape(bsz, seq, a.shape[-1])
    att = _prompt_attention(rel_bias, r3(q), r3(iq), r3(ikw), r3(kb), vtb, r3(ikb),
                            min(TOPK_MAX, seq // 4))
    gla_chunk = min(64, seq)
    gla_o, st_p = _gla(r3(gin), jnp.zeros((bsz, GLA_DV, GLA_KEY_WIDTH), F32), *gla_w,
                       tile=min(512, seq), chunk=gla_chunk, n_valid=gla_chunk)
    y_p = _post(att.reshape(tp, ATT_WIDTH), gla_o.reshape(tp, GLA_WIDTH), x_prompt.reshape(tp, d),
                p_prompt.reshape(tp, -1), *post_w, min(256, tp))
    heads_last = lambda a: jnp.transpose(
        a.reshape(bsz, ATT_HEADS, ATT_HEAD_DIM, seq), (0, 3, 1, 2))
    outs_p = (y_p.reshape(bsz, seq, d), heads_last(k_t), heads_last(v_t),
              jnp.swapaxes(ik_t, 1, 2), _state_from_t(st_p))

    ts = nseq * dec_seq
    q, kb, iq, ikw, ikb, gin, kf, vf, vb = _inproj(
        x_sample.reshape(ts, d), row2(g_pre_mix), w_packed, None, min(512, ts), ts)
    rows = dec_seq * ATT_HEADS
    iq_rows = iq.reshape(nseq, rows, IDX_DIM)
    w_rows = ikw[:, IDX_DIM:IDX_DIM + IDX_HEADS].reshape(nseq, rows, 1)
    head_of_col = jnp.arange(ATT_WIDTH, dtype=jnp.int32) // ATT_HEAD_DIM
    head_mask = head_of_col[None, :] == jnp.arange(ATT_HEADS, dtype=jnp.int32)[:, None]
    qp_rows = jnp.where(head_mask[None, None], q.reshape(nseq, dec_seq, 1, ATT_WIDTH),
                        jnp.zeros((), BF16)).reshape(nseq, rows, ATT_WIDTH)
    per_seq = lambda a: a.reshape(nseq, dec_seq, a.shape[-1])
    new_page_t = lambda a: jnp.pad(jnp.swapaxes(per_seq(a), 1, 2),
                                   ((0, 0), (0, 0), (0, PAGE_SIZE - dec_seq)))
    cache_idx_t = jnp.swapaxes(cache_idx_k, 1, 2)
    cache_k_t = jnp.transpose(cache_k, (0, 2, 3, 1)).reshape(n_pool, ATT_WIDTH, PAGE_SIZE)
    cache_v_t = jnp.transpose(cache_v, (0, 2, 3, 1)).reshape(n_pool, ATT_WIDTH, PAGE_SIZE)
    scores = _sample_scores(page_table, iq_rows, w_rows, cache_idx_t, dec_seq)
    att_s = _sample_attention(
        page_table, rel_bias, scores, iq_rows, w_rows, new_page_t(ikb[:, :IDX_DIM]), qp_rows,
        new_page_t(kb), new_page_t(vb), cache_k_t, cache_v_t, dec_seq,
        min(TOPK_MAX, (past + dec_seq) // 4))
    gin_s = jnp.pad(per_seq(gin), ((0, 0), (0, NEW_PAD - dec_seq), (0, 0)))
    gla_s, st_s = _gla(gin_s, _state_to_t(state_gla), *gla_w,
                       tile=NEW_PAD, chunk=NEW_PAD, n_valid=dec_seq)
    y_s = _post(att_s.reshape(ts, ATT_WIDTH), gla_s[:, :dec_seq].reshape(ts, GLA_WIDTH),
                x_sample.reshape(ts, d), p_sample.reshape(ts, -1), *post_w, min(256, ts))
    outs_s = (y_s.reshape(nseq, dec_seq, d),
              kf.reshape(nseq, dec_seq, ATT_HEADS, ATT_HEAD_DIM),
              vf.reshape(nseq, dec_seq, ATT_HEADS, ATT_HEAD_DIM),
              ikw[:, :IDX_DIM].reshape(nseq, dec_seq, IDX_DIM),
              _state_from_t(st_s))
    return outs_p, outs_s


def kernel(x_prompt, x_sample, p_prompt, p_sample, cache_k, cache_v, cache_idx_k, state_gla,
           page_table, rel_bias, g_pre_mix, w_in, w_gate_up, b_gate, g_gla_out, w_out,
           g_post_mix, g_pre_ffn, w_ff_gate, w_ff_up, w_ff_down, g_post_ffn, w_ple_gate,
           w_ple_proj):
    depth = w_in.shape[0]
    h_p, h_s = x_prompt, x_sample
    per_layer = []
    for i in range(depth):
        outs_p, outs_s = _layer(
            h_p, h_s, p_prompt[i], p_sample[i], cache_k[i], cache_v[i], cache_idx_k[i],
            state_gla[i], page_table, rel_bias, g_pre_mix[i], w_in[i], w_gate_up[i], b_gate[i],
            g_gla_out[i], w_out[i], g_post_mix[i], g_pre_ffn[i], w_ff_gate[i], w_ff_up[i],
            w_ff_down[i], g_post_ffn[i], w_ple_gate[i], w_ple_proj[i])
        h_p, h_s = outs_p[0], outs_s[0]
        per_layer.append(outs_p[1:] + outs_s[1:])
    stacked = [jnp.stack([lay[n] for lay in per_layer]) for n in range(8)]
    return (h_p, h_s, stacked[0], stacked[1], stacked[2], stacked[3],
            stacked[4], stacked[5], stacked[6], stacked[7])
```

```python
import functools
import math

import jax
import jax.numpy as jnp
from jax import lax
from jax.experimental import pallas as pl
from jax.experimental.pallas import tpu as pltpu

F32 = jnp.float32
BF16 = jnp.bfloat16

D_MODEL = 1024
ATT_HEADS = 8
ATT_HEAD_DIM = 64
ATT_WIDTH = ATT_HEADS * ATT_HEAD_DIM
IDX_HEADS = 8
IDX_DIM = 64
TOPK_MAX = 256
PAGE_SIZE = 128
GLA_HEADS = 4
GLA_DK = 64
GLA_DV = 128
GLA_KEY_WIDTH = GLA_HEADS * GLA_DK
GLA_WIDTH = GLA_HEADS * GLA_DV
GLA_GATE_RANK = 16
GLA_GATE_TAU = 16.0
REL_BUCKETS = 32
REL_MAX_DIST = 128
RMS_EPS = 1e-6
LOG2E = math.log2(math.e)

LANE = 128
SUBLANE = 8
BF16_ROWS = 16
MXU_WIDTH = 256
Q_TILE = MXU_WIDTH
K_CHUNK = MXU_WIDTH
HEAD_GROUP = MXU_WIDTH // ATT_HEAD_DIM
ACC_ROWS = ATT_HEAD_DIM + BF16_ROWS
VMEM_LIMIT = 56 * 1024 * 1024

SEG_Q = (0, 512)
SEG_K = (512, 1024)
SEG_V = (1024, 1536)
SEG_IQ = (1536, 2048)
SEG_IKW = (2048, 2176)
SEG_GLA = (2176, 3840)
PROJ_PACKED = 3840
GLA_IN_WIDTH = SEG_GLA[1] - SEG_GLA[0]

NT_DIMS = (((1,), (1,)), ((), ()))
TN_DIMS = (((0,), (0,)), ((), ()))


def _rms(x, g):
    return x * lax.rsqrt(jnp.mean(x * x, axis=-1, keepdims=True) + RMS_EPS) * g


def _inproj_body(x_ref, g_ref, w_ref, *rest, tokens_minor):
    a = _rms(x_ref[...], g_ref[...]).astype(BF16)

    def proj(seg):
        return jnp.dot(a, w_ref[:, seg[0]:seg[1]], preferred_element_type=F32)

    if tokens_minor:
        wt_ref, q_ref, kb_ref, iq_ref, ikw_ref, ikb_ref, gla_ref, kt_ref, vt_ref, vtb_ref, ikt_ref = rest
    else:
        q_ref, kb_ref, iq_ref, ikw_ref, ikb_ref, gla_ref, k_ref, v_ref, vb_ref = rest

    q_ref[...] = proj(SEG_Q).astype(BF16)
    kf = proj(SEG_K)
    kb_ref[...] = kf.astype(BF16)
    iq_ref[...] = proj(SEG_IQ).astype(BF16)
    lane = lax.broadcasted_iota(jnp.int32, (1, LANE), 1)
    is_w = (lane >= IDX_DIM) & (lane < IDX_DIM + IDX_HEADS)
    ikw = proj(SEG_IKW) * jnp.where(is_w, IDX_HEADS ** -0.5, 1.0)
    ikw_ref[...] = ikw
    ikb_ref[...] = ikw.astype(BF16)
    gla_ref[...] = proj(SEG_GLA)
    if tokens_minor:
        def proj_t(lo, hi):
            return lax.dot_general(wt_ref[lo:hi, :], a, NT_DIMS, preferred_element_type=F32)

        kt_ref[0] = proj_t(0, ATT_WIDTH)
        vt = proj_t(ATT_WIDTH, 2 * ATT_WIDTH)
        vt_ref[0] = vt
        vtb_ref[0] = vt.astype(BF16)
        ikt_ref[0] = proj_t(2 * ATT_WIDTH, 2 * ATT_WIDTH + IDX_DIM)
    else:
        k_ref[...] = kf
        vf = proj(SEG_V)
        v_ref[...] = vf
        vb_ref[...] = vf.astype(BF16)


def _inproj(x, g, w_packed, w_t, tm, seq):
    t = x.shape[0]
    tokens_minor = w_t is not None
    tiles_per_seq = seq // tm
    row = lambda w: pl.BlockSpec((tm, w), lambda i: (i, 0))
    const = lambda s: pl.BlockSpec(s, lambda i: (0, 0))
    col = lambda r: pl.BlockSpec((1, r, tm), lambda i: (i // tiles_per_seq, 0, i % tiles_per_seq))
    rows_shape = lambda w, dt: jax.ShapeDtypeStruct((t, w), dt)
    cols_shape = lambda r, dt: jax.ShapeDtypeStruct((t // seq, r, seq), dt)
    out_shape = [rows_shape(ATT_WIDTH, BF16), rows_shape(ATT_WIDTH, BF16),
                 rows_shape(ATT_WIDTH, BF16), rows_shape(LANE, F32), rows_shape(LANE, BF16),
                 rows_shape(GLA_IN_WIDTH, F32)]
    out_specs = [row(ATT_WIDTH), row(ATT_WIDTH), row(ATT_WIDTH), row(LANE), row(LANE),
                 row(GLA_IN_WIDTH)]
    in_specs = [row(D_MODEL), const((1, D_MODEL)), const((D_MODEL, PROJ_PACKED))]
    operands = [x, g, w_packed]
    if tokens_minor:
        in_specs.append(const(w_t.shape))
        operands.append(w_t)
        out_shape += [cols_shape(ATT_WIDTH, F32), cols_shape(ATT_WIDTH, F32),
                      cols_shape(ATT_WIDTH, BF16), cols_shape(IDX_DIM, F32)]
        out_specs += [col(ATT_WIDTH), col(ATT_WIDTH), col(ATT_WIDTH), col(IDX_DIM)]
    else:
        out_shape += [rows_shape(ATT_WIDTH, F32), rows_shape(ATT_WIDTH, F32),
                      rows_shape(ATT_WIDTH, BF16)]
        out_specs += [row(ATT_WIDTH), row(ATT_WIDTH), row(ATT_WIDTH)]
    return pl.pallas_call(
        functools.partial(_inproj_body, tokens_minor=tokens_minor),
        grid=(t // tm,),
        in_specs=in_specs,
        out_specs=tuple(out_specs),
        out_shape=tuple(out_shape),
        compiler_params=pltpu.CompilerParams(dimension_semantics=("arbitrary",),
                                             vmem_limit_bytes=VMEM_LIMIT),
        name="inproj",
    )(*operands)


def _log_sigmoid(z):
    return jnp.minimum(z, 0.0) - jnp.log1p(jnp.exp(-jnp.abs(z)))


def _gla_body(gin_ref, s0_ref, wgu_ref, bg_ref, gg_ref, o_ref, sout_ref, st_ref, *, chunk, n_valid):
    @pl.when(pl.program_id(1) == 0)
    def _():
        st_ref[...] = s0_ref[0]

    n_chunks = gin_ref.shape[1] // chunk
    ri = lax.broadcasted_iota(jnp.int32, (chunk, chunk), 0)
    ci = lax.broadcasted_iota(jnp.int32, (chunk, chunk), 1)
    causal = ri >= ci
    tri = causal.astype(F32)
    row_valid = lax.broadcasted_iota(jnp.int32, (chunk, 1), 0) < n_valid

    def one_chunk(c, carry):
        r0 = pl.multiple_of(c * chunk, chunk)
        blk = gin_ref[0, pl.ds(r0, chunk), :]
        gq = blk[:, 0:256]
        gk = blk[:, 256:512]
        gv = blk[:, 512:1024]
        glr = blk[:, 1024:1152]
        gout = blk[:, 1152:1664]
        z = jnp.dot(glr.astype(BF16), wgu_ref[...], preferred_element_type=F32) + bg_ref[...]
        la = _log_sigmoid(z) / GLA_GATE_TAU
        if n_valid < chunk:
            la = jnp.where(row_valid, la, 0.0)
        b = jnp.dot(tri, la, precision=lax.Precision.HIGHEST, preferred_element_type=F32)
        b_last = b[chunk - 1:chunk, :]
        qt = gq * jnp.exp(b)
        kt = gk * jnp.exp(-b)
        kd = gk * jnp.exp(b_last - b)
        dec = jnp.exp(b_last)
        st = st_ref[...]
        o_parts = []
        st_parts = []
        for h in range(GLA_HEADS):
            ks = slice(GLA_DK * h, GLA_DK * (h + 1))
            vs = slice(GLA_DV * h, GLA_DV * (h + 1))
            qh = qt[:, ks].astype(BF16)
            kh = kt[:, ks].astype(BF16)
            kdh = kd[:, ks].astype(BF16)
            vh = gv[:, vs].astype(BF16)
            sth = st[:, ks]
            attn = lax.dot_general(qh, kh, NT_DIMS, preferred_element_type=F32)
            attn = jnp.where(causal, attn, 0.0)
            o = jnp.dot(attn.astype(BF16), vh, preferred_element_type=F32)
            o = o + lax.dot_general(qh, sth.astype(BF16), NT_DIMS, preferred_element_type=F32)
            upd = lax.dot_general(vh, kdh, TN_DIMS, preferred_element_type=F32)
            st_parts.append(dec[:, ks] * sth + upd)
            o_parts.append(_rms(o, gg_ref[...]))
        st_ref[...] = jnp.concatenate(st_parts, axis=1)
        o_all = jnp.concatenate(o_parts, axis=1)
        gate = gout * jax.nn.sigmoid(gout)
        o_ref[0, pl.ds(r0, chunk), :] = (o_all * gate).astype(BF16)
        return carry

    lax.fori_loop(0, n_chunks, one_chunk, 0)
    sout_ref[0] = st_ref[...]


def _gla(gin, s0_t, wgu, bg, gg, *, tile, chunk, n_valid):
    nseq, t, _ = gin.shape
    body = functools.partial(_gla_body, chunk=chunk, n_valid=n_valid)
    return pl.pallas_call(
        body,
        grid=(nseq, t // tile),
        in_specs=[
            pl.BlockSpec((1, tile, GLA_IN_WIDTH), lambda b, j: (b, j, 0)),
            pl.BlockSpec((1, GLA_DV, GLA_KEY_WIDTH), lambda b, j: (b, 0, 0)),
            pl.BlockSpec((LANE, GLA_KEY_WIDTH), lambda b, j: (0, 0)),
            pl.BlockSpec((1, GLA_KEY_WIDTH), lambda b, j: (0, 0)),
            pl.BlockSpec((1, GLA_DV), lambda b, j: (0, 0)),
        ],
        out_specs=(
            pl.BlockSpec((1, tile, GLA_WIDTH), lambda b, j: (b, j, 0)),
            pl.BlockSpec((1, GLA_DV, GLA_KEY_WIDTH), lambda b, j: (b, 0, 0)),
        ),
        out_shape=(
            jax.ShapeDtypeStruct((nseq, t, GLA_WIDTH), BF16),
            jax.ShapeDtypeStruct((nseq, GLA_DV, GLA_KEY_WIDTH), F32),
        ),
        scratch_shapes=[pltpu.VMEM((GLA_DV, GLA_KEY_WIDTH), F32)],
        compiler_params=pltpu.CompilerParams(dimension_semantics=("arbitrary", "arbitrary"),
                                             vmem_limit_bytes=VMEM_LIMIT),
        name="gla",
    )(gin, s0_t, wgu, bg, gg)


def _rel_bucket(dist):
    n = jnp.maximum(dist, 0)
    exact = REL_BUCKETS // 2
    nf = jnp.maximum(n, 1).astype(F32)
    large = exact + (jnp.log(nf / exact) / math.log(REL_MAX_DIST / exact)
                     * (REL_BUCKETS - exact)).astype(jnp.int32)
    large = jnp.minimum(large, REL_BUCKETS - 1)
    return jnp.where(n < exact, n, large)


def _select_topk(sc_ref, n_queries, n_chunks, n_allowed, topk, keys_axis, max_iter=18):
    neg_inf = -jnp.inf
    n_acc = 4 if keys_axis == 0 else K_CHUNK // LANE
    if keys_axis == 1:
        part_shape = (n_queries, LANE)
        q_shape = (n_queries, 1)
        chunk_shape = (n_queries, K_CHUNK)
        load = lambda k0: sc_ref[:, pl.ds(k0, K_CHUNK)]
        pieces = lambda x: [x[:, j * LANE:(j + 1) * LANE] for j in range(K_CHUNK // LANE)]
        last = lambda r: r[:, K_CHUNK - 1:K_CHUNK]
    else:
        part_shape = (SUBLANE, n_queries)
        q_shape = (1, n_queries)
        chunk_shape = (K_CHUNK, n_queries)
        load = lambda k0: sc_ref[pl.ds(k0, K_CHUNK), :]
        pieces = lambda x: [x[j * SUBLANE:(j + 1) * SUBLANE] for j in range(K_CHUNK // SUBLANE)]
        last = lambda r: r[K_CHUNK - 1:K_CHUNK, :]

    def store(k0, val):
        if keys_axis == 1:
            sc_ref[:, pl.ds(k0, K_CHUNK)] = val
        else:
            sc_ref[pl.ds(k0, K_CHUNK), :] = val

    def sweep(fns, inits, reds, combines):
        n_stat = len(fns)

        def body(c, accs):
            x = load(pl.multiple_of(c * K_CHUNK, K_CHUNK))
            accs = [list(a) for a in accs]
            for n, piece in enumerate(pieces(x)):
                for s in range(n_stat):
                    accs[s][n % n_acc] = fns[s](accs[s][n % n_acc], piece)
            return tuple(tuple(a) for a in accs)

        init = tuple(tuple(jnp.full(part_shape, inits[s], F32) for _ in range(n_acc))
                     for s in range(n_stat))
        accs = lax.fori_loop(0, n_chunks, body, init)
        out = []
        for s in range(n_stat):
            acc = accs[s][0]
            for other in accs[s][1:]:
                acc = combines[s](acc, other)
            out.append(reds[s](acc, axis=keys_axis, keepdims=True))
        return out

    def count_ge(t):
        tb = jnp.broadcast_to(t, part_shape)
        return sweep([lambda a, x: jnp.where(x >= tb, a + 1.0, a)], [0.0], [jnp.sum],
                     [jnp.add])[0]

    def count_gt(t):
        tb = jnp.broadcast_to(t, part_shape)
        return sweep([lambda a, x: a + jnp.where(x > tb, 1.0, 0.0)], [0.0], [jnp.sum],
                     [jnp.add])[0]

    def max_below(t):
        tb = jnp.broadcast_to(t, part_shape)
        return sweep([lambda a, x: jnp.maximum(a, jnp.where(x < tb, x, neg_inf))], [neg_inf],
                     [jnp.max], [jnp.maximum])[0]

    row_max, row_min = sweep(
        [jnp.maximum, lambda a, x: jnp.minimum(a, jnp.where(x == neg_inf, jnp.inf, x))],
        [neg_inf, jnp.inf], [jnp.max, jnp.min], [jnp.maximum, jnp.minimum])

    k = float(topk)
    take_all = n_allowed <= k
    lo0 = row_min
    hi0 = row_max + jnp.maximum(jnp.abs(row_max) * 1e-6, 1e-30)
    t0 = jnp.full(q_shape, jnp.finfo(F32).min, F32)
    done0 = jnp.where(take_all, 1.0, 0.0)

    def bis_body(_, st):
        lo, hi, t, done = st
        mid = lo + (hi - lo) * 0.5
        c = count_ge(mid)
        hit = (c == k) & (done < 0.5)
        t = jnp.where(hit, mid, t)
        done = jnp.where(hit, 1.0, done)
        lo = jnp.where(c > k, mid, lo)
        hi = jnp.where(c < k, mid, hi)
        return lo, hi, t, done

    lo, hi, t, done = lax.fori_loop(0, max_iter, bis_body, (lo0, hi0, t0, done0))

    all_done = jnp.min(done) > 0.5

    def write(fn):
        def body(c, carry):
            k0 = pl.multiple_of(c * K_CHUNK, K_CHUNK)
            sel, carry = fn(load(k0), carry)
            store(k0, jnp.where(sel, 0.0, neg_inf))
            return carry
        return body

    def walk_down():
        def snap_cond(st):
            return jnp.min(st[2]) < 0.5

        def snap_body(st):
            hi_, t_, done_, over_ = st
            v = max_below(hi_)
            c = count_ge(v)
            ok = (c >= k) & (done_ < 0.5)
            t_ = jnp.where(ok, v, t_)
            over_ = jnp.where(ok, c - k, over_)
            hi_ = jnp.where((done_ < 0.5) & jnp.logical_not(ok), v, hi_)
            done_ = jnp.where(ok, 1.0, done_)
            return hi_, t_, done_, over_

        _, t_, _, over_ = lax.while_loop(snap_cond, snap_body,
                                         (hi, t, done, jnp.zeros(q_shape, F32)))
        return t_, over_

    tv, over = lax.cond(all_done, lambda: (t, jnp.zeros(q_shape, F32)), walk_down)
    has_ties = jnp.max(over) > 0.5

    @pl.when(jnp.logical_not(has_ties))
    def _():
        tb = jnp.broadcast_to(tv, chunk_shape)
        lax.fori_loop(0, n_chunks, write(lambda x, cr: (x >= tb, cr)), 0)

    @pl.when(has_ties)
    def _():
        quota = k - count_gt(tv)
        tb = jnp.broadcast_to(tv, chunk_shape)
        qb = jnp.broadcast_to(quota, chunk_shape)
        ui = lax.broadcasted_iota(jnp.int32, (K_CHUNK, K_CHUNK), 0)
        uj = lax.broadcasted_iota(jnp.int32, (K_CHUNK, K_CHUNK), 1)
        prefix = ((ui <= uj) if keys_axis == 1 else (ui >= uj)).astype(BF16)

        def tie_fn(x, seen):
            tie = x == tb
            if keys_axis == 1:
                rank = jnp.dot(tie.astype(BF16), prefix, preferred_element_type=F32) + seen
            else:
                rank = jnp.dot(prefix, tie.astype(BF16), preferred_element_type=F32) + seen
            sel = (x > tb) | (tie & (rank <= qb))
            return sel, last(rank)

        lax.fori_loop(0, n_chunks, write(tie_fn), jnp.zeros(q_shape, F32))


def _head_lane_mask(h, width):
    lane = lax.broadcasted_iota(jnp.int32, (1, width), 1)
    lo = (h % HEAD_GROUP) * ATT_HEAD_DIM
    return (lane >= lo) & (lane < lo + ATT_HEAD_DIM)


def _pattn_body(rb_ref, q_ref, iq_ref, w_ref, k_ref, vt_ref, ik_ref, o_ref,
                sc_ref, qp_ref, iqp_ref, bias_ref, m_ref, alpha_ref, acc_ref, x_ref, p_ref,
                *, topk):
    i = pl.program_id(1)
    n_chunks = i + 1

    @pl.when((pl.program_id(0) == 0) & (i == 0))
    def _():
        kk = lax.broadcasted_iota(jnp.int32, (K_CHUNK, Q_TILE), 0)
        qq = lax.broadcasted_iota(jnp.int32, (K_CHUNK, Q_TILE), 1)
        for delta in range(2):
            bucket = _rel_bucket(qq - kk + Q_TILE * delta)
            for h in range(ATT_HEADS):
                acc = jnp.zeros((K_CHUNK, Q_TILE), F32)
                for bk in range(REL_BUCKETS - 1):
                    acc = jnp.where(bucket == bk, rb_ref[bk, h] - rb_ref[REL_BUCKETS - 1, h], acc)
                bias_ref[delta, h] = acc * LOG2E
        bias_ref[2] = jnp.zeros(bias_ref.shape[1:], F32)

    q = q_ref[0]
    iq = iq_ref[0]
    for h in range(ATT_HEADS):
        g = h // HEAD_GROUP
        qp_ref[h] = jnp.where(_head_lane_mask(h, MXU_WIDTH),
                              q[:, MXU_WIDTH * g:MXU_WIDTH * (g + 1)], 0)
        iqp_ref[h] = jnp.concatenate(
            [iq[:, IDX_DIM * h:IDX_DIM * (h + 1)], jnp.zeros((Q_TILE, LANE - IDX_DIM), BF16)], axis=1)
    w_t = jnp.transpose(w_ref[0])[IDX_DIM:IDX_DIM + IDX_HEADS, :]

    q_pos = i * Q_TILE + lax.broadcasted_iota(jnp.int32, (1, Q_TILE), 1)

    def score_chunk(c, carry, *, diagonal):
        k0 = pl.multiple_of(c * K_CHUNK, K_CHUNK)
        ikc = ik_ref[0, pl.ds(k0, K_CHUNK), :]
        acc = jnp.zeros((K_CHUNK, Q_TILE), F32)
        for h in range(IDX_HEADS):
            s = lax.dot_general(ikc, iqp_ref[h], NT_DIMS, preferred_element_type=F32)
            acc = acc + jnp.maximum(s, 0.0) * w_t[h:h + 1, :]
        if diagonal:
            k_pos = k0 + lax.broadcasted_iota(jnp.int32, (K_CHUNK, 1), 0)
            acc = jnp.where(k_pos <= q_pos, acc, -jnp.inf)
        sc_ref[pl.ds(k0, K_CHUNK), :] = acc
        return carry

    def score_pair(a, carry):
        score_chunk(2 * a, carry, diagonal=False)
        return score_chunk(2 * a + 1, carry, diagonal=False)

    lax.fori_loop(0, i // 2, score_pair, 0)
    lax.fori_loop(2 * (i // 2), i, functools.partial(score_chunk, diagonal=False), 0)
    score_chunk(i, 0, diagonal=True)

    _select_topk(sc_ref, Q_TILE, n_chunks, (q_pos + 1).astype(F32), topk, keys_axis=0)

    m_ref[...] = jnp.full(m_ref.shape, -1e30, F32)
    alpha_ref[...] = jnp.ones(alpha_ref.shape, F32)
    acc_ref[...] = jnp.zeros(acc_ref.shape, F32)
    p_ref[0] = jnp.zeros(p_ref.shape[1:], BF16)
    x_ref[1] = jnp.full(x_ref.shape[1:], -jnp.inf, F32)
    masked_chunk = sc_ref.shape[0] - K_CHUNK
    sc_ref[pl.ds(masked_chunk, K_CHUNK), :] = jnp.full((K_CHUNK, Q_TILE), -jnp.inf, F32)
    ones_rows = jnp.ones((BF16_ROWS, K_CHUNK), BF16)
    row_block = 4 * SUBLANE

    def logits(h, slot, k0, mb, near):
        g = h // HEAD_GROUP
        kc = k_ref[0, pl.ds(k0, K_CHUNK), MXU_WIDTH * g:MXU_WIDTH * (g + 1)]
        s = lax.dot_general(kc, qp_ref[h], NT_DIMS, preferred_element_type=F32)
        mx = jnp.full((SUBLANE, Q_TILE), -jnp.inf, F32)
        for r in range(K_CHUNK // row_block):
            rs = slice(r * row_block, (r + 1) * row_block)
            xb = s[rs] + mb[rs]
            if near is not None:
                xb = xb + bias_ref[near, h, rs, :]
            x_ref[slot, h, rs, :] = xb
            for u in range(row_block // SUBLANE):
                mx = jnp.maximum(mx, xb[u * SUBLANE:(u + 1) * SUBLANE])
        m_old = m_ref[1 - slot, h]
        m_new = jnp.maximum(m_old, jnp.max(mx, axis=0, keepdims=True))
        alpha_ref[slot, h] = jnp.exp2(m_old - m_new)
        m_ref[slot, h] = m_new

    def probs(h, slot):
        p_ref[slot, h] = jnp.exp2(x_ref[slot, h] - m_ref[slot, h]).astype(BF16)

    def accumulate(h, slot, k0):
        vt = vt_ref[0, ATT_HEAD_DIM * h:ATT_HEAD_DIM * (h + 1), pl.ds(k0, K_CHUNK)]
        lhs = jnp.concatenate([vt, ones_rows], axis=0)
        acc_ref[h] = alpha_ref[slot, h] * acc_ref[h] + jnp.dot(
            lhs, p_ref[slot, h], preferred_element_type=F32)

    def one_step(t, slot, with_bias):
        k_new = pl.multiple_of(jnp.minimum(t, i) * K_CHUNK, K_CHUNK)
        k_old = pl.multiple_of(jnp.clip(t - 2, 0, i) * K_CHUNK, K_CHUNK)
        m0 = pl.multiple_of(jnp.where(t <= i, t * K_CHUNK, masked_chunk), K_CHUNK)
        mb = sc_ref[pl.ds(m0, K_CHUNK), :]
        near = jnp.clip(i - t, 0, 2) if with_bias else None
        for h in range(ATT_HEADS):
            accumulate(h, slot, k_old)
            probs(h, 1 - slot)
            logits(h, slot, k_new, mb, near)

    def two_steps(a, carry, *, with_bias):
        one_step(2 * a, 0, with_bias)
        one_step(2 * a + 1, 1, with_bias)
        return carry

    far_pairs = jnp.maximum(i - 1, 0) // 2
    all_pairs = (i + 4) // 2
    lax.fori_loop(0, far_pairs, functools.partial(two_steps, with_bias=False), 0)
    lax.fori_loop(far_pairs, all_pairs, functools.partial(two_steps, with_bias=True), 0)

    out_t = jnp.concatenate(
        [acc_ref[h, 0:ATT_HEAD_DIM, :] / acc_ref[h, ATT_HEAD_DIM:ATT_HEAD_DIM + 1, :]
         for h in range(ATT_HEADS)], axis=0)
    o_ref[0] = jnp.transpose(out_t).astype(BF16)


def _prompt_attention(rel_bias, q, iq, ikw, kb, vt, ikb, topk):
    bsz, s, _ = q.shape
    tile = lambda w: pl.BlockSpec((1, Q_TILE, w), lambda b, i: (b, i, 0))
    per_batch = lambda shape: pl.BlockSpec((1,) + shape, lambda b, i: (b, 0, 0),
                                           pipeline_mode=pl.Buffered(1))
    body = functools.partial(_pattn_body, topk=topk)
    return pl.pallas_call(
        body,
        grid=(bsz, s // Q_TILE),
        in_specs=[pl.BlockSpec(memory_space=pltpu.SMEM),
                  tile(ATT_WIDTH), tile(ATT_WIDTH), tile(LANE),
                  per_batch((s, ATT_WIDTH)), per_batch((ATT_WIDTH, s)), per_batch((s, LANE))],
        out_specs=tile(ATT_WIDTH),
        out_shape=jax.ShapeDtypeStruct((bsz, s, ATT_WIDTH), BF16),
        scratch_shapes=[
            pltpu.VMEM((s + K_CHUNK, Q_TILE), F32),
            pltpu.VMEM((ATT_HEADS, Q_TILE, MXU_WIDTH), BF16),
            pltpu.VMEM((IDX_HEADS, Q_TILE, LANE), BF16),
            pltpu.VMEM((3, ATT_HEADS, K_CHUNK, Q_TILE), F32),
            pltpu.VMEM((2, ATT_HEADS, 1, Q_TILE), F32),
            pltpu.VMEM((2, ATT_HEADS, 1, Q_TILE), F32),
            pltpu.VMEM((ATT_HEADS, ACC_ROWS, Q_TILE), F32),
            pltpu.VMEM((2, ATT_HEADS, K_CHUNK, Q_TILE), F32),
            pltpu.VMEM((2, ATT_HEADS, K_CHUNK, Q_TILE), BF16),
        ],
        compiler_params=pltpu.CompilerParams(dimension_semantics=("arbitrary", "arbitrary"),
                                             vmem_limit_bytes=VMEM_LIMIT),
        name="prompt_attention",
    )(rel_bias, q, iq, ikw, kb, vt, ikb)


SCORE_PAGES = 16
ATTN_PAGES = 16
NEW_PAD = 16


def _head_sums(y, dec_seq):
    return jnp.concatenate(
        [jnp.sum(y[IDX_HEADS * t:IDX_HEADS * (t + 1)], axis=0, keepdims=True)
         for t in range(dec_seq)], axis=0)


def _sscore_body(pt_ref, iq_ref, w_ref, *rest, dec_seq):
    del pt_ref
    pages = rest[:SCORE_PAGES]
    out_ref = rest[SCORE_PAGES]
    iq = iq_ref[0]
    w = w_ref[0]
    for m in range(SCORE_PAGES):
        s = jnp.dot(iq, pages[m][0].astype(BF16), preferred_element_type=F32)
        out_ref[0, :, m * PAGE_SIZE:(m + 1) * PAGE_SIZE] = _head_sums(jnp.maximum(s, 0.0) * w,
                                                                      dec_seq)


def _sample_scores(page_table, iq_rows, w_rows, cache_idx_t, dec_seq):
    nseq, n_pages = page_table.shape
    steps = n_pages // SCORE_PAGES
    rows = dec_seq * IDX_HEADS

    def page_spec(m):
        return pl.BlockSpec((1, IDX_DIM, PAGE_SIZE),
                            lambda b, j, pt: (pt[b, j * SCORE_PAGES + m], 0, 0))

    grid_spec = pltpu.PrefetchScalarGridSpec(
        num_scalar_prefetch=1,
        grid=(nseq, steps),
        in_specs=[pl.BlockSpec((1, rows, IDX_DIM), lambda b, j, pt: (b, 0, 0)),
                  pl.BlockSpec((1, rows, 1), lambda b, j, pt: (b, 0, 0))]
                 + [page_spec(m) for m in range(SCORE_PAGES)],
        out_specs=pl.BlockSpec((1, dec_seq, SCORE_PAGES * PAGE_SIZE), lambda b, j, pt: (b, 0, j)),
    )
    return pl.pallas_call(
        functools.partial(_sscore_body, dec_seq=dec_seq),
        grid_spec=grid_spec,
        out_shape=jax.ShapeDtypeStruct((nseq, dec_seq, n_pages * PAGE_SIZE), F32),
        compiler_params=pltpu.CompilerParams(dimension_semantics=("arbitrary", "arbitrary"),
                                             vmem_limit_bytes=VMEM_LIMIT),
        name="sample_scores",
    )(page_table, iq_rows, w_rows, *([cache_idx_t] * SCORE_PAGES))


def _sattn_body(pt_ref, rb_ref, sc_in_ref, iq_ref, w_ref, iknew_ref, qp_ref, knew_ref, vnew_ref,
                *rest, dec_seq, past, topk):
    del pt_ref
    kpages = rest[:ATTN_PAGES]
    vpages = rest[ATTN_PAGES:2 * ATTN_PAGES]
    o_ref = rest[2 * ATTN_PAGES]
    sc_ref, mb_ref, m_ref, l_ref, acc_ref = rest[2 * ATTN_PAGES + 1:]
    j = pl.program_id(1)
    n_steps = pl.num_programs(1)
    rows = dec_seq * ATT_HEADS
    n_chunks = past // K_CHUNK + 1
    row_t = lax.broadcasted_iota(jnp.int32, (rows, 1), 0) // ATT_HEADS
    row_h = lax.broadcasted_iota(jnp.int32, (rows, 1), 0) % ATT_HEADS

    @pl.when(j == 0)
    def _():
        s = jnp.dot(iq_ref[0], iknew_ref[0], preferred_element_type=F32)
        new_sc = _head_sums(jnp.maximum(s, 0.0) * w_ref[0], dec_seq)
        t_idx = lax.broadcasted_iota(jnp.int32, (dec_seq, K_CHUNK), 0)
        k_idx = lax.broadcasted_iota(jnp.int32, (dec_seq, K_CHUNK), 1)
        padded = jnp.concatenate(
            [new_sc, jnp.zeros((dec_seq, K_CHUNK - PAGE_SIZE), F32)], axis=1)
        sc_ref[:, 0:past] = sc_in_ref[0]
        sc_ref[:, past:past + K_CHUNK] = jnp.where(k_idx <= t_idx, padded, -jnp.inf)
        n_allowed = (past + 1 + lax.broadcasted_iota(jnp.int32, (dec_seq, 1), 0)).astype(F32)
        _select_topk(sc_ref, dec_seq, n_chunks, n_allowed, topk, keys_axis=1)
        for t in range(dec_seq):
            mb_ref[ATT_HEADS * t:ATT_HEADS * (t + 1), :] = jnp.broadcast_to(
                sc_ref[t:t + 1, :], (ATT_HEADS, past + K_CHUNK))
        m_ref[...] = jnp.full(m_ref.shape, -1e30, F32)
        l_ref[...] = jnp.zeros(l_ref.shape, F32)
        acc_ref[...] = jnp.zeros(acc_ref.shape, F32)

    qp = qp_ref[0]

    def near_bias(dist):
        bucket = _rel_bucket(dist)
        acc = jnp.zeros(dist.shape, F32)
        for bk in range(REL_BUCKETS - 1):
            per_row = jnp.zeros((rows, 1), F32)
            for h in range(ATT_HEADS):
                per_row = jnp.where(row_h == h, rb_ref[bk, h] - rb_ref[REL_BUCKETS - 1, h], per_row)
            acc = jnp.where(bucket == bk, per_row, acc)
        return acc * LOG2E

    def attend_pages(k_ts, v_ts, mb, last_bias):
        logit = jnp.concatenate(
            [jnp.dot(qp, k_t, preferred_element_type=F32) for k_t in k_ts], axis=1) + mb
        n = len(k_ts)
        if n > 1:
            last_bias = jnp.concatenate(
                [jnp.zeros((rows, (n - 1) * PAGE_SIZE), F32), last_bias], axis=1)
        logit = logit + last_bias
        m_old = m_ref[...]
        m_new = jnp.maximum(m_old, jnp.max(logit, axis=1, keepdims=True))
        alpha = jnp.exp2(m_old - m_new)
        p = jnp.exp2(logit - m_new).astype(BF16)
        l_ref[...] = alpha * l_ref[...] + jnp.sum(p.astype(F32), axis=1, keepdims=True)
        acc = alpha * acc_ref[...]
        for m, v_t in enumerate(v_ts):
            acc = acc + lax.dot_general(p[:, m * PAGE_SIZE:(m + 1) * PAGE_SIZE], v_t, NT_DIMS,
                                        preferred_element_type=F32)
        acc_ref[...] = acc
        m_ref[...] = m_new

    is_last = j == n_steps - 1
    off = lax.broadcasted_iota(jnp.int32, (rows, PAGE_SIZE), 1)
    k0 = pl.multiple_of(j * (ATTN_PAGES * PAGE_SIZE), ATTN_PAGES * PAGE_SIZE)
    last_bias = lax.cond(is_last, lambda: near_bias(PAGE_SIZE + row_t - off),
                         lambda: jnp.zeros((rows, PAGE_SIZE), F32))
    attend_pages([kp[0].astype(BF16) for kp in kpages], [vp[0].astype(BF16) for vp in vpages],
                 mb_ref[:, pl.ds(k0, ATTN_PAGES * PAGE_SIZE)], last_bias)

    @pl.when(is_last)
    def _():
        attend_pages([knew_ref[0]], [vnew_ref[0]], mb_ref[:, past:past + PAGE_SIZE],
                     near_bias(row_t - off))
        res = acc_ref[...] / l_ref[...]
        lane_h = lax.broadcasted_iota(jnp.int32, (rows, ATT_WIDTH), 1) // ATT_HEAD_DIM
        res = jnp.where(lane_h == row_h, res, 0.0)
        o_ref[0] = _head_sums(res, dec_seq).astype(BF16)


def _sample_attention(page_table, rel_bias, scores, iq_rows, w_rows, iknew_t, qp_rows, knew_t,
                      vnew_t, cache_k_t, cache_v_t, dec_seq, topk):
    nseq, n_pages = page_table.shape
    past = n_pages * PAGE_SIZE
    steps = n_pages // ATTN_PAGES
    rows = dec_seq * ATT_HEADS

    def page_spec(m):
        return pl.BlockSpec((1, ATT_WIDTH, PAGE_SIZE),
                            lambda b, j, pt: (pt[b, j * ATTN_PAGES + m], 0, 0))

    per_seq = lambda shape: pl.BlockSpec((1,) + shape, lambda b, j, pt: (b, 0, 0))
    grid_spec = pltpu.PrefetchScalarGridSpec(
        num_scalar_prefetch=1,
        grid=(nseq, steps),
        in_specs=[pl.BlockSpec(memory_space=pltpu.SMEM),
                  per_seq((dec_seq, past)), per_seq((rows, IDX_DIM)), per_seq((rows, 1)),
                  per_seq((IDX_DIM, PAGE_SIZE)), per_seq((rows, ATT_WIDTH)),
                  per_seq((ATT_WIDTH, PAGE_SIZE)), per_seq((ATT_WIDTH, PAGE_SIZE))]
                 + [page_spec(m) for m in range(ATTN_PAGES)] * 2,
        out_specs=per_seq((dec_seq, ATT_WIDTH)),
        scratch_shapes=[
            pltpu.VMEM((dec_seq, past + K_CHUNK), F32),
            pltpu.VMEM((rows, past + K_CHUNK), F32),
            pltpu.VMEM((rows, 1), F32),
            pltpu.VMEM((rows, 1), F32),
            pltpu.VMEM((rows, ATT_WIDTH), F32),
        ],
    )
    return pl.pallas_call(
        functools.partial(_sattn_body, dec_seq=dec_seq, past=past, topk=topk),
        grid_spec=grid_spec,
        out_shape=jax.ShapeDtypeStruct((nseq, dec_seq, ATT_WIDTH), BF16),
        compiler_params=pltpu.CompilerParams(dimension_semantics=("arbitrary", "arbitrary"),
                                             vmem_limit_bytes=VMEM_LIMIT),
        name="sample_attention",
    )(page_table, rel_bias, scores, iq_rows, w_rows, iknew_t, qp_rows, knew_t, vnew_t,
      *([cache_k_t] * ATTN_PAGES), *([cache_v_t] * ATTN_PAGES))


FF_CHUNK = 256


def _post_body(att_ref, gla_ref, x_ref, p_ref, wo_ref, gpm_ref, gpf_ref, wg_ref, wu_ref, wd_ref,
               gpo_ref, wpg_ref, wpp_ref, o_ref, *, d_ff):
    wo = wo_ref
    mixed = jnp.dot(att_ref[...], wo[0:ATT_WIDTH, :], preferred_element_type=F32)
    mixed = mixed + jnp.dot(gla_ref[...], wo[ATT_WIDTH:ATT_WIDTH + GLA_WIDTH, :],
                            preferred_element_type=F32)
    h = x_ref[...] + _rms(mixed, gpm_ref[...])
    f = _rms(h, gpf_ref[...]).astype(BF16)
    ff = jnp.zeros(h.shape, F32)
    for c in range(d_ff // FF_CHUNK):
        cs = slice(c * FF_CHUNK, (c + 1) * FF_CHUNK)
        gate = jnp.dot(f, wg_ref[:, cs], preferred_element_type=F32)
        up = jnp.dot(f, wu_ref[:, cs], preferred_element_type=F32)
        act = (gate * jax.nn.sigmoid(gate) * up).astype(BF16)
        ff = ff + jnp.dot(act, wd_ref[cs, :], preferred_element_type=F32)
    h = h + _rms(ff, gpo_ref[...])
    gate = jax.nn.sigmoid(jnp.dot(h.astype(BF16), wpg_ref[...], preferred_element_type=F32))
    ple = jnp.dot(p_ref[...].astype(BF16), wpp_ref[...], preferred_element_type=F32)
    o_ref[...] = h + gate * ple


def _post(att, gla, x, p, wo, gpm, gpf, wg, wu, wd, gpo, wpg, wpp, tm):
    t = x.shape[0]
    d_ff = wg.shape[1]
    row = lambda w: pl.BlockSpec((tm, w), lambda i: (i, 0))
    const = lambda a: pl.BlockSpec(a.shape, lambda i: (0, 0), pipeline_mode=pl.Buffered(1))
    return pl.pallas_call(
        functools.partial(_post_body, d_ff=d_ff),
        grid=(t // tm,),
        in_specs=[row(ATT_WIDTH), row(GLA_WIDTH), row(D_MODEL), row(p.shape[1]),
                  const(wo), const(gpm), const(gpf), const(wg), const(wu), const(wd),
                  const(gpo), const(wpg), const(wpp)],
        out_specs=row(D_MODEL),
        out_shape=jax.ShapeDtypeStruct((t, D_MODEL), F32),
        compiler_params=pltpu.CompilerParams(dimension_semantics=("arbitrary",),
                                             vmem_limit_bytes=VMEM_LIMIT),
        name="post",
    )(att, gla, x, p, wo, gpm, gpf, wg, wu, wd, gpo, wpg, wpp)


def _pack_w_in(w_in):
    offs = [0]
    for wdt in (ATT_WIDTH, ATT_WIDTH, ATT_WIDTH, IDX_HEADS * IDX_DIM, IDX_DIM, IDX_HEADS,
                GLA_KEY_WIDTH, GLA_KEY_WIDTH, GLA_WIDTH, GLA_GATE_RANK, GLA_WIDTH):
        offs.append(offs[-1] + wdt)
    aq, ak, av, iq, ik, iw, gq, gk, gv, glr, gout = [w_in[:, offs[n]:offs[n + 1]] for n in range(11)]
    d = w_in.shape[0]
    zeros = lambda n: jnp.zeros((d, n), w_in.dtype)
    packed = jnp.concatenate([
        aq * (ATT_HEAD_DIM ** -0.5 * LOG2E), ak, av, iq * (IDX_DIM ** -0.5),
        ik, iw, zeros(LANE - IDX_DIM - IDX_HEADS),
        gq * (GLA_DK ** -0.5), gk, gv, glr, zeros(LANE - GLA_GATE_RANK), gout], axis=1)
    return packed.astype(BF16)


def _state_to_t(s):
    n = s.shape[0]
    return jnp.transpose(s, (0, 3, 1, 2)).reshape(n, GLA_DV, GLA_KEY_WIDTH)


def _state_from_t(st):
    n = st.shape[0]
    return jnp.transpose(st.reshape(n, GLA_DV, GLA_HEADS, GLA_DK), (0, 2, 3, 1))


def _layer(x_prompt, x_sample, p_prompt, p_sample, cache_k, cache_v, cache_idx_k, state_gla,
           page_table, rel_bias, g_pre_mix, w_in, w_gate_up, b_gate, g_gla_out, w_out,
           g_post_mix, g_pre_ffn, w_ff_gate, w_ff_up, w_ff_down, g_post_ffn, w_ple_gate,
           w_ple_proj):
    bsz, seq, d = x_prompt.shape
    nseq, dec_seq, _ = x_sample.shape
    n_pool = cache_k.shape[0]
    past = page_table.shape[1] * PAGE_SIZE
    row2 = lambda g: g.reshape(1, -1)

    w_packed = _pack_w_in(w_in)
    w_t = jnp.transpose(w_packed[:, SEG_K[0]:SEG_IKW[0] + IDX_DIM])
    w_t = jnp.concatenate([w_t[0:2 * ATT_WIDTH], w_t[SEG_IKW[0] - SEG_K[0]:]], axis=0)
    wgu = jnp.concatenate(
        [w_gate_up, jnp.zeros((LANE - GLA_GATE_RANK, GLA_KEY_WIDTH), w_gate_up.dtype)],
        axis=0).astype(BF16)
    post_w = (w_out.astype(BF16), row2(g_post_mix), row2(g_pre_ffn), w_ff_gate.astype(BF16),
              w_ff_up.astype(BF16), w_ff_down.astype(BF16), row2(g_post_ffn),
              w_ple_gate.astype(BF16), w_ple_proj.astype(BF16))
    gla_w = (wgu, row2(b_gate), row2(g_gla_out))

    tp = bsz * seq
    q, kb, iq, ikw, ikb, gin, k_t, v_t, vtb, ik_t = _inproj(
        x_prompt.reshape(tp, d), row2(g_pre_mix), w_packed, w_t, min(512, seq), seq)
    r3 = lambda a: a.reshape(bsz, seq, a.shape[-1])
    att = _prompt_attention(rel_bias, r3(q), r3(iq), r3(ikw), r3(kb), vtb, r3(ikb),
                            min(TOPK_MAX, seq // 4))
    gla_chunk = min(64, seq)
    gla_o, st_p = _gla(r3(gin), jnp.zeros((bsz, GLA_DV, GLA_KEY_WIDTH), F32), *gla_w,
                       tile=min(512, seq), chunk=gla_chunk, n_valid=gla_chunk)
    y_p = _post(att.reshape(tp, ATT_WIDTH), gla_o.reshape(tp, GLA_WIDTH), x_prompt.reshape(tp, d),
                p_prompt.reshape(tp, -1), *post_w, min(256, tp))
    heads_last = lambda a: jnp.transpose(
        a.reshape(bsz, ATT_HEADS, ATT_HEAD_DIM, seq), (0, 3, 1, 2))
    outs_p = (y_p.reshape(bsz, seq, d), heads_last(k_t), heads_last(v_t),
              jnp.swapaxes(ik_t, 1, 2), _state_from_t(st_p))

    ts = nseq * dec_seq
    q, kb, iq, ikw, ikb, gin, kf, vf, vb = _inproj(
        x_sample.reshape(ts, d), row2(g_pre_mix), w_packed, None, min(512, ts), ts)
    rows = dec_seq * ATT_HEADS
    iq_rows = iq.reshape(nseq, rows, IDX_DIM)
    w_rows = ikw[:, IDX_DIM:IDX_DIM + IDX_HEADS].reshape(nseq, rows, 1)
    head_of_col = jnp.arange(ATT_WIDTH, dtype=jnp.int32) // ATT_HEAD_DIM
    head_mask = head_of_col[None, :] == jnp.arange(ATT_HEADS, dtype=jnp.int32)[:, None]
    qp_rows = jnp.where(head_mask[None, None], q.reshape(nseq, dec_seq, 1, ATT_WIDTH),
                        jnp.zeros((), BF16)).reshape(nseq, rows, ATT_WIDTH)
    per_seq = lambda a: a.reshape(nseq, dec_seq, a.shape[-1])
    new_page_t = lambda a: jnp.pad(jnp.swapaxes(per_seq(a), 1, 2),
                                   ((0, 0), (0, 0), (0, PAGE_SIZE - dec_seq)))
    cache_idx_t = jnp.swapaxes(cache_idx_k, 1, 2)
    cache_k_t = jnp.transpose(cache_k, (0, 2, 3, 1)).reshape(n_pool, ATT_WIDTH, PAGE_SIZE)
    cache_v_t = jnp.transpose(cache_v, (0, 2, 3, 1)).reshape(n_pool, ATT_WIDTH, PAGE_SIZE)
    scores = _sample_scores(page_table, iq_rows, w_rows, cache_idx_t, dec_seq)
    att_s = _sample_attention(
        page_table, rel_bias, scores, iq_rows, w_rows, new_page_t(ikb[:, :IDX_DIM]), qp_rows,
        new_page_t(kb), new_page_t(vb), cache_k_t, cache_v_t, dec_seq,
        min(TOPK_MAX, (past + dec_seq) // 4))
    gin_s = jnp.pad(per_seq(gin), ((0, 0), (0, NEW_PAD - dec_seq), (0, 0)))
    gla_s, st_s = _gla(gin_s, _state_to_t(state_gla), *gla_w,
                       tile=NEW_PAD, chunk=NEW_PAD, n_valid=dec_seq)
    y_s = _post(att_s.reshape(ts, ATT_WIDTH), gla_s[:, :dec_seq].reshape(ts, GLA_WIDTH),
                x_sample.reshape(ts, d), p_sample.reshape(ts, -1), *post_w, min(256, ts))
    outs_s = (y_s.reshape(nseq, dec_seq, d),
              kf.reshape(nseq, dec_seq, ATT_HEADS, ATT_HEAD_DIM),
              vf.reshape(nseq, dec_seq, ATT_HEADS, ATT_HEAD_DIM),
              ikw[:, :IDX_DIM].reshape(nseq, dec_seq, IDX_DIM),
              _state_from_t(st_s))
    return outs_p, outs_s


def kernel(x_prompt, x_sample, p_prompt, p_sample, cache_k, cache_v, cache_idx_k, state_gla,
           page_table, rel_bias, g_pre_mix, w_in, w_gate_up, b_gate, g_gla_out, w_out,
           g_post_mix, g_pre_ffn, w_ff_gate, w_ff_up, w_ff_down, g_post_ffn, w_ple_gate,
           w_ple_proj):
    depth = w_in.shape[0]
    h_p, h_s = x_prompt, x_sample
    per_layer = []
    for i in range(depth):
        outs_p, outs_s = _layer(
            h_p, h_s, p_prompt[i], p_sample[i], cache_k[i], cache_v[i], cache_idx_k[i],
            state_gla[i], page_table, rel_bias, g_pre_mix[i], w_in[i], w_gate_up[i], b_gate[i],
            g_gla_out[i], w_out[i], g_post_mix[i], g_pre_ffn[i], w_ff_gate[i], w_ff_up[i],
            w_ff_down[i], g_post_ffn[i], w_ple_gate[i], w_ple_proj[i])
        h_p, h_s = outs_p[0], outs_s[0]
        per_layer.append(outs_p[1:] + outs_s[1:])
    stacked = [jnp.stack([lay[n] for lay in per_layer]) for n in range(8)]
    return (h_p, h_s, stacked[0], stacked[1], stacked[2], stacked[3],
            stacked[4], stacked[5], stacked[6], stacked[7])
```

```python
import functools
import math

import jax
import jax.numpy as jnp
from jax import lax
from jax.experimental import pallas as pl
from jax.experimental.pallas import tpu as pltpu

F32 = jnp.float32
BF16 = jnp.bfloat16

D_MODEL = 1024
ATT_HEADS = 8
ATT_HEAD_DIM = 64
ATT_WIDTH = ATT_HEADS * ATT_HEAD_DIM
IDX_HEADS = 8
IDX_DIM = 64
TOPK_MAX = 256
PAGE_SIZE = 128
GLA_HEADS = 4
GLA_DK = 64
GLA_DV = 128
GLA_KEY_WIDTH = GLA_HEADS * GLA_DK
GLA_WIDTH = GLA_HEADS * GLA_DV
GLA_GATE_RANK = 16
GLA_GATE_TAU = 16.0
REL_BUCKETS = 32
REL_MAX_DIST = 128
RMS_EPS = 1e-6
LOG2E = math.log2(math.e)

LANE = 128
SUBLANE = 8
BF16_ROWS = 16
MXU_WIDTH = 256
Q_TILE = MXU_WIDTH
K_CHUNK = MXU_WIDTH
HEAD_GROUP = MXU_WIDTH // ATT_HEAD_DIM
ACC_ROWS = ATT_HEAD_DIM + BF16_ROWS
VMEM_LIMIT = 56 * 1024 * 1024

SEG_Q = (0, 512)
SEG_K = (512, 1024)
SEG_V = (1024, 1536)
SEG_IQ = (1536, 2048)
SEG_IKW = (2048, 2176)
SEG_GLA = (2176, 3840)
PROJ_PACKED = 3840
GLA_IN_WIDTH = SEG_GLA[1] - SEG_GLA[0]

NT_DIMS = (((1,), (1,)), ((), ()))
TN_DIMS = (((0,), (0,)), ((), ()))


def _rms(x, g):
    return x * lax.rsqrt(jnp.mean(x * x, axis=-1, keepdims=True) + RMS_EPS) * g


def _inproj_body(x_ref, g_ref, w_ref, *rest, tokens_minor):
    a = _rms(x_ref[...], g_ref[...]).astype(BF16)

    def proj(seg):
        return jnp.dot(a, w_ref[:, seg[0]:seg[1]], preferred_element_type=F32)

    if tokens_minor:
        wt_ref, q_ref, kb_ref, iq_ref, ikw_ref, ikb_ref, gla_ref, kt_ref, vt_ref, vtb_ref, ikt_ref = rest
    else:
        q_ref, kb_ref, iq_ref, ikw_ref, ikb_ref, gla_ref, k_ref, v_ref, vb_ref = rest

    q_ref[...] = proj(SEG_Q).astype(BF16)
    kf = proj(SEG_K)
    kb_ref[...] = kf.astype(BF16)
    iq_ref[...] = proj(SEG_IQ).astype(BF16)
    lane = lax.broadcasted_iota(jnp.int32, (1, LANE), 1)
    is_w = (lane >= IDX_DIM) & (lane < IDX_DIM + IDX_HEADS)
    ikw = proj(SEG_IKW) * jnp.where(is_w, IDX_HEADS ** -0.5, 1.0)
    ikw_ref[...] = ikw
    ikb_ref[...] = ikw.astype(BF16)
    gla_ref[...] = proj(SEG_GLA)
    if tokens_minor:
        def proj_t(lo, hi):
            return lax.dot_general(wt_ref[lo:hi, :], a, NT_DIMS, preferred_element_type=F32)

        kt_ref[0] = proj_t(0, ATT_WIDTH)
        vt = proj_t(ATT_WIDTH, 2 * ATT_WIDTH)
        vt_ref[0] = vt
        vtb_ref[0] = vt.astype(BF16)
        ikt_ref[0] = proj_t(2 * ATT_WIDTH, 2 * ATT_WIDTH + IDX_DIM)
    else:
        k_ref[...] = kf
        vf = proj(SEG_V)
        v_ref[...] = vf
        vb_ref[...] = vf.astype(BF16)


def _inproj(x, g, w_packed, w_t, tm, seq):
    t = x.shape[0]
    tokens_minor = w_t is not None
    tiles_per_seq = seq // tm
    row = lambda w: pl.BlockSpec((tm, w), lambda i: (i, 0))
    const = lambda s: pl.BlockSpec(s, lambda i: (0, 0))
    col = lambda r: pl.BlockSpec((1, r, tm), lambda i: (i // tiles_per_seq, 0, i % tiles_per_seq))
    rows_shape = lambda w, dt: jax.ShapeDtypeStruct((t, w), dt)
    cols_shape = lambda r, dt: jax.ShapeDtypeStruct((t // seq, r, seq), dt)
    out_shape = [rows_shape(ATT_WIDTH, BF16), rows_shape(ATT_WIDTH, BF16),
                 rows_shape(ATT_WIDTH, BF16), rows_shape(LANE, F32), rows_shape(LANE, BF16),
                 rows_shape(GLA_IN_WIDTH, F32)]
    out_specs = [row(ATT_WIDTH), row(ATT_WIDTH), row(ATT_WIDTH), row(LANE), row(LANE),
                 row(GLA_IN_WIDTH)]
    in_specs = [row(D_MODEL), const((1, D_MODEL)), const((D_MODEL, PROJ_PACKED))]
    operands = [x, g, w_packed]
    if tokens_minor:
        in_specs.append(const(w_t.shape))
        operands.append(w_t)
        out_shape += [cols_shape(ATT_WIDTH, F32), cols_shape(ATT_WIDTH, F32),
                      cols_shape(ATT_WIDTH, BF16), cols_shape(IDX_DIM, F32)]
        out_specs += [col(ATT_WIDTH), col(ATT_WIDTH), col(ATT_WIDTH), col(IDX_DIM)]
    else:
        out_shape += [rows_shape(ATT_WIDTH, F32), rows_shape(ATT_WIDTH, F32),
                      rows_shape(ATT_WIDTH, BF16)]
        out_specs += [row(ATT_WIDTH), row(ATT_WIDTH), row(ATT_WIDTH)]
    return pl.pallas_call(
        functools.partial(_inproj_body, tokens_minor=tokens_minor),
        grid=(t // tm,),
        in_specs=in_specs,
        out_specs=tuple(out_specs),
        out_shape=tuple(out_shape),
        compiler_params=pltpu.CompilerParams(dimension_semantics=("arbitrary",),
                                             vmem_limit_bytes=VMEM_LIMIT),
        name="inproj",
    )(*operands)


def _log_sigmoid(z):
    return jnp.minimum(z, 0.0) - jnp.log1p(jnp.exp(-jnp.abs(z)))


def _gla_body(gin_ref, s0_ref, wgu_ref, bg_ref, gg_ref, o_ref, sout_ref, st_ref, *, chunk, n_valid):
    @pl.when(pl.program_id(1) == 0)
    def _():
        st_ref[...] = s0_ref[0]

    n_chunks = gin_ref.shape[1] // chunk
    ri = lax.broadcasted_iota(jnp.int32, (chunk, chunk), 0)
    ci = lax.broadcasted_iota(jnp.int32, (chunk, chunk), 1)
    causal = ri >= ci
    tri = causal.astype(F32)
    row_valid = lax.broadcasted_iota(jnp.int32, (chunk, 1), 0) < n_valid

    def one_chunk(c, carry):
        r0 = pl.multiple_of(c * chunk, chunk)
        blk = gin_ref[0, pl.ds(r0, chunk), :]
        gq = blk[:, 0:256]
        gk = blk[:, 256:512]
        gv = blk[:, 512:1024]
        glr = blk[:, 1024:1152]
        gout = blk[:, 1152:1664]
        z = jnp.dot(glr.astype(BF16), wgu_ref[...], preferred_element_type=F32) + bg_ref[...]
        la = _log_sigmoid(z) / GLA_GATE_TAU
        if n_valid < chunk:
            la = jnp.where(row_valid, la, 0.0)
        b = jnp.dot(tri, la, precision=lax.Precision.HIGHEST, preferred_element_type=F32)
        b_last = b[chunk - 1:chunk, :]
        qt = gq * jnp.exp(b)
        kt = gk * jnp.exp(-b)
        kd = gk * jnp.exp(b_last - b)
        dec = jnp.exp(b_last)
        st = st_ref[...]
        o_parts = []
        st_parts = []
        for h in range(GLA_HEADS):
            ks = slice(GLA_DK * h, GLA_DK * (h + 1))
            vs = slice(GLA_DV * h, GLA_DV * (h + 1))
            qh = qt[:, ks].astype(BF16)
            kh = kt[:, ks].astype(BF16)
            kdh = kd[:, ks].astype(BF16)
            vh = gv[:, vs].astype(BF16)
            sth = st[:, ks]
            attn = lax.dot_general(qh, kh, NT_DIMS, preferred_element_type=F32)
            attn = jnp.where(causal, attn, 0.0)
            o = jnp.dot(attn.astype(BF16), vh, preferred_element_type=F32)
            o = o + lax.dot_general(qh, sth.astype(BF16), NT_DIMS, preferred_element_type=F32)
            upd = lax.dot_general(vh, kdh, TN_DIMS, preferred_element_type=F32)
            st_parts.append(dec[:, ks] * sth + upd)
            o_parts.append(_rms(o, gg_ref[...]))
        st_ref[...] = jnp.concatenate(st_parts, axis=1)
        o_all = jnp.concatenate(o_parts, axis=1)
        gate = gout * jax.nn.sigmoid(gout)
        o_ref[0, pl.ds(r0, chunk), :] = (o_all * gate).astype(BF16)
        return carry

    lax.fori_loop(0, n_chunks, one_chunk, 0)
    sout_ref[0] = st_ref[...]


def _gla(gin, s0_t, wgu, bg, gg, *, tile, chunk, n_valid):
    nseq, t, _ = gin.shape
    body = functools.partial(_gla_body, chunk=chunk, n_valid=n_valid)
    return pl.pallas_call(
        body,
        grid=(nseq, t // tile),
        in_specs=[
            pl.BlockSpec((1, tile, GLA_IN_WIDTH), lambda b, j: (b, j, 0)),
            pl.BlockSpec((1, GLA_DV, GLA_KEY_WIDTH), lambda b, j: (b, 0, 0)),
            pl.BlockSpec((LANE, GLA_KEY_WIDTH), lambda b, j: (0, 0)),
            pl.BlockSpec((1, GLA_KEY_WIDTH), lambda b, j: (0, 0)),
            pl.BlockSpec((1, GLA_DV), lambda b, j: (0, 0)),
        ],
        out_specs=(
            pl.BlockSpec((1, tile, GLA_WIDTH), lambda b, j: (b, j, 0)),
            pl.BlockSpec((1, GLA_DV, GLA_KEY_WIDTH), lambda b, j: (b, 0, 0)),
        ),
        out_shape=(
            jax.ShapeDtypeStruct((nseq, t, GLA_WIDTH), BF16),
            jax.ShapeDtypeStruct((nseq, GLA_DV, GLA_KEY_WIDTH), F32),
        ),
        scratch_shapes=[pltpu.VMEM((GLA_DV, GLA_KEY_WIDTH), F32)],
        compiler_params=pltpu.CompilerParams(dimension_semantics=("arbitrary", "arbitrary"),
                                             vmem_limit_bytes=VMEM_LIMIT),
        name="gla",
    )(gin, s0_t, wgu, bg, gg)


def _rel_bucket(dist):
    n = jnp.maximum(dist, 0)
    exact = REL_BUCKETS // 2
    nf = jnp.maximum(n, 1).astype(F32)
    large = exact + (jnp.log(nf / exact) / math.log(REL_MAX_DIST / exact)
                     * (REL_BUCKETS - exact)).astype(jnp.int32)
    large = jnp.minimum(large, REL_BUCKETS - 1)
    return jnp.where(n < exact, n, large)


def _select_topk(sc_ref, n_queries, n_chunks, n_allowed, topk, keys_axis, max_iter=18):
    neg_inf = -jnp.inf
    n_acc = 4 if keys_axis == 0 else K_CHUNK // LANE
    if keys_axis == 1:
        part_shape = (n_queries, LANE)
        q_shape = (n_queries, 1)
        chunk_shape = (n_queries, K_CHUNK)
        load = lambda k0: sc_ref[:, pl.ds(k0, K_CHUNK)]
        pieces = lambda x: [x[:, j * LANE:(j + 1) * LANE] for j in range(K_CHUNK // LANE)]
        last = lambda r: r[:, K_CHUNK - 1:K_CHUNK]
    else:
        part_shape = (SUBLANE, n_queries)
        q_shape = (1, n_queries)
        chunk_shape = (K_CHUNK, n_queries)
        load = lambda k0: sc_ref[pl.ds(k0, K_CHUNK), :]
        pieces = lambda x: [x[j * SUBLANE:(j + 1) * SUBLANE] for j in range(K_CHUNK // SUBLANE)]
        last = lambda r: r[K_CHUNK - 1:K_CHUNK, :]

    def store(k0, val):
        if keys_axis == 1:
            sc_ref[:, pl.ds(k0, K_CHUNK)] = val
        else:
            sc_ref[pl.ds(k0, K_CHUNK), :] = val

    def sweep(fns, inits, reds, combines):
        n_stat = len(fns)

        def body(c, accs):
            x = load(pl.multiple_of(c * K_CHUNK, K_CHUNK))
            accs = [list(a) for a in accs]
            for n, piece in enumerate(pieces(x)):
                for s in range(n_stat):
                    accs[s][n % n_acc] = fns[s](accs[s][n % n_acc], piece)
            return tuple(tuple(a) for a in accs)

        init = tuple(tuple(jnp.full(part_shape, inits[s], F32) for _ in range(n_acc))
                     for s in range(n_stat))
        accs = lax.fori_loop(0, n_chunks, body, init)
        out = []
        for s in range(n_stat):
            acc = accs[s][0]
            for other in accs[s][1:]:
                acc = combines[s](acc, other)
            out.append(reds[s](acc, axis=keys_axis, keepdims=True))
        return out

    def count_ge(t):
        tb = jnp.broadcast_to(t, part_shape)
        return sweep([lambda a, x: jnp.where(x >= tb, a + 1.0, a)], [0.0], [jnp.sum],
                     [jnp.add])[0]

    def count_gt(t):
        tb = jnp.broadcast_to(t, part_shape)
        return sweep([lambda a, x: a + jnp.where(x > tb, 1.0, 0.0)], [0.0], [jnp.sum],
                     [jnp.add])[0]

    def max_below(t):
        tb = jnp.broadcast_to(t, part_shape)
        return sweep([lambda a, x: jnp.maximum(a, jnp.where(x < tb, x, neg_inf))], [neg_inf],
                     [jnp.max], [jnp.maximum])[0]

    row_max, row_min = sweep(
        [jnp.maximum, lambda a, x: jnp.minimum(a, jnp.where(x == neg_inf, jnp.inf, x))],
        [neg_inf, jnp.inf], [jnp.max, jnp.min], [jnp.maximum, jnp.minimum])

    k = float(topk)
    take_all = n_allowed <= k
    lo0 = row_min
    hi0 = row_max + jnp.maximum(jnp.abs(row_max) * 1e-6, 1e-30)
    t0 = jnp.full(q_shape, jnp.finfo(F32).min, F32)
    done0 = jnp.where(take_all, 1.0, 0.0)

    def bis_body(_, st):
        lo, hi, t, done = st
        mid = lo + (hi - lo) * 0.5
        c = count_ge(mid)
        hit = (c == k) & (done < 0.5)
        t = jnp.where(hit, mid, t)
        done = jnp.where(hit, 1.0, done)
        lo = jnp.where(c > k, mid, lo)
        hi = jnp.where(c < k, mid, hi)
        return lo, hi, t, done

    lo, hi, t, done = lax.fori_loop(0, max_iter, bis_body, (lo0, hi0, t0, done0))

    all_done = jnp.min(done) > 0.5

    def write(fn):
        def body(c, carry):
            k0 = pl.multiple_of(c * K_CHUNK, K_CHUNK)
            sel, carry = fn(load(k0), carry)
            store(k0, jnp.where(sel, 0.0, neg_inf))
            return carry
        return body

    def walk_down():
        def snap_cond(st):
            return jnp.min(st[2]) < 0.5

        def snap_body(st):
            hi_, t_, done_, over_ = st
            v = max_below(hi_)
            c = count_ge(v)
            ok = (c >= k) & (done_ < 0.5)
            t_ = jnp.where(ok, v, t_)
            over_ = jnp.where(ok, c - k, over_)
            hi_ = jnp.where((done_ < 0.5) & jnp.logical_not(ok), v, hi_)
            done_ = jnp.where(ok, 1.0, done_)
            return hi_, t_, done_, over_

        _, t_, _, over_ = lax.while_loop(snap_cond, snap_body,
                                         (hi, t, done, jnp.zeros(q_shape, F32)))
        return t_, over_

    tv, over = lax.cond(all_done, lambda: (t, jnp.zeros(q_shape, F32)), walk_down)
    has_ties = jnp.max(over) > 0.5

    @pl.when(jnp.logical_not(has_ties))
    def _():
        tb = jnp.broadcast_to(tv, chunk_shape)
        lax.fori_loop(0, n_chunks, write(lambda x, cr: (x >= tb, cr)), 0)

    @pl.when(has_ties)
    def _():
        quota = k - count_gt(tv)
        tb = jnp.broadcast_to(tv, chunk_shape)
        qb = jnp.broadcast_to(quota, chunk_shape)
        ui = lax.broadcasted_iota(jnp.int32, (K_CHUNK, K_CHUNK), 0)
        uj = lax.broadcasted_iota(jnp.int32, (K_CHUNK, K_CHUNK), 1)
        prefix = ((ui <= uj) if keys_axis == 1 else (ui >= uj)).astype(BF16)

        def tie_fn(x, seen):
            tie = x == tb
            if keys_axis == 1:
                rank = jnp.dot(tie.astype(BF16), prefix, preferred_element_type=F32) + seen
            else:
                rank = jnp.dot(prefix, tie.astype(BF16), preferred_element_type=F32) + seen
            sel = (x > tb) | (tie & (rank <= qb))
            return sel, last(rank)

        lax.fori_loop(0, n_chunks, write(tie_fn), jnp.zeros(q_shape, F32))


def _head_lane_mask(h, width):
    lane = lax.broadcasted_iota(jnp.int32, (1, width), 1)
    lo = (h % HEAD_GROUP) * ATT_HEAD_DIM
    return (lane >= lo) & (lane < lo + ATT_HEAD_DIM)


def _pattn_body(rb_ref, q_ref, iq_ref, w_ref, k_ref, vt_ref, ik_ref, o_ref,
                sc_ref, qp_ref, iqp_ref, bias_ref, m_ref, alpha_ref, acc_ref, x_ref, p_ref,
                *, topk):
    i = pl.program_id(1)
    n_chunks = i + 1

    @pl.when((pl.program_id(0) == 0) & (i == 0))
    def _():
        kk = lax.broadcasted_iota(jnp.int32, (K_CHUNK, Q_TILE), 0)
        qq = lax.broadcasted_iota(jnp.int32, (K_CHUNK, Q_TILE), 1)
        for delta in range(2):
            bucket = _rel_bucket(qq - kk + Q_TILE * delta)
            for h in range(ATT_HEADS):
                acc = jnp.zeros((K_CHUNK, Q_TILE), F32)
                for bk in range(REL_BUCKETS - 1):
                    acc = jnp.where(bucket == bk, rb_ref[bk, h] - rb_ref[REL_BUCKETS - 1, h], acc)
                bias_ref[delta, h] = acc * LOG2E
        bias_ref[2] = jnp.zeros(bias_ref.shape[1:], F32)

    q = q_ref[0]
    iq = iq_ref[0]
    for h in range(ATT_HEADS):
        g = h // HEAD_GROUP
        qp_ref[h] = jnp.where(_head_lane_mask(h, MXU_WIDTH),
                              q[:, MXU_WIDTH * g:MXU_WIDTH * (g + 1)], 0)
        iqp_ref[h] = jnp.concatenate(
            [iq[:, IDX_DIM * h:IDX_DIM * (h + 1)], jnp.zeros((Q_TILE, LANE - IDX_DIM), BF16)], axis=1)
    w_t = jnp.transpose(w_ref[0])[IDX_DIM:IDX_DIM + IDX_HEADS, :]

    q_pos = i * Q_TILE + lax.broadcasted_iota(jnp.int32, (1, Q_TILE), 1)

    def score_chunk(c, carry, *, diagonal):
        k0 = pl.multiple_of(c * K_CHUNK, K_CHUNK)
        ikc = ik_ref[0, pl.ds(k0, K_CHUNK), :]
        acc = jnp.zeros((K_CHUNK, Q_TILE), F32)
        for h in range(IDX_HEADS):
            s = lax.dot_general(ikc, iqp_ref[h], NT_DIMS, preferred_element_type=F32)
            acc = acc + jnp.maximum(s, 0.0) * w_t[h:h + 1, :]
        if diagonal:
            k_pos = k0 + lax.broadcasted_iota(jnp.int32, (K_CHUNK, 1), 0)
            acc = jnp.where(k_pos <= q_pos, acc, -jnp.inf)
        sc_ref[pl.ds(k0, K_CHUNK), :] = acc
        return carry

    def score_pair(a, carry):
        score_chunk(2 * a, carry, diagonal=False)
        return score_chunk(2 * a + 1, carry, diagonal=False)

    lax.fori_loop(0, i // 2, score_pair, 0)
    lax.fori_loop(2 * (i // 2), i, functools.partial(score_chunk, diagonal=False), 0)
    score_chunk(i, 0, diagonal=True)

    _select_topk(sc_ref, Q_TILE, n_chunks, (q_pos + 1).astype(F32), topk, keys_axis=0)

    m_ref[...] = jnp.full(m_ref.shape, -1e30, F32)
    alpha_ref[...] = jnp.ones(alpha_ref.shape, F32)
    acc_ref[...] = jnp.zeros(acc_ref.shape, F32)
    p_ref[0] = jnp.zeros(p_ref.shape[1:], BF16)
    x_ref[1] = jnp.full(x_ref.shape[1:], -jnp.inf, F32)
    masked_chunk = sc_ref.shape[0] - K_CHUNK
    sc_ref[pl.ds(masked_chunk, K_CHUNK), :] = jnp.full((K_CHUNK, Q_TILE), -jnp.inf, F32)
    ones_rows = jnp.ones((BF16_ROWS, K_CHUNK), BF16)
    row_block = 4 * SUBLANE

    def logits(h, slot, k0, mb, near):
        g = h // HEAD_GROUP
        kc = k_ref[0, pl.ds(k0, K_CHUNK), MXU_WIDTH * g:MXU_WIDTH * (g + 1)]
        s = lax.dot_general(kc, qp_ref[h], NT_DIMS, preferred_element_type=F32)
        mx = jnp.full((SUBLANE, Q_TILE), -jnp.inf, F32)
        for r in range(K_CHUNK // row_block):
            rs = slice(r * row_block, (r + 1) * row_block)
            xb = s[rs] + mb[rs]
            if near is not None:
                xb = xb + bias_ref[near, h, rs, :]
            x_ref[slot, h, rs, :] = xb
            for u in range(row_block // SUBLANE):
                mx = jnp.maximum(mx, xb[u * SUBLANE:(u + 1) * SUBLANE])
        m_old = m_ref[1 - slot, h]
        m_new = jnp.maximum(m_old, jnp.max(mx, axis=0, keepdims=True))
        alpha_ref[slot, h] = jnp.exp2(m_old - m_new)
        m_ref[slot, h] = m_new

    def probs(h, slot):
        p_ref[slot, h] = jnp.exp2(x_ref[slot, h] - m_ref[slot, h]).astype(BF16)

    def accumulate(h, slot, k0):
        vt = vt_ref[0, ATT_HEAD_DIM * h:ATT_HEAD_DIM * (h + 1), pl.ds(k0, K_CHUNK)]
        lhs = jnp.concatenate([vt, ones_rows], axis=0)
        acc_ref[h] = alpha_ref[slot, h] * acc_ref[h] + jnp.dot(
            lhs, p_ref[slot, h], preferred_element_type=F32)

    def one_step(t, slot, with_bias):
        k_new = pl.multiple_of(jnp.minimum(t, i) * K_CHUNK, K_CHUNK)
        k_old = pl.multiple_of(jnp.clip(t - 2, 0, i) * K_CHUNK, K_CHUNK)
        m0 = pl.multiple_of(jnp.where(t <= i, t * K_CHUNK, masked_chunk), K_CHUNK)
        mb = sc_ref[pl.ds(m0, K_CHUNK), :]
        near = jnp.clip(i - t, 0, 2) if with_bias else None
        for h in range(ATT_HEADS):
            accumulate(h, slot, k_old)
            probs(h, 1 - slot)
            logits(h, slot, k_new, mb, near)

    def two_steps(a, carry, *, with_bias):
        one_step(2 * a, 0, with_bias)
        one_step(2 * a + 1, 1, with_bias)
        return carry

    far_pairs = jnp.maximum(i - 1, 0) // 2
    all_pairs = (i + 4) // 2
    lax.fori_loop(0, far_pairs, functools.partial(two_steps, with_bias=False), 0)
    lax.fori_loop(far_pairs, all_pairs, functools.partial(two_steps, with_bias=True), 0)

    out_t = jnp.concatenate(
        [acc_ref[h, 0:ATT_HEAD_DIM, :] / acc_ref[h, ATT_HEAD_DIM:ATT_HEAD_DIM + 1, :]
         for h in range(ATT_HEADS)], axis=0)
    o_ref[0] = jnp.transpose(out_t).astype(BF16)


def _prompt_attention(rel_bias, q, iq, ikw, kb, vt, ikb, topk):
    bsz, s, _ = q.shape
    tile = lambda w: pl.BlockSpec((1, Q_TILE, w), lambda b, i: (b, i, 0))
    per_batch = lambda shape: pl.BlockSpec((1,) + shape, lambda b, i: (b, 0, 0),
                                           pipeline_mode=pl.Buffered(1))
    body = functools.partial(_pattn_body, topk=topk)
    return pl.pallas_call(
        body,
        grid=(bsz, s // Q_TILE),
        in_specs=[pl.BlockSpec(memory_space=pltpu.SMEM),
                  tile(ATT_WIDTH), tile(ATT_WIDTH), tile(LANE),
                  per_batch((s, ATT_WIDTH)), per_batch((ATT_WIDTH, s)), per_batch((s, LANE))],
        out_specs=tile(ATT_WIDTH),
        out_shape=jax.ShapeDtypeStruct((bsz, s, ATT_WIDTH), BF16),
        scratch_shapes=[
            pltpu.VMEM((s + K_CHUNK, Q_TILE), F32),
            pltpu.VMEM((ATT_HEADS, Q_TILE, MXU_WIDTH), BF16),
            pltpu.VMEM((IDX_HEADS, Q_TILE, LANE), BF16),
            pltpu.VMEM((3, ATT_HEADS, K_CHUNK, Q_TILE), F32),
            pltpu.VMEM((2, ATT_HEADS, 1, Q_TILE), F32),
            pltpu.VMEM((2, ATT_HEADS, 1, Q_TILE), F32),
            pltpu.VMEM((ATT_HEADS, ACC_ROWS, Q_TILE), F32),
            pltpu.VMEM((2, ATT_HEADS, K_CHUNK, Q_TILE), F32),
            pltpu.VMEM((2, ATT_HEADS, K_CHUNK, Q_TILE), BF16),
        ],
        compiler_params=pltpu.CompilerParams(dimension_semantics=("arbitrary", "arbitrary"),
                                             vmem_limit_bytes=VMEM_LIMIT),
        name="prompt_attention",
    )(rel_bias, q, iq, ikw, kb, vt, ikb)


SCORE_PAGES = 16
ATTN_PAGES = 16
NEW_PAD = 16


def _head_sums(y, dec_seq):
    return jnp.concatenate(
        [jnp.sum(y[IDX_HEADS * t:IDX_HEADS * (t + 1)], axis=0, keepdims=True)
         for t in range(dec_seq)], axis=0)


def _sscore_body(pt_ref, iq_ref, w_ref, iknew_ref, *rest, dec_seq):
    del pt_ref
    pages = rest[:SCORE_PAGES]
    out_ref, new_ref = rest[SCORE_PAGES:]
    iq = iq_ref[0]
    w = w_ref[0]
    for m in range(SCORE_PAGES):
        s = jnp.dot(iq, pages[m][0].astype(BF16), preferred_element_type=F32)
        out_ref[0, :, m * PAGE_SIZE:(m + 1) * PAGE_SIZE] = _head_sums(jnp.maximum(s, 0.0) * w,
                                                                      dec_seq)

    @pl.when(pl.program_id(1) == 0)
    def _():
        s = jnp.dot(iq, iknew_ref[0], preferred_element_type=F32)
        new_sc = _head_sums(jnp.maximum(s, 0.0) * w, dec_seq)
        padded = jnp.concatenate(
            [new_sc, jnp.zeros((dec_seq, K_CHUNK - PAGE_SIZE), F32)], axis=1)
        t_idx = lax.broadcasted_iota(jnp.int32, (dec_seq, K_CHUNK), 0)
        k_idx = lax.broadcasted_iota(jnp.int32, (dec_seq, K_CHUNK), 1)
        new_ref[0] = jnp.where(k_idx <= t_idx, padded, -jnp.inf)


def _sample_scores(page_table, iq_rows, w_rows, iknew_t, cache_idx_t, dec_seq):
    nseq, n_pages = page_table.shape
    steps = n_pages // SCORE_PAGES
    rows = dec_seq * IDX_HEADS

    def page_spec(m):
        return pl.BlockSpec((1, IDX_DIM, PAGE_SIZE),
                            lambda b, j, pt: (pt[b, j * SCORE_PAGES + m], 0, 0))

    per_seq = lambda shape: pl.BlockSpec((1,) + shape, lambda b, j, pt: (b, 0, 0))
    grid_spec = pltpu.PrefetchScalarGridSpec(
        num_scalar_prefetch=1,
        grid=(nseq, steps),
        in_specs=[per_seq((rows, IDX_DIM)), per_seq((rows, 1)), per_seq((IDX_DIM, PAGE_SIZE))]
                 + [page_spec(m) for m in range(SCORE_PAGES)],
        out_specs=(pl.BlockSpec((1, dec_seq, SCORE_PAGES * PAGE_SIZE),
                                lambda b, j, pt: (b, 0, j)),
                   per_seq((dec_seq, K_CHUNK))),
    )
    return pl.pallas_call(
        functools.partial(_sscore_body, dec_seq=dec_seq),
        grid_spec=grid_spec,
        out_shape=(jax.ShapeDtypeStruct((nseq, dec_seq, n_pages * PAGE_SIZE), F32),
                   jax.ShapeDtypeStruct((nseq, dec_seq, K_CHUNK), F32)),
        compiler_params=pltpu.CompilerParams(dimension_semantics=("arbitrary", "arbitrary"),
                                             vmem_limit_bytes=VMEM_LIMIT),
        name="sample_scores",
    )(page_table, iq_rows, w_rows, iknew_t, *([cache_idx_t] * SCORE_PAGES))


SAMPLE_TOPK_LANES = LANE


def _stopk_body(sc_in_ref, o_ref, sc_ref, *, dec_seq, past, topk):
    n_keys, lanes = sc_in_ref.shape
    sc_ref[...] = sc_in_ref[...]
    q = pl.program_id(0) * lanes + lax.broadcasted_iota(jnp.int32, (1, lanes), 1)
    n_allowed = (past + 1 + q % dec_seq).astype(F32)
    _select_topk(sc_ref, lanes, n_keys // K_CHUNK, n_allowed, topk, keys_axis=0)
    o_ref[...] = sc_ref[...]


def _sample_topk(scores_t, dec_seq, past, topk):
    n_keys, n_q = scores_t.shape
    lanes = min(SAMPLE_TOPK_LANES, n_q)
    spec = pl.BlockSpec((n_keys, lanes), lambda g: (0, g))
    return pl.pallas_call(
        functools.partial(_stopk_body, dec_seq=dec_seq, past=past, topk=topk),
        grid=(n_q // lanes,),
        in_specs=[spec],
        out_specs=spec,
        out_shape=jax.ShapeDtypeStruct((n_keys, n_q), F32),
        scratch_shapes=[pltpu.VMEM((n_keys, lanes), F32)],
        compiler_params=pltpu.CompilerParams(dimension_semantics=("arbitrary",),
                                             vmem_limit_bytes=VMEM_LIMIT),
        name="sample_topk",
    )(scores_t)


def _sattn_body(pt_ref, rb_ref, mask_ref, qp_ref, knew_ref, vnew_ref, *rest, dec_seq, past):
    del pt_ref
    kpages = rest[:ATTN_PAGES]
    vpages = rest[ATTN_PAGES:2 * ATTN_PAGES]
    o_ref = rest[2 * ATTN_PAGES]
    mb_ref, m_ref, l_ref, acc_ref = rest[2 * ATTN_PAGES + 1:]
    j = pl.program_id(1)
    n_steps = pl.num_programs(1)
    rows = dec_seq * ATT_HEADS
    row_t = lax.broadcasted_iota(jnp.int32, (rows, 1), 0) // ATT_HEADS
    row_h = lax.broadcasted_iota(jnp.int32, (rows, 1), 0) % ATT_HEADS

    @pl.when(j == 0)
    def _():
        for t in range(dec_seq):
            mb_ref[ATT_HEADS * t:ATT_HEADS * (t + 1), :] = jnp.broadcast_to(
                mask_ref[0, t:t + 1, :], (ATT_HEADS, past + K_CHUNK))
        m_ref[...] = jnp.full(m_ref.shape, -1e30, F32)
        l_ref[...] = jnp.zeros(l_ref.shape, F32)
        acc_ref[...] = jnp.zeros(acc_ref.shape, F32)

    qp = qp_ref[0]

    def near_bias(dist):
        bucket = _rel_bucket(dist)
        acc = jnp.zeros(dist.shape, F32)
        for bk in range(REL_BUCKETS - 1):
            per_row = jnp.zeros((rows, 1), F32)
            for h in range(ATT_HEADS):
                per_row = jnp.where(row_h == h, rb_ref[bk, h] - rb_ref[REL_BUCKETS - 1, h], per_row)
            acc = jnp.where(bucket == bk, per_row, acc)
        return acc * LOG2E

    def attend_pages(k_ts, v_ts, mb, last_bias):
        logit = jnp.concatenate(
            [jnp.dot(qp, k_t, preferred_element_type=F32) for k_t in k_ts], axis=1) + mb
        n = len(k_ts)
        if n > 1:
            last_bias = jnp.concatenate(
                [jnp.zeros((rows, (n - 1) * PAGE_SIZE), F32), last_bias], axis=1)
        logit = logit + last_bias
        m_old = m_ref[...]
        m_new = jnp.maximum(m_old, jnp.max(logit, axis=1, keepdims=True))
        alpha = jnp.exp2(m_old - m_new)
        p = jnp.exp2(logit - m_new).astype(BF16)
        l_ref[...] = alpha * l_ref[...] + jnp.sum(p.astype(F32), axis=1, keepdims=True)
        acc = alpha * acc_ref[...]
        for m, v_t in enumerate(v_ts):
            acc = acc + lax.dot_general(p[:, m * PAGE_SIZE:(m + 1) * PAGE_SIZE], v_t, NT_DIMS,
                                        preferred_element_type=F32)
        acc_ref[...] = acc
        m_ref[...] = m_new

    is_last = j == n_steps - 1
    off = lax.broadcasted_iota(jnp.int32, (rows, PAGE_SIZE), 1)
    k0 = pl.multiple_of(j * (ATTN_PAGES * PAGE_SIZE), ATTN_PAGES * PAGE_SIZE)
    last_bias = lax.cond(is_last, lambda: near_bias(PAGE_SIZE + row_t - off),
                         lambda: jnp.zeros((rows, PAGE_SIZE), F32))
    attend_pages([kp[0].astype(BF16) for kp in kpages], [vp[0].astype(BF16) for vp in vpages],
                 mb_ref[:, pl.ds(k0, ATTN_PAGES * PAGE_SIZE)], last_bias)

    @pl.when(is_last)
    def _():
        attend_pages([knew_ref[0]], [vnew_ref[0]], mb_ref[:, past:past + PAGE_SIZE],
                     near_bias(row_t - off))
        res = acc_ref[...] / l_ref[...]
        lane_h = lax.broadcasted_iota(jnp.int32, (rows, ATT_WIDTH), 1) // ATT_HEAD_DIM
        res = jnp.where(lane_h == row_h, res, 0.0)
        o_ref[0] = _head_sums(res, dec_seq).astype(BF16)


def _sample_attention(page_table, rel_bias, mask, qp_rows, knew_t, vnew_t, cache_k_t, cache_v_t,
                      dec_seq):
    nseq, n_pages = page_table.shape
    past = n_pages * PAGE_SIZE
    steps = n_pages // ATTN_PAGES
    rows = dec_seq * ATT_HEADS

    def page_spec(m):
        return pl.BlockSpec((1, ATT_WIDTH, PAGE_SIZE),
                            lambda b, j, pt: (pt[b, j * ATTN_PAGES + m], 0, 0))

    per_seq = lambda shape: pl.BlockSpec((1,) + shape, lambda b, j, pt: (b, 0, 0))
    grid_spec = pltpu.PrefetchScalarGridSpec(
        num_scalar_prefetch=1,
        grid=(nseq, steps),
        in_specs=[pl.BlockSpec(memory_space=pltpu.SMEM),
                  per_seq((dec_seq, past + K_CHUNK)), per_seq((rows, ATT_WIDTH)),
                  per_seq((ATT_WIDTH, PAGE_SIZE)), per_seq((ATT_WIDTH, PAGE_SIZE))]
                 + [page_spec(m) for m in range(ATTN_PAGES)] * 2,
        out_specs=per_seq((dec_seq, ATT_WIDTH)),
        scratch_shapes=[
            pltpu.VMEM((rows, past + K_CHUNK), F32),
            pltpu.VMEM((rows, 1), F32),
            pltpu.VMEM((rows, 1), F32),
            pltpu.VMEM((rows, ATT_WIDTH), F32),
        ],
    )
    return pl.pallas_call(
        functools.partial(_sattn_body, dec_seq=dec_seq, past=past),
        grid_spec=grid_spec,
        out_shape=jax.ShapeDtypeStruct((nseq, dec_seq, ATT_WIDTH), BF16),
        compiler_params=pltpu.CompilerParams(dimension_semantics=("arbitrary", "arbitrary"),
                                             vmem_limit_bytes=VMEM_LIMIT),
        name="sample_attention",
    )(page_table, rel_bias, mask, qp_rows, knew_t, vnew_t,
      *([cache_k_t] * ATTN_PAGES), *([cache_v_t] * ATTN_PAGES))


FF_CHUNK = 256


def _post_body(att_ref, gla_ref, x_ref, p_ref, wo_ref, gpm_ref, gpf_ref, wg_ref, wu_ref, wd_ref,
               gpo_ref, wpg_ref, wpp_ref, o_ref, *, d_ff):
    wo = wo_ref
    mixed = jnp.dot(att_ref[...], wo[0:ATT_WIDTH, :], preferred_element_type=F32)
    mixed = mixed + jnp.dot(gla_ref[...], wo[ATT_WIDTH:ATT_WIDTH + GLA_WIDTH, :],
                            preferred_element_type=F32)
    h = x_ref[...] + _rms(mixed, gpm_ref[...])
    f = _rms(h, gpf_ref[...]).astype(BF16)
    ff = jnp.zeros(h.shape, F32)
    for c in range(d_ff // FF_CHUNK):
        cs = slice(c * FF_CHUNK, (c + 1) * FF_CHUNK)
        gate = jnp.dot(f, wg_ref[:, cs], preferred_element_type=F32)
        up = jnp.dot(f, wu_ref[:, cs], preferred_element_type=F32)
        act = (gate * jax.nn.sigmoid(gate) * up).astype(BF16)
        ff = ff + jnp.dot(act, wd_ref[cs, :], preferred_element_type=F32)
    h = h + _rms(ff, gpo_ref[...])
    gate = jax.nn.sigmoid(jnp.dot(h.astype(BF16), wpg_ref[...], preferred_element_type=F32))
    ple = jnp.dot(p_ref[...].astype(BF16), wpp_ref[...], preferred_element_type=F32)
    o_ref[...] = h + gate * ple


def _post(att, gla, x, p, wo, gpm, gpf, wg, wu, wd, gpo, wpg, wpp, tm):
    t = x.shape[0]
    d_ff = wg.shape[1]
    row = lambda w: pl.BlockSpec((tm, w), lambda i: (i, 0))
    const = lambda a: pl.BlockSpec(a.shape, lambda i: (0, 0), pipeline_mode=pl.Buffered(1))
    return pl.pallas_call(
        functools.partial(_post_body, d_ff=d_ff),
        grid=(t // tm,),
        in_specs=[row(ATT_WIDTH), row(GLA_WIDTH), row(D_MODEL), row(p.shape[1]),
                  const(wo), const(gpm), const(gpf), const(wg), const(wu), const(wd),
                  const(gpo), const(wpg), const(wpp)],
        out_specs=row(D_MODEL),
        out_shape=jax.ShapeDtypeStruct((t, D_MODEL), F32),
        compiler_params=pltpu.CompilerParams(dimension_semantics=("arbitrary",),
                                             vmem_limit_bytes=VMEM_LIMIT),
        name="post",
    )(att, gla, x, p, wo, gpm, gpf, wg, wu, wd, gpo, wpg, wpp)


def _pack_w_in(w_in):
    offs = [0]
    for wdt in (ATT_WIDTH, ATT_WIDTH, ATT_WIDTH, IDX_HEADS * IDX_DIM, IDX_DIM, IDX_HEADS,
                GLA_KEY_WIDTH, GLA_KEY_WIDTH, GLA_WIDTH, GLA_GATE_RANK, GLA_WIDTH):
        offs.append(offs[-1] + wdt)
    aq, ak, av, iq, ik, iw, gq, gk, gv, glr, gout = [w_in[:, offs[n]:offs[n + 1]] for n in range(11)]
    d = w_in.shape[0]
    zeros = lambda n: jnp.zeros((d, n), w_in.dtype)
    packed = jnp.concatenate([
        aq * (ATT_HEAD_DIM ** -0.5 * LOG2E), ak, av, iq * (IDX_DIM ** -0.5),
        ik, iw, zeros(LANE - IDX_DIM - IDX_HEADS),
        gq * (GLA_DK ** -0.5), gk, gv, glr, zeros(LANE - GLA_GATE_RANK), gout], axis=1)
    return packed.astype(BF16)


def _state_to_t(s):
    n = s.shape[0]
    return jnp.transpose(s, (0, 3, 1, 2)).reshape(n, GLA_DV, GLA_KEY_WIDTH)


def _state_from_t(st):
    n = st.shape[0]
    return jnp.transpose(st.reshape(n, GLA_DV, GLA_HEADS, GLA_DK), (0, 2, 3, 1))


def _layer(x_prompt, x_sample, p_prompt, p_sample, cache_k, cache_v, cache_idx_k, state_gla,
           page_table, rel_bias, g_pre_mix, w_in, w_gate_up, b_gate, g_gla_out, w_out,
           g_post_mix, g_pre_ffn, w_ff_gate, w_ff_up, w_ff_down, g_post_ffn, w_ple_gate,
           w_ple_proj):
    bsz, seq, d = x_prompt.shape
    nseq, dec_seq, _ = x_sample.shape
    n_pool = cache_k.shape[0]
    past = page_table.shape[1] * PAGE_SIZE
    row2 = lambda g: g.reshape(1, -1)

    w_packed = _pack_w_in(w_in)
    w_t = jnp.transpose(w_packed[:, SEG_K[0]:SEG_IKW[0] + IDX_DIM])
    w_t = jnp.concatenate([w_t[0:2 * ATT_WIDTH], w_t[SEG_IKW[0] - SEG_K[0]:]], axis=0)
    wgu = jnp.concatenate(
        [w_gate_up, jnp.zeros((LANE - GLA_GATE_RANK, GLA_KEY_WIDTH), w_gate_up.dtype)],
        axis=0).astype(BF16)
    post_w = (w_out.astype(BF16), row2(g_post_mix), row2(g_pre_ffn), w_ff_gate.astype(BF16),
              w_ff_up.astype(BF16), w_ff_down.astype(BF16), row2(g_post_ffn),
              w_ple_gate.astype(BF16), w_ple_proj.astype(BF16))
    gla_w = (wgu, row2(b_gate), row2(g_gla_out))

    tp = bsz * seq
    q, kb, iq, ikw, ikb, gin, k_t, v_t, vtb, ik_t = _inproj(
        x_prompt.reshape(tp, d), row2(g_pre_mix), w_packed, w_t, min(512, seq), seq)
    r3 = lambda a: a.reshape(bsz, seq, a.shape[-1])
    att = _prompt_attention(rel_bias, r3(q), r3(iq), r3(ikw), r3(kb), vtb, r3(ikb),
                            min(TOPK_MAX, seq // 4))
    gla_chunk = min(128, seq)
    gla_o, st_p = _gla(r3(gin), jnp.zeros((bsz, GLA_DV, GLA_KEY_WIDTH), F32), *gla_w,
                       tile=min(512, seq), chunk=gla_chunk, n_valid=gla_chunk)
    y_p = _post(att.reshape(tp, ATT_WIDTH), gla_o.reshape(tp, GLA_WIDTH), x_prompt.reshape(tp, d),
                p_prompt.reshape(tp, -1), *post_w, min(512, tp))
    heads_last = lambda a: jnp.transpose(
        a.reshape(bsz, ATT_HEADS, ATT_HEAD_DIM, seq), (0, 3, 1, 2))
    outs_p = (y_p.reshape(bsz, seq, d), heads_last(k_t), heads_last(v_t),
              jnp.swapaxes(ik_t, 1, 2), _state_from_t(st_p))

    ts = nseq * dec_seq
    q, kb, iq, ikw, ikb, gin, kf, vf, vb = _inproj(
        x_sample.reshape(ts, d), row2(g_pre_mix), w_packed, None, min(512, ts), ts)
    rows = dec_seq * ATT_HEADS
    iq_rows = iq.reshape(nseq, rows, IDX_DIM)
    w_rows = ikw[:, IDX_DIM:IDX_DIM + IDX_HEADS].reshape(nseq, rows, 1)
    head_of_col = jnp.arange(ATT_WIDTH, dtype=jnp.int32) // ATT_HEAD_DIM
    head_mask = head_of_col[None, :] == jnp.arange(ATT_HEADS, dtype=jnp.int32)[:, None]
    qp_rows = jnp.where(head_mask[None, None], q.reshape(nseq, dec_seq, 1, ATT_WIDTH),
                        jnp.zeros((), BF16)).reshape(nseq, rows, ATT_WIDTH)
    per_seq = lambda a: a.reshape(nseq, dec_seq, a.shape[-1])
    new_page_t = lambda a: jnp.pad(jnp.swapaxes(per_seq(a), 1, 2),
                                   ((0, 0), (0, 0), (0, PAGE_SIZE - dec_seq)))
    cache_idx_t = jnp.swapaxes(cache_idx_k, 1, 2)
    cache_k_t = jnp.transpose(cache_k, (0, 2, 3, 1)).reshape(n_pool, ATT_WIDTH, PAGE_SIZE)
    cache_v_t = jnp.transpose(cache_v, (0, 2, 3, 1)).reshape(n_pool, ATT_WIDTH, PAGE_SIZE)
    sc_past, sc_new = _sample_scores(page_table, iq_rows, w_rows, new_page_t(ikb[:, :IDX_DIM]),
                                     cache_idx_t, dec_seq)
    scores_t = jnp.transpose(jnp.concatenate([sc_past, sc_new], axis=2).reshape(ts, -1))
    mask_t = _sample_topk(scores_t, dec_seq, past, min(TOPK_MAX, (past + dec_seq) // 4))
    mask = jnp.transpose(mask_t).reshape(nseq, dec_seq, -1)
    att_s = _sample_attention(page_table, rel_bias, mask, qp_rows, new_page_t(kb),
                              new_page_t(vb), cache_k_t, cache_v_t, dec_seq)
    gin_s = jnp.pad(per_seq(gin), ((0, 0), (0, NEW_PAD - dec_seq), (0, 0)))
    gla_s, st_s = _gla(gin_s, _state_to_t(state_gla), *gla_w,
                       tile=NEW_PAD, chunk=NEW_PAD, n_valid=dec_seq)
    y_s = _post(att_s.reshape(ts, ATT_WIDTH), gla_s[:, :dec_seq].reshape(ts, GLA_WIDTH),
                x_sample.reshape(ts, d), p_sample.reshape(ts, -1), *post_w, min(256, ts))
    outs_s = (y_s.reshape(nseq, dec_seq, d),
              kf.reshape(nseq, dec_seq, ATT_HEADS, ATT_HEAD_DIM),
              vf.reshape(nseq, dec_seq, ATT_HEADS, ATT_HEAD_DIM),
              ikw[:, :IDX_DIM].reshape(nseq, dec_seq, IDX_DIM),
              _state_from_t(st_s))
    return outs_p, outs_s


def kernel(x_prompt, x_sample, p_prompt, p_sample, cache_k, cache_v, cache_idx_k, state_gla,
           page_table, rel_bias, g_pre_mix, w_in, w_gate_up, b_gate, g_gla_out, w_out,
           g_post_mix, g_pre_ffn, w_ff_gate, w_ff_up, w_ff_down, g_post_ffn, w_ple_gate,
           w_ple_proj):
    depth = w_in.shape[0]
    h_p, h_s = x_prompt, x_sample
    per_layer = []
    for i in range(depth):
        outs_p, outs_s = _layer(
            h_p, h_s, p_prompt[i], p_sample[i], cache_k[i], cache_v[i], cache_idx_k[i],
            state_gla[i], page_table, rel_bias, g_pre_mix[i], w_in[i], w_gate_up[i], b_gate[i],
            g_gla_out[i], w_out[i], g_post_mix[i], g_pre_ffn[i], w_ff_gate[i], w_ff_up[i],
            w_ff_down[i], g_post_ffn[i], w_ple_gate[i], w_ple_proj[i])
        h_p, h_s = outs_p[0], outs_s[0]
        per_layer.append(outs_p[1:] + outs_s[1:])
    stacked = [jnp.stack([lay[n] for lay in per_layer]) for n in range(8)]
    return (h_p, h_s, stacked[0], stacked[1], stacked[2], stacked[3],
            stacked[4], stacked[5], stacked[6], stacked[7])
```

```python
import functools
import math

import jax
import jax.numpy as jnp
from jax import lax
from jax.experimental import pallas as pl
from jax.experimental.pallas import tpu as pltpu

F32 = jnp.float32
BF16 = jnp.bfloat16

D_MODEL = 1024
ATT_HEADS = 8
ATT_HEAD_DIM = 64
ATT_WIDTH = ATT_HEADS * ATT_HEAD_DIM
IDX_HEADS = 8
IDX_DIM = 64
TOPK_MAX = 256
PAGE_SIZE = 128
GLA_HEADS = 4
GLA_DK = 64
GLA_DV = 128
GLA_KEY_WIDTH = GLA_HEADS * GLA_DK
GLA_WIDTH = GLA_HEADS * GLA_DV
GLA_GATE_RANK = 16
GLA_GATE_TAU = 16.0
REL_BUCKETS = 32
REL_MAX_DIST = 128
RMS_EPS = 1e-6
LOG2E = math.log2(math.e)

LANE = 128
SUBLANE = 8
BF16_ROWS = 16
MXU_WIDTH = 256
Q_TILE = MXU_WIDTH
K_CHUNK = MXU_WIDTH
HEAD_GROUP = MXU_WIDTH // ATT_HEAD_DIM
ACC_ROWS = ATT_HEAD_DIM + BF16_ROWS
VMEM_LIMIT = 56 * 1024 * 1024
COARSE_ITERS = 10
FINE_ITERS = 9

SEG_Q = (0, 512)
SEG_K = (512, 1024)
SEG_V = (1024, 1536)
SEG_IQ = (1536, 2048)
SEG_IKW = (2048, 2176)
SEG_GLA = (2176, 3840)
PROJ_PACKED = 3840
GLA_IN_WIDTH = SEG_GLA[1] - SEG_GLA[0]

NT_DIMS = (((1,), (1,)), ((), ()))
TN_DIMS = (((0,), (0,)), ((), ()))


def _rms(x, g):
    return x * lax.rsqrt(jnp.mean(x * x, axis=-1, keepdims=True) + RMS_EPS) * g


def _inproj_body(x_ref, g_ref, w_ref, *rest, tokens_minor):
    a = _rms(x_ref[...], g_ref[...]).astype(BF16)

    def proj(seg):
        return jnp.dot(a, w_ref[:, seg[0]:seg[1]], preferred_element_type=F32)

    if tokens_minor:
        wt_ref, q_ref, kb_ref, iq_ref, ikw_ref, ikb_ref, gla_ref, kt_ref, vt_ref, vtb_ref, ikt_ref = rest
    else:
        q_ref, kb_ref, iq_ref, ikw_ref, ikb_ref, gla_ref, k_ref, v_ref, vb_ref = rest

    q_ref[...] = proj(SEG_Q).astype(BF16)
    kf = proj(SEG_K)
    kb_ref[...] = kf.astype(BF16)
    iq_ref[...] = proj(SEG_IQ).astype(BF16)
    lane = lax.broadcasted_iota(jnp.int32, (1, LANE), 1)
    is_w = (lane >= IDX_DIM) & (lane < IDX_DIM + IDX_HEADS)
    ikw = proj(SEG_IKW) * jnp.where(is_w, IDX_HEADS ** -0.5, 1.0)
    ikw_ref[...] = ikw
    ikb_ref[...] = ikw.astype(BF16)
    gla_ref[...] = proj(SEG_GLA)
    if tokens_minor:
        def proj_t(lo, hi):
            return lax.dot_general(wt_ref[lo:hi, :], a, NT_DIMS, preferred_element_type=F32)

        kt_ref[0] = proj_t(0, ATT_WIDTH)
        vt = proj_t(ATT_WIDTH, 2 * ATT_WIDTH)
        vt_ref[0] = vt
        vtb_ref[0] = vt.astype(BF16)
        ikt_ref[0] = proj_t(2 * ATT_WIDTH, 2 * ATT_WIDTH + IDX_DIM)
    else:
        k_ref[...] = kf
        vf = proj(SEG_V)
        v_ref[...] = vf
        vb_ref[...] = vf.astype(BF16)


def _inproj(x, g, w_packed, w_t, tm, seq):
    t = x.shape[0]
    tokens_minor = w_t is not None
    tiles_per_seq = seq // tm
    row = lambda w: pl.BlockSpec((tm, w), lambda i: (i, 0))
    const = lambda s: pl.BlockSpec(s, lambda i: (0, 0))
    col = lambda r: pl.BlockSpec((1, r, tm), lambda i: (i // tiles_per_seq, 0, i % tiles_per_seq))
    rows_shape = lambda w, dt: jax.ShapeDtypeStruct((t, w), dt)
    cols_shape = lambda r, dt: jax.ShapeDtypeStruct((t // seq, r, seq), dt)
    out_shape = [rows_shape(ATT_WIDTH, BF16), rows_shape(ATT_WIDTH, BF16),
                 rows_shape(ATT_WIDTH, BF16), rows_shape(LANE, F32), rows_shape(LANE, BF16),
                 rows_shape(GLA_IN_WIDTH, F32)]
    out_specs = [row(ATT_WIDTH), row(ATT_WIDTH), row(ATT_WIDTH), row(LANE), row(LANE),
                 row(GLA_IN_WIDTH)]
    in_specs = [row(D_MODEL), const((1, D_MODEL)), const((D_MODEL, PROJ_PACKED))]
    operands = [x, g, w_packed]
    if tokens_minor:
        in_specs.append(const(w_t.shape))
        operands.append(w_t)
        out_shape += [cols_shape(ATT_WIDTH, F32), cols_shape(ATT_WIDTH, F32),
                      cols_shape(ATT_WIDTH, BF16), cols_shape(IDX_DIM, F32)]
        out_specs += [col(ATT_WIDTH), col(ATT_WIDTH), col(ATT_WIDTH), col(IDX_DIM)]
    else:
        out_shape += [rows_shape(ATT_WIDTH, F32), rows_shape(ATT_WIDTH, F32),
                      rows_shape(ATT_WIDTH, BF16)]
        out_specs += [row(ATT_WIDTH), row(ATT_WIDTH), row(ATT_WIDTH)]
    return pl.pallas_call(
        functools.partial(_inproj_body, tokens_minor=tokens_minor),
        grid=(t // tm,),
        in_specs=in_specs,
        out_specs=tuple(out_specs),
        out_shape=tuple(out_shape),
        compiler_params=pltpu.CompilerParams(dimension_semantics=("arbitrary",),
                                             vmem_limit_bytes=VMEM_LIMIT),
        name="inproj",
    )(*operands)


def _log_sigmoid(z):
    return jnp.minimum(z, 0.0) - jnp.log1p(jnp.exp(-jnp.abs(z)))


def _gla_body(gin_ref, s0_ref, wgu_ref, bg_ref, gg_ref, o_ref, sout_ref, st_ref, *, chunk, n_valid):
    @pl.when(pl.program_id(1) == 0)
    def _():
        st_ref[...] = s0_ref[0]

    n_chunks = gin_ref.shape[1] // chunk
    ri = lax.broadcasted_iota(jnp.int32, (chunk, chunk), 0)
    ci = lax.broadcasted_iota(jnp.int32, (chunk, chunk), 1)
    causal = ri >= ci
    tri = causal.astype(F32)
    row_valid = lax.broadcasted_iota(jnp.int32, (chunk, 1), 0) < n_valid

    def one_chunk(c, carry):
        r0 = pl.multiple_of(c * chunk, chunk)
        blk = gin_ref[0, pl.ds(r0, chunk), :]
        gq = blk[:, 0:256]
        gk = blk[:, 256:512]
        gv = blk[:, 512:1024]
        glr = blk[:, 1024:1152]
        gout = blk[:, 1152:1664]
        z = jnp.dot(glr.astype(BF16), wgu_ref[...], preferred_element_type=F32) + bg_ref[...]
        la = _log_sigmoid(z) / GLA_GATE_TAU
        if n_valid < chunk:
            la = jnp.where(row_valid, la, 0.0)
        b = jnp.dot(tri, la, precision=lax.Precision.HIGHEST, preferred_element_type=F32)
        b_last = b[chunk - 1:chunk, :]
        qt = gq * jnp.exp(b)
        kt = gk * jnp.exp(-b)
        kd = gk * jnp.exp(b_last - b)
        dec = jnp.exp(b_last)
        st = st_ref[...]
        o_parts = []
        st_parts = []
        for h in range(GLA_HEADS):
            ks = slice(GLA_DK * h, GLA_DK * (h + 1))
            vs = slice(GLA_DV * h, GLA_DV * (h + 1))
            qh = qt[:, ks].astype(BF16)
            kh = kt[:, ks].astype(BF16)
            kdh = kd[:, ks].astype(BF16)
            vh = gv[:, vs].astype(BF16)
            sth = st[:, ks]
            attn = lax.dot_general(qh, kh, NT_DIMS, preferred_element_type=F32)
            attn = jnp.where(causal, attn, 0.0)
            o = jnp.dot(attn.astype(BF16), vh, preferred_element_type=F32)
            o = o + lax.dot_general(qh, sth.astype(BF16), NT_DIMS, preferred_element_type=F32)
            upd = lax.dot_general(vh, kdh, TN_DIMS, preferred_element_type=F32)
            st_parts.append(dec[:, ks] * sth + upd)
            o_parts.append(_rms(o, gg_ref[...]))
        st_ref[...] = jnp.concatenate(st_parts, axis=1)
        o_all = jnp.concatenate(o_parts, axis=1)
        gate = gout * jax.nn.sigmoid(gout)
        o_ref[0, pl.ds(r0, chunk), :] = (o_all * gate).astype(BF16)
        return carry

    lax.fori_loop(0, n_chunks, one_chunk, 0)
    sout_ref[0] = st_ref[...]


def _gla(gin, s0_t, wgu, bg, gg, *, tile, chunk, n_valid):
    nseq, t, _ = gin.shape
    body = functools.partial(_gla_body, chunk=chunk, n_valid=n_valid)
    return pl.pallas_call(
        body,
        grid=(nseq, t // tile),
        in_specs=[
            pl.BlockSpec((1, tile, GLA_IN_WIDTH), lambda b, j: (b, j, 0)),
            pl.BlockSpec((1, GLA_DV, GLA_KEY_WIDTH), lambda b, j: (b, 0, 0)),
            pl.BlockSpec((LANE, GLA_KEY_WIDTH), lambda b, j: (0, 0)),
            pl.BlockSpec((1, GLA_KEY_WIDTH), lambda b, j: (0, 0)),
            pl.BlockSpec((1, GLA_DV), lambda b, j: (0, 0)),
        ],
        out_specs=(
            pl.BlockSpec((1, tile, GLA_WIDTH), lambda b, j: (b, j, 0)),
            pl.BlockSpec((1, GLA_DV, GLA_KEY_WIDTH), lambda b, j: (b, 0, 0)),
        ),
        out_shape=(
            jax.ShapeDtypeStruct((nseq, t, GLA_WIDTH), BF16),
            jax.ShapeDtypeStruct((nseq, GLA_DV, GLA_KEY_WIDTH), F32),
        ),
        scratch_shapes=[pltpu.VMEM((GLA_DV, GLA_KEY_WIDTH), F32)],
        compiler_params=pltpu.CompilerParams(dimension_semantics=("arbitrary", "arbitrary"),
                                             vmem_limit_bytes=VMEM_LIMIT),
        name="gla",
    )(gin, s0_t, wgu, bg, gg)


def _rel_bucket(dist):
    n = jnp.maximum(dist, 0)
    exact = REL_BUCKETS // 2
    nf = jnp.maximum(n, 1).astype(F32)
    large = exact + (jnp.log(nf / exact) / math.log(REL_MAX_DIST / exact)
                     * (REL_BUCKETS - exact)).astype(jnp.int32)
    large = jnp.minimum(large, REL_BUCKETS - 1)
    return jnp.where(n < exact, n, large)


def _select_topk(sc_ref, n_queries, n_chunks, n_allowed, topk, keys_axis, max_iter=18,
                 coarse_ref=None, coarse_iter=0):
    neg_inf = -jnp.inf
    n_acc = 4 if keys_axis == 0 else K_CHUNK // LANE
    if keys_axis == 1:
        part_shape = (n_queries, LANE)
        q_shape = (n_queries, 1)
        chunk_shape = (n_queries, K_CHUNK)
        load = lambda k0: sc_ref[:, pl.ds(k0, K_CHUNK)]
        pieces = lambda x: [x[:, j * LANE:(j + 1) * LANE] for j in range(K_CHUNK // LANE)]
        last = lambda r: r[:, K_CHUNK - 1:K_CHUNK]
    else:
        part_shape = (SUBLANE, n_queries)
        q_shape = (1, n_queries)
        chunk_shape = (K_CHUNK, n_queries)
        load = lambda k0: sc_ref[pl.ds(k0, K_CHUNK), :]
        pieces = lambda x: [x[j * SUBLANE:(j + 1) * SUBLANE] for j in range(K_CHUNK // SUBLANE)]
        last = lambda r: r[K_CHUNK - 1:K_CHUNK, :]

    def store(k0, val):
        if keys_axis == 1:
            sc_ref[:, pl.ds(k0, K_CHUNK)] = val
        else:
            sc_ref[pl.ds(k0, K_CHUNK), :] = val

    def sweep(fns, inits, reds, combines):
        n_stat = len(fns)

        def body(c, accs):
            x = load(pl.multiple_of(c * K_CHUNK, K_CHUNK))
            accs = [list(a) for a in accs]
            for n, piece in enumerate(pieces(x)):
                for s in range(n_stat):
                    accs[s][n % n_acc] = fns[s](accs[s][n % n_acc], piece)
            return tuple(tuple(a) for a in accs)

        init = tuple(tuple(jnp.full(part_shape, inits[s], F32) for _ in range(n_acc))
                     for s in range(n_stat))
        accs = lax.fori_loop(0, n_chunks, body, init)
        out = []
        for s in range(n_stat):
            acc = accs[s][0]
            for other in accs[s][1:]:
                acc = combines[s](acc, other)
            out.append(reds[s](acc, axis=keys_axis, keepdims=True))
        return out

    def count_ge(t):
        tb = jnp.broadcast_to(t, part_shape)
        return sweep([lambda a, x: jnp.where(x >= tb, a + 1.0, a)], [0.0], [jnp.sum],
                     [jnp.add])[0]

    def count_gt(t):
        tb = jnp.broadcast_to(t, part_shape)
        return sweep([lambda a, x: a + jnp.where(x > tb, 1.0, 0.0)], [0.0], [jnp.sum],
                     [jnp.add])[0]

    def max_below(t):
        tb = jnp.broadcast_to(t, part_shape)
        return sweep([lambda a, x: jnp.maximum(a, jnp.where(x < tb, x, neg_inf))], [neg_inf],
                     [jnp.max], [jnp.maximum])[0]

    row_max, row_min = sweep(
        [jnp.maximum, lambda a, x: jnp.minimum(a, jnp.where(x == neg_inf, jnp.inf, x))],
        [neg_inf, jnp.inf], [jnp.max, jnp.min], [jnp.maximum, jnp.minimum])

    k = float(topk)
    take_all = n_allowed <= k
    lo0 = row_min
    hi0 = row_max + jnp.maximum(jnp.abs(row_max) * 1e-6, 1e-30)
    t0 = jnp.full(q_shape, jnp.finfo(F32).min, F32)
    done0 = jnp.where(take_all, 1.0, 0.0)

    if coarse_ref is not None:
        rows_b = BF16_ROWS
        one_b = jnp.ones((rows_b, n_queries), BF16)
        zero_b = jnp.zeros((rows_b, n_queries), BF16)

        def count_b(m):
            mb_ = jnp.broadcast_to(m.astype(BF16), (rows_b, n_queries))

            def body(c, accs):
                x = coarse_ref[pl.ds(pl.multiple_of(c * K_CHUNK, K_CHUNK), K_CHUNK), :]
                accs = list(accs)
                for n in range(K_CHUNK // rows_b):
                    piece = x[n * rows_b:(n + 1) * rows_b]
                    accs[n % n_acc] = accs[n % n_acc] + jnp.where(piece >= mb_, one_b, zero_b)
                return tuple(accs)

            accs = lax.fori_loop(0, n_chunks, body, tuple(zero_b for _ in range(n_acc)))
            total = accs[0].astype(F32)
            for other in accs[1:]:
                total = total + other.astype(F32)
            return jnp.sum(total, axis=0, keepdims=True)

        def coarse_body(_, st):
            lo, hi = st
            m = (lo + (hi - lo) * 0.5).astype(BF16).astype(F32)
            inside = (m > lo) & (m < hi)
            c = count_b(m)
            below = m - (jnp.abs(m) * (2.0 ** -7) + 1e-30)
            hi = jnp.where(inside & (c < k), m, hi)
            lo = jnp.where(inside & (c >= k) & (below > lo), below, lo)
            return lo, hi

        lo0, hi0 = lax.fori_loop(0, coarse_iter, coarse_body, (lo0, hi0))

    def bis_body(_, st):
        lo, hi, t, done = st
        mid = lo + (hi - lo) * 0.5
        c = count_ge(mid)
        hit = (c == k) & (done < 0.5)
        t = jnp.where(hit, mid, t)
        done = jnp.where(hit, 1.0, done)
        lo = jnp.where(c > k, mid, lo)
        hi = jnp.where(c < k, mid, hi)
        return lo, hi, t, done

    lo, hi, t, done = lax.fori_loop(0, max_iter, bis_body, (lo0, hi0, t0, done0))

    all_done = jnp.min(done) > 0.5

    def write(fn):
        def body(c, carry):
            k0 = pl.multiple_of(c * K_CHUNK, K_CHUNK)
            sel, carry = fn(load(k0), carry)
            store(k0, jnp.where(sel, 0.0, neg_inf))
            return carry
        return body

    def walk_down():
        def snap_cond(st):
            return jnp.min(st[2]) < 0.5

        def snap_body(st):
            hi_, t_, done_, over_ = st
            v = max_below(hi_)
            c = count_ge(v)
            ok = (c >= k) & (done_ < 0.5)
            t_ = jnp.where(ok, v, t_)
            over_ = jnp.where(ok, c - k, over_)
            hi_ = jnp.where((done_ < 0.5) & jnp.logical_not(ok), v, hi_)
            done_ = jnp.where(ok, 1.0, done_)
            return hi_, t_, done_, over_

        _, t_, _, over_ = lax.while_loop(snap_cond, snap_body,
                                         (hi, t, done, jnp.zeros(q_shape, F32)))
        return t_, over_

    tv, over = lax.cond(all_done, lambda: (t, jnp.zeros(q_shape, F32)), walk_down)
    has_ties = jnp.max(over) > 0.5

    @pl.when(jnp.logical_not(has_ties))
    def _():
        tb = jnp.broadcast_to(tv, chunk_shape)
        lax.fori_loop(0, n_chunks, write(lambda x, cr: (x >= tb, cr)), 0)

    @pl.when(has_ties)
    def _():
        quota = k - count_gt(tv)
        tb = jnp.broadcast_to(tv, chunk_shape)
        qb = jnp.broadcast_to(quota, chunk_shape)
        ui = lax.broadcasted_iota(jnp.int32, (K_CHUNK, K_CHUNK), 0)
        uj = lax.broadcasted_iota(jnp.int32, (K_CHUNK, K_CHUNK), 1)
        prefix = ((ui <= uj) if keys_axis == 1 else (ui >= uj)).astype(BF16)

        def tie_fn(x, seen):
            tie = x == tb
            if keys_axis == 1:
                rank = jnp.dot(tie.astype(BF16), prefix, preferred_element_type=F32) + seen
            else:
                rank = jnp.dot(prefix, tie.astype(BF16), preferred_element_type=F32) + seen
            sel = (x > tb) | (tie & (rank <= qb))
            return sel, last(rank)

        lax.fori_loop(0, n_chunks, write(tie_fn), jnp.zeros(q_shape, F32))


def _head_lane_mask(h, width):
    lane = lax.broadcasted_iota(jnp.int32, (1, width), 1)
    lo = (h % HEAD_GROUP) * ATT_HEAD_DIM
    return (lane >= lo) & (lane < lo + ATT_HEAD_DIM)


def _pattn_body(rb_ref, q_ref, iq_ref, w_ref, k_ref, vt_ref, ik_ref, o_ref,
                sc_ref, scb_ref, qp_ref, iqp_ref, bias_ref, m_ref, alpha_ref, acc_ref, x_ref,
                p_ref, *, topk):
    i = pl.program_id(1)
    n_chunks = i + 1

    @pl.when((pl.program_id(0) == 0) & (i == 0))
    def _():
        kk = lax.broadcasted_iota(jnp.int32, (K_CHUNK, Q_TILE), 0)
        qq = lax.broadcasted_iota(jnp.int32, (K_CHUNK, Q_TILE), 1)
        for delta in range(2):
            bucket = _rel_bucket(qq - kk + Q_TILE * delta)
            for h in range(ATT_HEADS):
                acc = jnp.zeros((K_CHUNK, Q_TILE), F32)
                for bk in range(REL_BUCKETS - 1):
                    acc = jnp.where(bucket == bk, rb_ref[bk, h] - rb_ref[REL_BUCKETS - 1, h], acc)
                bias_ref[delta, h] = acc * LOG2E
        bias_ref[2] = jnp.zeros(bias_ref.shape[1:], F32)

    q = q_ref[0]
    iq = iq_ref[0]
    for h in range(ATT_HEADS):
        g = h // HEAD_GROUP
        qp_ref[h] = jnp.where(_head_lane_mask(h, MXU_WIDTH),
                              q[:, MXU_WIDTH * g:MXU_WIDTH * (g + 1)], 0)
        iqp_ref[h] = jnp.concatenate(
            [iq[:, IDX_DIM * h:IDX_DIM * (h + 1)], jnp.zeros((Q_TILE, LANE - IDX_DIM), BF16)], axis=1)
    w_t = jnp.transpose(w_ref[0])[IDX_DIM:IDX_DIM + IDX_HEADS, :]

    q_pos = i * Q_TILE + lax.broadcasted_iota(jnp.int32, (1, Q_TILE), 1)

    def score_chunk(c, carry, *, diagonal):
        k0 = pl.multiple_of(c * K_CHUNK, K_CHUNK)
        ikc = ik_ref[0, pl.ds(k0, K_CHUNK), :]
        acc = jnp.zeros((K_CHUNK, Q_TILE), F32)
        for h in range(IDX_HEADS):
            s = lax.dot_general(ikc, iqp_ref[h], NT_DIMS, preferred_element_type=F32)
            acc = acc + jnp.maximum(s, 0.0) * w_t[h:h + 1, :]
        if diagonal:
            k_pos = k0 + lax.broadcasted_iota(jnp.int32, (K_CHUNK, 1), 0)
            acc = jnp.where(k_pos <= q_pos, acc, -jnp.inf)
        sc_ref[pl.ds(k0, K_CHUNK), :] = acc
        scb_ref[pl.ds(k0, K_CHUNK), :] = acc.astype(BF16)
        return carry

    def score_pair(a, carry):
        score_chunk(2 * a, carry, diagonal=False)
        return score_chunk(2 * a + 1, carry, diagonal=False)

    lax.fori_loop(0, i // 2, score_pair, 0)
    lax.fori_loop(2 * (i // 2), i, functools.partial(score_chunk, diagonal=False), 0)
    score_chunk(i, 0, diagonal=True)

    _select_topk(sc_ref, Q_TILE, n_chunks, (q_pos + 1).astype(F32), topk, keys_axis=0,
                 max_iter=FINE_ITERS, coarse_ref=scb_ref, coarse_iter=COARSE_ITERS)

    m_ref[...] = jnp.full(m_ref.shape, -1e30, F32)
    alpha_ref[...] = jnp.ones(alpha_ref.shape, F32)
    acc_ref[...] = jnp.zeros(acc_ref.shape, F32)
    p_ref[0] = jnp.zeros(p_ref.shape[1:], BF16)
    x_ref[1] = jnp.full(x_ref.shape[1:], -jnp.inf, F32)
    masked_chunk = sc_ref.shape[0] - K_CHUNK
    sc_ref[pl.ds(masked_chunk, K_CHUNK), :] = jnp.full((K_CHUNK, Q_TILE), -jnp.inf, F32)
    ones_rows = jnp.ones((BF16_ROWS, K_CHUNK), BF16)
    row_block = 4 * SUBLANE

    def logits(h, slot, k0, mb, near):
        g = h // HEAD_GROUP
        kc = k_ref[0, pl.ds(k0, K_CHUNK), MXU_WIDTH * g:MXU_WIDTH * (g + 1)]
        s = lax.dot_general(kc, qp_ref[h], NT_DIMS, preferred_element_type=F32)
        mx = jnp.full((SUBLANE, Q_TILE), -jnp.inf, F32)
        for r in range(K_CHUNK // row_block):
            rs = slice(r * row_block, (r + 1) * row_block)
            xb = s[rs] + mb[rs]
            if near is not None:
                xb = xb + bias_ref[near, h, rs, :]
            x_ref[slot, h, rs, :] = xb
            for u in range(row_block // SUBLANE):
                mx = jnp.maximum(mx, xb[u * SUBLANE:(u + 1) * SUBLANE])
        m_old = m_ref[1 - slot, h]
        m_new = jnp.maximum(m_old, jnp.max(mx, axis=0, keepdims=True))
        alpha_ref[slot, h] = jnp.exp2(m_old - m_new)
        m_ref[slot, h] = m_new

    def probs(h, slot):
        p_ref[slot, h] = jnp.exp2(x_ref[slot, h] - m_ref[slot, h]).astype(BF16)

    def accumulate(h, slot, k0):
        vt = vt_ref[0, ATT_HEAD_DIM * h:ATT_HEAD_DIM * (h + 1), pl.ds(k0, K_CHUNK)]
        lhs = jnp.concatenate([vt, ones_rows], axis=0)
        acc_ref[h] = alpha_ref[slot, h] * acc_ref[h] + jnp.dot(
            lhs, p_ref[slot, h], preferred_element_type=F32)

    def one_step(t, slot, with_bias):
        k_new = pl.multiple_of(jnp.minimum(t, i) * K_CHUNK, K_CHUNK)
        k_old = pl.multiple_of(jnp.clip(t - 2, 0, i) * K_CHUNK, K_CHUNK)
        m0 = pl.multiple_of(jnp.where(t <= i, t * K_CHUNK, masked_chunk), K_CHUNK)
        mb = sc_ref[pl.ds(m0, K_CHUNK), :]
        near = jnp.clip(i - t, 0, 2) if with_bias else None
        for h in range(ATT_HEADS):
            accumulate(h, slot, k_old)
            probs(h, 1 - slot)
            logits(h, slot, k_new, mb, near)

    def two_steps(a, carry, *, with_bias):
        one_step(2 * a, 0, with_bias)
        one_step(2 * a + 1, 1, with_bias)
        return carry

    far_pairs = jnp.maximum(i - 1, 0) // 2
    all_pairs = (i + 4) // 2
    lax.fori_loop(0, far_pairs, functools.partial(two_steps, with_bias=False), 0)
    lax.fori_loop(far_pairs, all_pairs, functools.partial(two_steps, with_bias=True), 0)

    out_t = jnp.concatenate(
        [acc_ref[h, 0:ATT_HEAD_DIM, :] / acc_ref[h, ATT_HEAD_DIM:ATT_HEAD_DIM + 1, :]
         for h in range(ATT_HEADS)], axis=0)
    o_ref[0] = jnp.transpose(out_t).astype(BF16)


def _prompt_attention(rel_bias, q, iq, ikw, kb, vt, ikb, topk):
    bsz, s, _ = q.shape
    tile = lambda w: pl.BlockSpec((1, Q_TILE, w), lambda b, i: (b, i, 0))
    per_batch = lambda shape: pl.BlockSpec((1,) + shape, lambda b, i: (b, 0, 0),
                                           pipeline_mode=pl.Buffered(1))
    body = functools.partial(_pattn_body, topk=topk)
    return pl.pallas_call(
        body,
        grid=(bsz, s // Q_TILE),
        in_specs=[pl.BlockSpec(memory_space=pltpu.SMEM),
                  tile(ATT_WIDTH), tile(ATT_WIDTH), tile(LANE),
                  per_batch((s, ATT_WIDTH)), per_batch((ATT_WIDTH, s)), per_batch((s, LANE))],
        out_specs=tile(ATT_WIDTH),
        out_shape=jax.ShapeDtypeStruct((bsz, s, ATT_WIDTH), BF16),
        scratch_shapes=[
            pltpu.VMEM((s + K_CHUNK, Q_TILE), F32),
            pltpu.VMEM((s, Q_TILE), BF16),
            pltpu.VMEM((ATT_HEADS, Q_TILE, MXU_WIDTH), BF16),
            pltpu.VMEM((IDX_HEADS, Q_TILE, LANE), BF16),
            pltpu.VMEM((3, ATT_HEADS, K_CHUNK, Q_TILE), F32),
            pltpu.VMEM((2, ATT_HEADS, 1, Q_TILE), F32),
            pltpu.VMEM((2, ATT_HEADS, 1, Q_TILE), F32),
            pltpu.VMEM((ATT_HEADS, ACC_ROWS, Q_TILE), F32),
            pltpu.VMEM((2, ATT_HEADS, K_CHUNK, Q_TILE), F32),
            pltpu.VMEM((2, ATT_HEADS, K_CHUNK, Q_TILE), BF16),
        ],
        compiler_params=pltpu.CompilerParams(dimension_semantics=("arbitrary", "arbitrary"),
                                             vmem_limit_bytes=VMEM_LIMIT),
        name="prompt_attention",
    )(rel_bias, q, iq, ikw, kb, vt, ikb)


SCORE_PAGES = 32
ATTN_PAGES = 16
NEW_PAD = 16


def _head_sums(y, dec_seq):
    return jnp.concatenate(
        [jnp.sum(y[IDX_HEADS * t:IDX_HEADS * (t + 1)], axis=0, keepdims=True)
         for t in range(dec_seq)], axis=0)


def _sscore_body(pt_ref, iq_ref, w_ref, iknew_ref, *rest, dec_seq):
    del pt_ref
    pages = rest[:SCORE_PAGES]
    out_ref, new_ref = rest[SCORE_PAGES:]
    iq = iq_ref[0]
    w = w_ref[0]
    for m in range(SCORE_PAGES):
        s = jnp.dot(iq, pages[m][0].astype(BF16), preferred_element_type=F32)
        out_ref[0, :, m * PAGE_SIZE:(m + 1) * PAGE_SIZE] = _head_sums(jnp.maximum(s, 0.0) * w,
                                                                      dec_seq)

    @pl.when(pl.program_id(1) == 0)
    def _():
        s = jnp.dot(iq, iknew_ref[0], preferred_element_type=F32)
        new_sc = _head_sums(jnp.maximum(s, 0.0) * w, dec_seq)
        padded = jnp.concatenate(
            [new_sc, jnp.zeros((dec_seq, K_CHUNK - PAGE_SIZE), F32)], axis=1)
        t_idx = lax.broadcasted_iota(jnp.int32, (dec_seq, K_CHUNK), 0)
        k_idx = lax.broadcasted_iota(jnp.int32, (dec_seq, K_CHUNK), 1)
        new_ref[0] = jnp.where(k_idx <= t_idx, padded, -jnp.inf)


def _sample_scores(page_table, iq_rows, w_rows, iknew_t, cache_idx_t, dec_seq):
    nseq, n_pages = page_table.shape
    steps = n_pages // SCORE_PAGES
    rows = dec_seq * IDX_HEADS

    def page_spec(m):
        return pl.BlockSpec((1, IDX_DIM, PAGE_SIZE),
                            lambda b, j, pt: (pt[b, j * SCORE_PAGES + m], 0, 0))

    per_seq = lambda shape: pl.BlockSpec((1,) + shape, lambda b, j, pt: (b, 0, 0))
    grid_spec = pltpu.PrefetchScalarGridSpec(
        num_scalar_prefetch=1,
        grid=(nseq, steps),
        in_specs=[per_seq((rows, IDX_DIM)), per_seq((rows, 1)), per_seq((IDX_DIM, PAGE_SIZE))]
                 + [page_spec(m) for m in range(SCORE_PAGES)],
        out_specs=(pl.BlockSpec((1, dec_seq, SCORE_PAGES * PAGE_SIZE),
                                lambda b, j, pt: (b, 0, j)),
                   per_seq((dec_seq, K_CHUNK))),
    )
    return pl.pallas_call(
        functools.partial(_sscore_body, dec_seq=dec_seq),
        grid_spec=grid_spec,
        out_shape=(jax.ShapeDtypeStruct((nseq, dec_seq, n_pages * PAGE_SIZE), F32),
                   jax.ShapeDtypeStruct((nseq, dec_seq, K_CHUNK), F32)),
        compiler_params=pltpu.CompilerParams(dimension_semantics=("arbitrary", "arbitrary"),
                                             vmem_limit_bytes=VMEM_LIMIT),
        name="sample_scores",
    )(page_table, iq_rows, w_rows, iknew_t, *([cache_idx_t] * SCORE_PAGES))


SAMPLE_TOPK_LANES = LANE


def _stopk_body(sc_in_ref, o_ref, sc_ref, *, dec_seq, past, topk):
    n_keys, lanes = sc_in_ref.shape
    sc_ref[...] = sc_in_ref[...]
    q = pl.program_id(0) * lanes + lax.broadcasted_iota(jnp.int32, (1, lanes), 1)
    n_allowed = (past + 1 + q % dec_seq).astype(F32)
    _select_topk(sc_ref, lanes, n_keys // K_CHUNK, n_allowed, topk, keys_axis=0)
    o_ref[...] = sc_ref[...]


def _sample_topk(scores_t, dec_seq, past, topk):
    n_keys, n_q = scores_t.shape
    lanes = min(SAMPLE_TOPK_LANES, n_q)
    spec = pl.BlockSpec((n_keys, lanes), lambda g: (0, g))
    return pl.pallas_call(
        functools.partial(_stopk_body, dec_seq=dec_seq, past=past, topk=topk),
        grid=(n_q // lanes,),
        in_specs=[spec],
        out_specs=spec,
        out_shape=jax.ShapeDtypeStruct((n_keys, n_q), F32),
        scratch_shapes=[pltpu.VMEM((n_keys, lanes), F32)],
        compiler_params=pltpu.CompilerParams(dimension_semantics=("arbitrary",),
                                             vmem_limit_bytes=VMEM_LIMIT),
        name="sample_topk",
    )(scores_t)


def _sattn_body(pt_ref, rb_ref, mask_ref, qp_ref, knew_ref, vnew_ref, *rest, dec_seq, past):
    del pt_ref
    kpages = rest[:ATTN_PAGES]
    vpages = rest[ATTN_PAGES:2 * ATTN_PAGES]
    o_ref = rest[2 * ATTN_PAGES]
    mb_ref, m_ref, l_ref, acc_ref = rest[2 * ATTN_PAGES + 1:]
    j = pl.program_id(1)
    n_steps = pl.num_programs(1)
    rows = dec_seq * ATT_HEADS
    row_t = lax.broadcasted_iota(jnp.int32, (rows, 1), 0) // ATT_HEADS
    row_h = lax.broadcasted_iota(jnp.int32, (rows, 1), 0) % ATT_HEADS

    @pl.when(j == 0)
    def _():
        for t in range(dec_seq):
            mb_ref[ATT_HEADS * t:ATT_HEADS * (t + 1), :] = jnp.broadcast_to(
                mask_ref[0, t:t + 1, :], (ATT_HEADS, past + K_CHUNK))
        m_ref[...] = jnp.full(m_ref.shape, -1e30, F32)
        l_ref[...] = jnp.zeros(l_ref.shape, F32)
        acc_ref[...] = jnp.zeros(acc_ref.shape, F32)

    qp = qp_ref[0]

    def near_bias(dist):
        bucket = _rel_bucket(dist)
        acc = jnp.zeros(dist.shape, F32)
        for bk in range(REL_BUCKETS - 1):
            per_row = jnp.zeros((rows, 1), F32)
            for h in range(ATT_HEADS):
                per_row = jnp.where(row_h == h, rb_ref[bk, h] - rb_ref[REL_BUCKETS - 1, h], per_row)
            acc = jnp.where(bucket == bk, per_row, acc)
        return acc * LOG2E

    def attend_pages(k_ts, v_ts, mb, last_bias):
        logit = jnp.concatenate(
            [jnp.dot(qp, k_t, preferred_element_type=F32) for k_t in k_ts], axis=1) + mb
        n = len(k_ts)
        if n > 1:
            last_bias = jnp.concatenate(
                [jnp.zeros((rows, (n - 1) * PAGE_SIZE), F32), last_bias], axis=1)
        logit = logit + last_bias
        m_old = m_ref[...]
        m_new = jnp.maximum(m_old, jnp.max(logit, axis=1, keepdims=True))
        alpha = jnp.exp2(m_old - m_new)
        p = jnp.exp2(logit - m_new).astype(BF16)
        l_ref[...] = alpha * l_ref[...] + jnp.sum(p.astype(F32), axis=1, keepdims=True)
        acc = alpha * acc_ref[...]
        for m, v_t in enumerate(v_ts):
            acc = acc + lax.dot_general(p[:, m * PAGE_SIZE:(m + 1) * PAGE_SIZE], v_t, NT_DIMS,
                                        preferred_element_type=F32)
        acc_ref[...] = acc
        m_ref[...] = m_new

    is_last = j == n_steps - 1
    off = lax.broadcasted_iota(jnp.int32, (rows, PAGE_SIZE), 1)
    k0 = pl.multiple_of(j * (ATTN_PAGES * PAGE_SIZE), ATTN_PAGES * PAGE_SIZE)
    last_bias = lax.cond(is_last, lambda: near_bias(PAGE_SIZE + row_t - off),
                         lambda: jnp.zeros((rows, PAGE_SIZE), F32))
    attend_pages([kp[0].astype(BF16) for kp in kpages], [vp[0].astype(BF16) for vp in vpages],
                 mb_ref[:, pl.ds(k0, ATTN_PAGES * PAGE_SIZE)], last_bias)

    @pl.when(is_last)
    def _():
        attend_pages([knew_ref[0]], [vnew_ref[0]], mb_ref[:, past:past + PAGE_SIZE],
                     near_bias(row_t - off))
        res = acc_ref[...] / l_ref[...]
        lane_h = lax.broadcasted_iota(jnp.int32, (rows, ATT_WIDTH), 1) // ATT_HEAD_DIM
        res = jnp.where(lane_h == row_h, res, 0.0)
        o_ref[0] = _head_sums(res, dec_seq).astype(BF16)


def _sample_attention(page_table, rel_bias, mask, qp_rows, knew_t, vnew_t, cache_k_t, cache_v_t,
                      dec_seq):
    nseq, n_pages = page_table.shape
    past = n_pages * PAGE_SIZE
    steps = n_pages // ATTN_PAGES
    rows = dec_seq * ATT_HEADS

    def page_spec(m):
        return pl.BlockSpec((1, ATT_WIDTH, PAGE_SIZE),
                            lambda b, j, pt: (pt[b, j * ATTN_PAGES + m], 0, 0))

    per_seq = lambda shape: pl.BlockSpec((1,) + shape, lambda b, j, pt: (b, 0, 0))
    grid_spec = pltpu.PrefetchScalarGridSpec(
        num_scalar_prefetch=1,
        grid=(nseq, steps),
        in_specs=[pl.BlockSpec(memory_space=pltpu.SMEM),
                  per_seq((dec_seq, past + K_CHUNK)), per_seq((rows, ATT_WIDTH)),
                  per_seq((ATT_WIDTH, PAGE_SIZE)), per_seq((ATT_WIDTH, PAGE_SIZE))]
                 + [page_spec(m) for m in range(ATTN_PAGES)] * 2,
        out_specs=per_seq((dec_seq, ATT_WIDTH)),
        scratch_shapes=[
            pltpu.VMEM((rows, past + K_CHUNK), F32),
            pltpu.VMEM((rows, 1), F32),
            pltpu.VMEM((rows, 1), F32),
            pltpu.VMEM((rows, ATT_WIDTH), F32),
        ],
    )
    return pl.pallas_call(
        functools.partial(_sattn_body, dec_seq=dec_seq, past=past),
        grid_spec=grid_spec,
        out_shape=jax.ShapeDtypeStruct((nseq, dec_seq, ATT_WIDTH), BF16),
        compiler_params=pltpu.CompilerParams(dimension_semantics=("arbitrary", "arbitrary"),
                                             vmem_limit_bytes=VMEM_LIMIT),
        name="sample_attention",
    )(page_table, rel_bias, mask, qp_rows, knew_t, vnew_t,
      *([cache_k_t] * ATTN_PAGES), *([cache_v_t] * ATTN_PAGES))


FF_CHUNK = 256


def _post_body(att_ref, gla_ref, x_ref, p_ref, wo_ref, gpm_ref, gpf_ref, wg_ref, wu_ref, wd_ref,
               gpo_ref, wpg_ref, wpp_ref, o_ref, *, d_ff):
    wo = wo_ref
    mixed = jnp.dot(att_ref[...], wo[0:ATT_WIDTH, :], preferred_element_type=F32)
    mixed = mixed + jnp.dot(gla_ref[...], wo[ATT_WIDTH:ATT_WIDTH + GLA_WIDTH, :],
                            preferred_element_type=F32)
    h = x_ref[...] + _rms(mixed, gpm_ref[...])
    f = _rms(h, gpf_ref[...]).astype(BF16)
    ff = jnp.zeros(h.shape, F32)
    for c in range(d_ff // FF_CHUNK):
        cs = slice(c * FF_CHUNK, (c + 1) * FF_CHUNK)
        gate = jnp.dot(f, wg_ref[:, cs], preferred_element_type=F32)
        up = jnp.dot(f, wu_ref[:, cs], preferred_element_type=F32)
        act = (gate * jax.nn.sigmoid(gate) * up).astype(BF16)
        ff = ff + jnp.dot(act, wd_ref[cs, :], preferred_element_type=F32)
    h = h + _rms(ff, gpo_ref[...])
    gate = jax.nn.sigmoid(jnp.dot(h.astype(BF16), wpg_ref[...], preferred_element_type=F32))
    ple = jnp.dot(p_ref[...].astype(BF16), wpp_ref[...], preferred_element_type=F32)
    o_ref[...] = h + gate * ple


def _post(att, gla, x, p, wo, gpm, gpf, wg, wu, wd, gpo, wpg, wpp, tm):
    t = x.shape[0]
    d_ff = wg.shape[1]
    row = lambda w: pl.BlockSpec((tm, w), lambda i: (i, 0))
    const = lambda a: pl.BlockSpec(a.shape, lambda i: (0, 0), pipeline_mode=pl.Buffered(1))
    return pl.pallas_call(
        functools.partial(_post_body, d_ff=d_ff),
        grid=(t // tm,),
        in_specs=[row(ATT_WIDTH), row(GLA_WIDTH), row(D_MODEL), row(p.shape[1]),
                  const(wo), const(gpm), const(gpf), const(wg), const(wu), const(wd),
                  const(gpo), const(wpg), const(wpp)],
        out_specs=row(D_MODEL),
        out_shape=jax.ShapeDtypeStruct((t, D_MODEL), F32),
        compiler_params=pltpu.CompilerParams(dimension_semantics=("arbitrary",),
                                             vmem_limit_bytes=VMEM_LIMIT),
        name="post",
    )(att, gla, x, p, wo, gpm, gpf, wg, wu, wd, gpo, wpg, wpp)


def _pack_w_in(w_in):
    offs = [0]
    for wdt in (ATT_WIDTH, ATT_WIDTH, ATT_WIDTH, IDX_HEADS * IDX_DIM, IDX_DIM, IDX_HEADS,
                GLA_KEY_WIDTH, GLA_KEY_WIDTH, GLA_WIDTH, GLA_GATE_RANK, GLA_WIDTH):
        offs.append(offs[-1] + wdt)
    aq, ak, av, iq, ik, iw, gq, gk, gv, glr, gout = [w_in[:, offs[n]:offs[n + 1]] for n in range(11)]
    d = w_in.shape[0]
    zeros = lambda n: jnp.zeros((d, n), w_in.dtype)
    packed = jnp.concatenate([
        aq * (ATT_HEAD_DIM ** -0.5 * LOG2E), ak, av, iq * (IDX_DIM ** -0.5),
        ik, iw, zeros(LANE - IDX_DIM - IDX_HEADS),
        gq * (GLA_DK ** -0.5), gk, gv, glr, zeros(LANE - GLA_GATE_RANK), gout], axis=1)
    return packed.astype(BF16)


def _state_to_t(s):
    n = s.shape[0]
    return jnp.transpose(s, (0, 3, 1, 2)).reshape(n, GLA_DV, GLA_KEY_WIDTH)


def _state_from_t(st):
    n = st.shape[0]
    return jnp.transpose(st.reshape(n, GLA_DV, GLA_HEADS, GLA_DK), (0, 2, 3, 1))


def _layer(x_prompt, x_sample, p_prompt, p_sample, cache_k, cache_v, cache_idx_k, state_gla,
           page_table, rel_bias, g_pre_mix, w_in, w_gate_up, b_gate, g_gla_out, w_out,
           g_post_mix, g_pre_ffn, w_ff_gate, w_ff_up, w_ff_down, g_post_ffn, w_ple_gate,
           w_ple_proj):
    bsz, seq, d = x_prompt.shape
    nseq, dec_seq, _ = x_sample.shape
    n_pool = cache_k.shape[0]
    past = page_table.shape[1] * PAGE_SIZE
    row2 = lambda g: g.reshape(1, -1)

    w_packed = _pack_w_in(w_in)
    w_t = jnp.transpose(w_packed[:, SEG_K[0]:SEG_IKW[0] + IDX_DIM])
    w_t = jnp.concatenate([w_t[0:2 * ATT_WIDTH], w_t[SEG_IKW[0] - SEG_K[0]:]], axis=0)
    wgu = jnp.concatenate(
        [w_gate_up, jnp.zeros((LANE - GLA_GATE_RANK, GLA_KEY_WIDTH), w_gate_up.dtype)],
        axis=0).astype(BF16)
    post_w = (w_out.astype(BF16), row2(g_post_mix), row2(g_pre_ffn), w_ff_gate.astype(BF16),
              w_ff_up.astype(BF16), w_ff_down.astype(BF16), row2(g_post_ffn),
              w_ple_gate.astype(BF16), w_ple_proj.astype(BF16))
    gla_w = (wgu, row2(b_gate), row2(g_gla_out))

    tp = bsz * seq
    q, kb, iq, ikw, ikb, gin, k_t, v_t, vtb, ik_t = _inproj(
        x_prompt.reshape(tp, d), row2(g_pre_mix), w_packed, w_t, min(512, seq), seq)
    r3 = lambda a: a.reshape(bsz, seq, a.shape[-1])
    att = _prompt_attention(rel_bias, r3(q), r3(iq), r3(ikw), r3(kb), vtb, r3(ikb),
                            min(TOPK_MAX, seq // 4))
    gla_chunk = min(128, seq)
    gla_o, st_p = _gla(r3(gin), jnp.zeros((bsz, GLA_DV, GLA_KEY_WIDTH), F32), *gla_w,
                       tile=min(512, seq), chunk=gla_chunk, n_valid=gla_chunk)
    y_p = _post(att.reshape(tp, ATT_WIDTH), gla_o.reshape(tp, GLA_WIDTH), x_prompt.reshape(tp, d),
                p_prompt.reshape(tp, -1), *post_w, min(512, tp))
    heads_last = lambda a: jnp.transpose(
        a.reshape(bsz, ATT_HEADS, ATT_HEAD_DIM, seq), (0, 3, 1, 2))
    outs_p = (y_p.reshape(bsz, seq, d), heads_last(k_t), heads_last(v_t),
              jnp.swapaxes(ik_t, 1, 2), _state_from_t(st_p))

    ts = nseq * dec_seq
    q, kb, iq, ikw, ikb, gin, kf, vf, vb = _inproj(
        x_sample.reshape(ts, d), row2(g_pre_mix), w_packed, None, min(512, ts), ts)
    rows = dec_seq * ATT_HEADS
    iq_rows = iq.reshape(nseq, rows, IDX_DIM)
    w_rows = ikw[:, IDX_DIM:IDX_DIM + IDX_HEADS].reshape(nseq, rows, 1)
    head_of_col = jnp.arange(ATT_WIDTH, dtype=jnp.int32) // ATT_HEAD_DIM
    head_mask = head_of_col[None, :] == jnp.arange(ATT_HEADS, dtype=jnp.int32)[:, None]
    qp_rows = jnp.where(head_mask[None, None], q.reshape(nseq, dec_seq, 1, ATT_WIDTH),
                        jnp.zeros((), BF16)).reshape(nseq, rows, ATT_WIDTH)
    per_seq = lambda a: a.reshape(nseq, dec_seq, a.shape[-1])
    new_page_t = lambda a: jnp.pad(jnp.swapaxes(per_seq(a), 1, 2),
                                   ((0, 0), (0, 0), (0, PAGE_SIZE - dec_seq)))
    cache_idx_t = jnp.swapaxes(cache_idx_k, 1, 2)
    cache_k_t = jnp.transpose(cache_k, (0, 2, 3, 1)).reshape(n_pool, ATT_WIDTH, PAGE_SIZE)
    cache_v_t = jnp.transpose(cache_v, (0, 2, 3, 1)).reshape(n_pool, ATT_WIDTH, PAGE_SIZE)
    sc_past, sc_new = _sample_scores(page_table, iq_rows, w_rows, new_page_t(ikb[:, :IDX_DIM]),
                                     cache_idx_t, dec_seq)
    scores_t = jnp.transpose(jnp.concatenate([sc_past, sc_new], axis=2).reshape(ts, -1))
    mask_t = _sample_topk(scores_t, dec_seq, past, min(TOPK_MAX, (past + dec_seq) // 4))
    mask = jnp.transpose(mask_t).reshape(nseq, dec_seq, -1)
    att_s = _sample_attention(page_table, rel_bias, mask, qp_rows, new_page_t(kb),
                              new_page_t(vb), cache_k_t, cache_v_t, dec_seq)
    gin_s = jnp.pad(per_seq(gin), ((0, 0), (0, NEW_PAD - dec_seq), (0, 0)))
    gla_s, st_s = _gla(gin_s, _state_to_t(state_gla), *gla_w,
                       tile=NEW_PAD, chunk=NEW_PAD, n_valid=dec_seq)
    y_s = _post(att_s.reshape(ts, ATT_WIDTH), gla_s[:, :dec_seq].reshape(ts, GLA_WIDTH),
                x_sample.reshape(ts, d), p_sample.reshape(ts, -1), *post_w, min(256, ts))
    outs_s = (y_s.reshape(nseq, dec_seq, d),
              kf.reshape(nseq, dec_seq, ATT_HEADS, ATT_HEAD_DIM),
              vf.reshape(nseq, dec_seq, ATT_HEADS, ATT_HEAD_DIM),
              ikw[:, :IDX_DIM].reshape(nseq, dec_seq, IDX_DIM),
              _state_from_t(st_s))
    return outs_p, outs_s


def kernel(x_prompt, x_sample, p_prompt, p_sample, cache_k, cache_v, cache_idx_k, state_gla,
           page_table, rel_bias, g_pre_mix, w_in, w_gate_up, b_gate, g_gla_out, w_out,
           g_post_mix, g_pre_ffn, w_ff_gate, w_ff_up, w_ff_down, g_post_ffn, w_ple_gate,
           w_ple_proj):
    depth = w_in.shape[0]
    h_p, h_s = x_prompt, x_sample
    per_layer = []
    for i in range(depth):
        outs_p, outs_s = _layer(
            h_p, h_s, p_prompt[i], p_sample[i], cache_k[i], cache_v[i], cache_idx_k[i],
            state_gla[i], page_table, rel_bias, g_pre_mix[i], w_in[i], w_gate_up[i], b_gate[i],
            g_gla_out[i], w_out[i], g_post_mix[i], g_pre_ffn[i], w_ff_gate[i], w_ff_up[i],
            w_ff_down[i], g_post_ffn[i], w_ple_gate[i], w_ple_proj[i])
        h_p, h_s = outs_p[0], outs_s[0]
        per_layer.append(outs_p[1:] + outs_s[1:])
    stacked = [jnp.stack([lay[n] for lay in per_layer]) for n in range(8)]
    return (h_p, h_s, stacked[0], stacked[1], stacked[2], stacked[3],
            stacked[4], stacked[5], stacked[6], stacked[7])
```

```python
import functools
import math

import jax
import jax.numpy as jnp
from jax import lax
from jax.experimental import pallas as pl
from jax.experimental.pallas import tpu as pltpu

F32 = jnp.float32
BF16 = jnp.bfloat16

D_MODEL = 1024
ATT_HEADS = 8
ATT_HEAD_DIM = 64
ATT_WIDTH = ATT_HEADS * ATT_HEAD_DIM
IDX_HEADS = 8
IDX_DIM = 64
TOPK_MAX = 256
PAGE_SIZE = 128
GLA_HEADS = 4
GLA_DK = 64
GLA_DV = 128
GLA_KEY_WIDTH = GLA_HEADS * GLA_DK
GLA_WIDTH = GLA_HEADS * GLA_DV
GLA_GATE_RANK = 16
GLA_GATE_TAU = 16.0
REL_BUCKETS = 32
REL_MAX_DIST = 128
RMS_EPS = 1e-6
LOG2E = math.log2(math.e)

LANE = 128
SUBLANE = 8
BF16_ROWS = 16
MXU_WIDTH = 256
Q_TILE = MXU_WIDTH
K_CHUNK = MXU_WIDTH
HEAD_GROUP = MXU_WIDTH // ATT_HEAD_DIM
ACC_ROWS = ATT_HEAD_DIM + BF16_ROWS
VMEM_LIMIT = 56 * 1024 * 1024
COARSE_ITERS = 10
FINE_ITERS = 9

SEG_Q = (0, 512)
SEG_K = (512, 1024)
SEG_V = (1024, 1536)
SEG_IQ = (1536, 2048)
SEG_IKW = (2048, 2176)
SEG_GLA = (2176, 3840)
PROJ_PACKED = 3840
GLA_IN_WIDTH = SEG_GLA[1] - SEG_GLA[0]

NT_DIMS = (((1,), (1,)), ((), ()))
TN_DIMS = (((0,), (0,)), ((), ()))


def _rms(x, g):
    return x * lax.rsqrt(jnp.mean(x * x, axis=-1, keepdims=True) + RMS_EPS) * g


def _inproj_body(x_ref, g_ref, w_ref, *rest, tokens_minor):
    a = _rms(x_ref[...], g_ref[...]).astype(BF16)

    def proj(seg):
        return jnp.dot(a, w_ref[:, seg[0]:seg[1]], preferred_element_type=F32)

    if tokens_minor:
        wt_ref, q_ref, kb_ref, iq_ref, ikw_ref, ikb_ref, gla_ref, kt_ref, vt_ref, vtb_ref, ikt_ref = rest
    else:
        q_ref, kb_ref, iq_ref, ikw_ref, ikb_ref, gla_ref, k_ref, v_ref, vb_ref = rest

    q_ref[...] = proj(SEG_Q).astype(BF16)
    kf = proj(SEG_K)
    kb_ref[...] = kf.astype(BF16)
    iq_ref[...] = proj(SEG_IQ).astype(BF16)
    lane = lax.broadcasted_iota(jnp.int32, (1, LANE), 1)
    is_w = (lane >= IDX_DIM) & (lane < IDX_DIM + IDX_HEADS)
    ikw = proj(SEG_IKW) * jnp.where(is_w, IDX_HEADS ** -0.5, 1.0)
    ikw_ref[...] = ikw
    ikb_ref[...] = ikw.astype(BF16)
    gla_ref[...] = proj(SEG_GLA)
    if tokens_minor:
        def proj_t(lo, hi):
            return lax.dot_general(wt_ref[lo:hi, :], a, NT_DIMS, preferred_element_type=F32)

        kt_ref[0] = proj_t(0, ATT_WIDTH)
        vt = proj_t(ATT_WIDTH, 2 * ATT_WIDTH)
        vt_ref[0] = vt
        vtb_ref[0] = vt.astype(BF16)
        ikt_ref[0] = proj_t(2 * ATT_WIDTH, 2 * ATT_WIDTH + IDX_DIM)
    else:
        k_ref[...] = kf
        vf = proj(SEG_V)
        v_ref[...] = vf
        vb_ref[...] = vf.astype(BF16)


def _inproj(x, g, w_packed, w_t, tm, seq):
    t = x.shape[0]
    tokens_minor = w_t is not None
    tiles_per_seq = seq // tm
    row = lambda w: pl.BlockSpec((tm, w), lambda i: (i, 0))
    const = lambda s: pl.BlockSpec(s, lambda i: (0, 0))
    col = lambda r: pl.BlockSpec((1, r, tm), lambda i: (i // tiles_per_seq, 0, i % tiles_per_seq))
    rows_shape = lambda w, dt: jax.ShapeDtypeStruct((t, w), dt)
    cols_shape = lambda r, dt: jax.ShapeDtypeStruct((t // seq, r, seq), dt)
    out_shape = [rows_shape(ATT_WIDTH, BF16), rows_shape(ATT_WIDTH, BF16),
                 rows_shape(ATT_WIDTH, BF16), rows_shape(LANE, F32), rows_shape(LANE, BF16),
                 rows_shape(GLA_IN_WIDTH, F32)]
    out_specs = [row(ATT_WIDTH), row(ATT_WIDTH), row(ATT_WIDTH), row(LANE), row(LANE),
                 row(GLA_IN_WIDTH)]
    in_specs = [row(D_MODEL), const((1, D_MODEL)), const((D_MODEL, PROJ_PACKED))]
    operands = [x, g, w_packed]
    if tokens_minor:
        in_specs.append(const(w_t.shape))
        operands.append(w_t)
        out_shape += [cols_shape(ATT_WIDTH, F32), cols_shape(ATT_WIDTH, F32),
                      cols_shape(ATT_WIDTH, BF16), cols_shape(IDX_DIM, F32)]
        out_specs += [col(ATT_WIDTH), col(ATT_WIDTH), col(ATT_WIDTH), col(IDX_DIM)]
    else:
        out_shape += [rows_shape(ATT_WIDTH, F32), rows_shape(ATT_WIDTH, F32),
                      rows_shape(ATT_WIDTH, BF16)]
        out_specs += [row(ATT_WIDTH), row(ATT_WIDTH), row(ATT_WIDTH)]
    return pl.pallas_call(
        functools.partial(_inproj_body, tokens_minor=tokens_minor),
        grid=(t // tm,),
        in_specs=in_specs,
        out_specs=tuple(out_specs),
        out_shape=tuple(out_shape),
        compiler_params=pltpu.CompilerParams(dimension_semantics=("arbitrary",),
                                             vmem_limit_bytes=VMEM_LIMIT),
        name="inproj",
    )(*operands)


def _log_sigmoid(z):
    return jnp.minimum(z, 0.0) - jnp.log1p(jnp.exp(-jnp.abs(z)))


def _gla_body(gin_ref, s0_ref, wgu_ref, bg_ref, gg_ref, o_ref, sout_ref, st_ref, *, chunk, n_valid):
    @pl.when(pl.program_id(1) == 0)
    def _():
        st_ref[...] = s0_ref[0]

    n_chunks = gin_ref.shape[1] // chunk
    ri = lax.broadcasted_iota(jnp.int32, (chunk, chunk), 0)
    ci = lax.broadcasted_iota(jnp.int32, (chunk, chunk), 1)
    causal = ri >= ci
    tri = causal.astype(F32)
    row_valid = lax.broadcasted_iota(jnp.int32, (chunk, 1), 0) < n_valid

    def one_chunk(c, carry):
        r0 = pl.multiple_of(c * chunk, chunk)
        blk = gin_ref[0, pl.ds(r0, chunk), :]
        gq = blk[:, 0:256]
        gk = blk[:, 256:512]
        gv = blk[:, 512:1024]
        glr = blk[:, 1024:1152]
        gout = blk[:, 1152:1664]
        z = jnp.dot(glr.astype(BF16), wgu_ref[...], preferred_element_type=F32) + bg_ref[...]
        la = _log_sigmoid(z) / GLA_GATE_TAU
        if n_valid < chunk:
            la = jnp.where(row_valid, la, 0.0)
        b = jnp.dot(tri, la, precision=lax.Precision.HIGHEST, preferred_element_type=F32)
        b_last = b[chunk - 1:chunk, :]
        qt = gq * jnp.exp(b)
        kt = gk * jnp.exp(-b)
        kd = gk * jnp.exp(b_last - b)
        dec = jnp.exp(b_last)
        st = st_ref[...]
        o_parts = []
        st_parts = []
        for h in range(GLA_HEADS):
            ks = slice(GLA_DK * h, GLA_DK * (h + 1))
            vs = slice(GLA_DV * h, GLA_DV * (h + 1))
            qh = qt[:, ks].astype(BF16)
            kh = kt[:, ks].astype(BF16)
            kdh = kd[:, ks].astype(BF16)
            vh = gv[:, vs].astype(BF16)
            sth = st[:, ks]
            attn = lax.dot_general(qh, kh, NT_DIMS, preferred_element_type=F32)
            attn = jnp.where(causal, attn, 0.0)
            o = jnp.dot(attn.astype(BF16), vh, preferred_element_type=F32)
            o = o + lax.dot_general(qh, sth.astype(BF16), NT_DIMS, preferred_element_type=F32)
            upd = lax.dot_general(vh, kdh, TN_DIMS, preferred_element_type=F32)
            st_parts.append(dec[:, ks] * sth + upd)
            o_parts.append(_rms(o, gg_ref[...]))
        st_ref[...] = jnp.concatenate(st_parts, axis=1)
        o_all = jnp.concatenate(o_parts, axis=1)
        gate = gout * jax.nn.sigmoid(gout)
        o_ref[0, pl.ds(r0, chunk), :] = (o_all * gate).astype(BF16)
        return carry

    lax.fori_loop(0, n_chunks, one_chunk, 0)
    sout_ref[0] = st_ref[...]


def _gla(gin, s0_t, wgu, bg, gg, *, tile, chunk, n_valid):
    nseq, t, _ = gin.shape
    body = functools.partial(_gla_body, chunk=chunk, n_valid=n_valid)
    return pl.pallas_call(
        body,
        grid=(nseq, t // tile),
        in_specs=[
            pl.BlockSpec((1, tile, GLA_IN_WIDTH), lambda b, j: (b, j, 0)),
            pl.BlockSpec((1, GLA_DV, GLA_KEY_WIDTH), lambda b, j: (b, 0, 0)),
            pl.BlockSpec((LANE, GLA_KEY_WIDTH), lambda b, j: (0, 0)),
            pl.BlockSpec((1, GLA_KEY_WIDTH), lambda b, j: (0, 0)),
            pl.BlockSpec((1, GLA_DV), lambda b, j: (0, 0)),
        ],
        out_specs=(
            pl.BlockSpec((1, tile, GLA_WIDTH), lambda b, j: (b, j, 0)),
            pl.BlockSpec((1, GLA_DV, GLA_KEY_WIDTH), lambda b, j: (b, 0, 0)),
        ),
        out_shape=(
            jax.ShapeDtypeStruct((nseq, t, GLA_WIDTH), BF16),
            jax.ShapeDtypeStruct((nseq, GLA_DV, GLA_KEY_WIDTH), F32),
        ),
        scratch_shapes=[pltpu.VMEM((GLA_DV, GLA_KEY_WIDTH), F32)],
        compiler_params=pltpu.CompilerParams(dimension_semantics=("arbitrary", "arbitrary"),
                                             vmem_limit_bytes=VMEM_LIMIT),
        name="gla",
    )(gin, s0_t, wgu, bg, gg)


def _rel_bucket(dist):
    n = jnp.maximum(dist, 0)
    exact = REL_BUCKETS // 2
    nf = jnp.maximum(n, 1).astype(F32)
    large = exact + (jnp.log(nf / exact) / math.log(REL_MAX_DIST / exact)
                     * (REL_BUCKETS - exact)).astype(jnp.int32)
    large = jnp.minimum(large, REL_BUCKETS - 1)
    return jnp.where(n < exact, n, large)


def _select_topk(sc_ref, n_queries, n_chunks, n_allowed, topk, keys_axis, max_iter=18,
                 coarse_ref=None, coarse_iter=0):
    neg_inf = -jnp.inf
    n_acc = 4 if keys_axis == 0 else K_CHUNK // LANE
    if keys_axis == 1:
        part_shape = (n_queries, LANE)
        q_shape = (n_queries, 1)
        chunk_shape = (n_queries, K_CHUNK)
        load = lambda k0: sc_ref[:, pl.ds(k0, K_CHUNK)]
        pieces = lambda x: [x[:, j * LANE:(j + 1) * LANE] for j in range(K_CHUNK // LANE)]
        last = lambda r: r[:, K_CHUNK - 1:K_CHUNK]
    else:
        part_shape = (SUBLANE, n_queries)
        q_shape = (1, n_queries)
        chunk_shape = (K_CHUNK, n_queries)
        load = lambda k0: sc_ref[pl.ds(k0, K_CHUNK), :]
        pieces = lambda x: [x[j * SUBLANE:(j + 1) * SUBLANE] for j in range(K_CHUNK // SUBLANE)]
        last = lambda r: r[K_CHUNK - 1:K_CHUNK, :]

    def store(k0, val):
        if keys_axis == 1:
            sc_ref[:, pl.ds(k0, K_CHUNK)] = val
        else:
            sc_ref[pl.ds(k0, K_CHUNK), :] = val

    def sweep(fns, inits, reds, combines):
        n_stat = len(fns)

        def body(c, accs):
            x = load(pl.multiple_of(c * K_CHUNK, K_CHUNK))
            accs = [list(a) for a in accs]
            for n, piece in enumerate(pieces(x)):
                for s in range(n_stat):
                    accs[s][n % n_acc] = fns[s](accs[s][n % n_acc], piece)
            return tuple(tuple(a) for a in accs)

        init = tuple(tuple(jnp.full(part_shape, inits[s], F32) for _ in range(n_acc))
                     for s in range(n_stat))
        accs = lax.fori_loop(0, n_chunks, body, init)
        out = []
        for s in range(n_stat):
            acc = accs[s][0]
            for other in accs[s][1:]:
                acc = combines[s](acc, other)
            out.append(reds[s](acc, axis=keys_axis, keepdims=True))
        return out

    def count_ge(t):
        tb = jnp.broadcast_to(t, part_shape)
        return sweep([lambda a, x: jnp.where(x >= tb, a + 1.0, a)], [0.0], [jnp.sum],
                     [jnp.add])[0]

    def count_gt(t):
        tb = jnp.broadcast_to(t, part_shape)
        return sweep([lambda a, x: a + jnp.where(x > tb, 1.0, 0.0)], [0.0], [jnp.sum],
                     [jnp.add])[0]

    def max_below(t):
        tb = jnp.broadcast_to(t, part_shape)
        return sweep([lambda a, x: jnp.maximum(a, jnp.where(x < tb, x, neg_inf))], [neg_inf],
                     [jnp.max], [jnp.maximum])[0]

    row_max, row_min = sweep(
        [jnp.maximum, lambda a, x: jnp.minimum(a, jnp.where(x == neg_inf, jnp.inf, x))],
        [neg_inf, jnp.inf], [jnp.max, jnp.min], [jnp.maximum, jnp.minimum])

    k = float(topk)
    take_all = n_allowed <= k
    lo0 = row_min
    hi0 = row_max + jnp.maximum(jnp.abs(row_max) * 1e-6, 1e-30)
    t0 = jnp.full(q_shape, jnp.finfo(F32).min, F32)
    done0 = jnp.where(take_all, 1.0, 0.0)

    if coarse_ref is not None:
        rows_b = BF16_ROWS
        one_b = jnp.ones((rows_b, n_queries), BF16)
        zero_b = jnp.zeros((rows_b, n_queries), BF16)

        def count_b(m):
            mb_ = jnp.broadcast_to(m.astype(BF16), (rows_b, n_queries))

            def body(c, accs):
                x = coarse_ref[pl.ds(pl.multiple_of(c * K_CHUNK, K_CHUNK), K_CHUNK), :]
                accs = list(accs)
                for n in range(K_CHUNK // rows_b):
                    piece = x[n * rows_b:(n + 1) * rows_b]
                    accs[n % n_acc] = accs[n % n_acc] + jnp.where(piece >= mb_, one_b, zero_b)
                return tuple(accs)

            accs = lax.fori_loop(0, n_chunks, body, tuple(zero_b for _ in range(n_acc)))
            total = accs[0].astype(F32)
            for other in accs[1:]:
                total = total + other.astype(F32)
            return jnp.sum(total, axis=0, keepdims=True)

        def coarse_body(_, st):
            lo, hi = st
            m = (lo + (hi - lo) * 0.5).astype(BF16).astype(F32)
            inside = (m > lo) & (m < hi)
            c = count_b(m)
            below = m - (jnp.abs(m) * (2.0 ** -7) + 1e-30)
            hi = jnp.where(inside & (c < k), m, hi)
            lo = jnp.where(inside & (c >= k) & (below > lo), below, lo)
            return lo, hi

        lo0, hi0 = lax.fori_loop(0, coarse_iter, coarse_body, (lo0, hi0))

    def bis_body(_, st):
        lo, hi, t, done = st
        mid = lo + (hi - lo) * 0.5
        c = count_ge(mid)
        hit = (c == k) & (done < 0.5)
        t = jnp.where(hit, mid, t)
        done = jnp.where(hit, 1.0, done)
        lo = jnp.where(c > k, mid, lo)
        hi = jnp.where(c < k, mid, hi)
        return lo, hi, t, done

    lo, hi, t, done = lax.fori_loop(0, max_iter, bis_body, (lo0, hi0, t0, done0))

    all_done = jnp.min(done) > 0.5

    def write(fn):
        def body(c, carry):
            k0 = pl.multiple_of(c * K_CHUNK, K_CHUNK)
            sel, carry = fn(load(k0), carry)
            store(k0, jnp.where(sel, 0.0, neg_inf))
            return carry
        return body

    def walk_down():
        def snap_cond(st):
            return jnp.min(st[2]) < 0.5

        def snap_body(st):
            hi_, t_, done_, over_ = st
            v = max_below(hi_)
            c = count_ge(v)
            ok = (c >= k) & (done_ < 0.5)
            t_ = jnp.where(ok, v, t_)
            over_ = jnp.where(ok, c - k, over_)
            hi_ = jnp.where((done_ < 0.5) & jnp.logical_not(ok), v, hi_)
            done_ = jnp.where(ok, 1.0, done_)
            return hi_, t_, done_, over_

        _, t_, _, over_ = lax.while_loop(snap_cond, snap_body,
                                         (hi, t, done, jnp.zeros(q_shape, F32)))
        return t_, over_

    tv, over = lax.cond(all_done, lambda: (t, jnp.zeros(q_shape, F32)), walk_down)
    has_ties = jnp.max(over) > 0.5

    @pl.when(jnp.logical_not(has_ties))
    def _():
        tb = jnp.broadcast_to(tv, chunk_shape)
        lax.fori_loop(0, n_chunks, write(lambda x, cr: (x >= tb, cr)), 0)

    @pl.when(has_ties)
    def _():
        quota = k - count_gt(tv)
        tb = jnp.broadcast_to(tv, chunk_shape)
        qb = jnp.broadcast_to(quota, chunk_shape)
        ui = lax.broadcasted_iota(jnp.int32, (K_CHUNK, K_CHUNK), 0)
        uj = lax.broadcasted_iota(jnp.int32, (K_CHUNK, K_CHUNK), 1)
        prefix = ((ui <= uj) if keys_axis == 1 else (ui >= uj)).astype(BF16)

        def tie_fn(x, seen):
            tie = x == tb
            if keys_axis == 1:
                rank = jnp.dot(tie.astype(BF16), prefix, preferred_element_type=F32) + seen
            else:
                rank = jnp.dot(prefix, tie.astype(BF16), preferred_element_type=F32) + seen
            sel = (x > tb) | (tie & (rank <= qb))
            return sel, last(rank)

        lax.fori_loop(0, n_chunks, write(tie_fn), jnp.zeros(q_shape, F32))


def _head_lane_mask(h, width):
    lane = lax.broadcasted_iota(jnp.int32, (1, width), 1)
    lo = (h % HEAD_GROUP) * ATT_HEAD_DIM
    return (lane >= lo) & (lane < lo + ATT_HEAD_DIM)


def _pattn_body(rb_ref, q_ref, iq_ref, w_ref, k_ref, vt_ref, ik_ref, o_ref,
                sc_ref, scb_ref, qp_ref, iqp_ref, bias_ref, m_ref, alpha_ref, acc_ref, x_ref,
                p_ref, *, topk):
    i = pl.program_id(1)
    n_chunks = i + 1

    @pl.when((pl.program_id(0) == 0) & (i == 0))
    def _():
        kk = lax.broadcasted_iota(jnp.int32, (K_CHUNK, Q_TILE), 0)
        qq = lax.broadcasted_iota(jnp.int32, (K_CHUNK, Q_TILE), 1)
        for delta in range(2):
            bucket = _rel_bucket(qq - kk + Q_TILE * delta)
            for h in range(ATT_HEADS):
                acc = jnp.zeros((K_CHUNK, Q_TILE), F32)
                for bk in range(REL_BUCKETS - 1):
                    acc = jnp.where(bucket == bk, rb_ref[bk, h] - rb_ref[REL_BUCKETS - 1, h], acc)
                bias_ref[delta, h] = acc * LOG2E
        bias_ref[2] = jnp.zeros(bias_ref.shape[1:], F32)

    q = q_ref[0]
    iq = iq_ref[0]
    for h in range(ATT_HEADS):
        g = h // HEAD_GROUP
        qp_ref[h] = jnp.where(_head_lane_mask(h, MXU_WIDTH),
                              q[:, MXU_WIDTH * g:MXU_WIDTH * (g + 1)], 0)
        iqp_ref[h] = jnp.concatenate(
            [iq[:, IDX_DIM * h:IDX_DIM * (h + 1)], jnp.zeros((Q_TILE, LANE - IDX_DIM), BF16)], axis=1)
    w_t = jnp.transpose(w_ref[0])[IDX_DIM:IDX_DIM + IDX_HEADS, :]

    q_pos = i * Q_TILE + lax.broadcasted_iota(jnp.int32, (1, Q_TILE), 1)

    def score_chunk(c, carry, *, diagonal):
        k0 = pl.multiple_of(c * K_CHUNK, K_CHUNK)
        ikc = ik_ref[0, pl.ds(k0, K_CHUNK), :]
        acc = jnp.zeros((K_CHUNK, Q_TILE), F32)
        for h in range(IDX_HEADS):
            s = lax.dot_general(ikc, iqp_ref[h], NT_DIMS, preferred_element_type=F32)
            acc = acc + jnp.maximum(s, 0.0) * w_t[h:h + 1, :]
        if diagonal:
            k_pos = k0 + lax.broadcasted_iota(jnp.int32, (K_CHUNK, 1), 0)
            acc = jnp.where(k_pos <= q_pos, acc, -jnp.inf)
        sc_ref[pl.ds(k0, K_CHUNK), :] = acc
        scb_ref[pl.ds(k0, K_CHUNK), :] = acc.astype(BF16)
        return carry

    def score_pair(a, carry):
        score_chunk(2 * a, carry, diagonal=False)
        return score_chunk(2 * a + 1, carry, diagonal=False)

    lax.fori_loop(0, i // 2, score_pair, 0)
    lax.fori_loop(2 * (i // 2), i, functools.partial(score_chunk, diagonal=False), 0)
    score_chunk(i, 0, diagonal=True)

    _select_topk(sc_ref, Q_TILE, n_chunks, (q_pos + 1).astype(F32), topk, keys_axis=0,
                 max_iter=FINE_ITERS, coarse_ref=scb_ref, coarse_iter=COARSE_ITERS)

    m_ref[...] = jnp.full(m_ref.shape, -1e30, F32)
    alpha_ref[...] = jnp.ones(alpha_ref.shape, F32)
    acc_ref[...] = jnp.zeros(acc_ref.shape, F32)
    p_ref[0] = jnp.zeros(p_ref.shape[1:], BF16)
    x_ref[1] = jnp.full(x_ref.shape[1:], -jnp.inf, F32)
    ones_rows = jnp.ones((BF16_ROWS, K_CHUNK), BF16)
    row_block = 4 * SUBLANE

    def logits(h, slot, k0, mb, near):
        g = h // HEAD_GROUP
        kc = k_ref[0, pl.ds(k0, K_CHUNK), MXU_WIDTH * g:MXU_WIDTH * (g + 1)]
        s = lax.dot_general(kc, qp_ref[h], NT_DIMS, preferred_element_type=F32)
        mx = jnp.full((SUBLANE, Q_TILE), -jnp.inf, F32)
        for r in range(K_CHUNK // row_block):
            rs = slice(r * row_block, (r + 1) * row_block)
            xb = s[rs] + mb[rs]
            if near is not None:
                xb = xb + bias_ref[near, h, rs, :]
            x_ref[slot, h, rs, :] = xb
            for u in range(row_block // SUBLANE):
                mx = jnp.maximum(mx, xb[u * SUBLANE:(u + 1) * SUBLANE])
        m_old = m_ref[1 - slot, h]
        m_new = jnp.maximum(m_old, jnp.max(mx, axis=0, keepdims=True))
        alpha_ref[slot, h] = jnp.exp2(m_old - m_new)
        m_ref[slot, h] = m_new

    def probs(h, slot):
        p_ref[slot, h] = jnp.exp2(x_ref[slot, h] - m_ref[slot, h]).astype(BF16)

    def accumulate(h, slot, k0):
        vt = vt_ref[0, ATT_HEAD_DIM * h:ATT_HEAD_DIM * (h + 1), pl.ds(k0, K_CHUNK)]
        lhs = jnp.concatenate([vt, ones_rows], axis=0)
        acc_ref[h] = alpha_ref[slot, h] * acc_ref[h] + jnp.dot(
            lhs, p_ref[slot, h], preferred_element_type=F32)

    def chunk_start(c):
        return pl.multiple_of(jnp.maximum(c, 0) * K_CHUNK, K_CHUNK)

    def one_step(t, slot, with_bias):
        k_new = chunk_start(t)
        k_old = chunk_start(t - 2)
        mb = sc_ref[pl.ds(k_new, K_CHUNK), :]
        near = jnp.minimum(i - t, 2) if with_bias else None
        for h in range(ATT_HEADS):
            accumulate(h, slot, k_old)
            probs(h, 1 - slot)
            logits(h, slot, k_new, mb, near)

    def two_steps(a, carry, *, with_bias):
        one_step(2 * a, 0, with_bias)
        one_step(2 * a + 1, 1, with_bias)
        return carry

    far_pairs = jnp.maximum(i - 1, 0) // 2
    n_pairs = n_chunks // 2
    lax.fori_loop(0, far_pairs, functools.partial(two_steps, with_bias=False), 0)
    lax.fori_loop(far_pairs, n_pairs, functools.partial(two_steps, with_bias=True), 0)

    @pl.when(n_chunks % 2 == 1)
    def _():
        one_step(i, 0, True)

    last = i % 2
    for h in range(ATT_HEADS):
        accumulate(h, 1 - last, chunk_start(i - 1))
    for h in range(ATT_HEADS):
        probs(h, last)
    for h in range(ATT_HEADS):
        accumulate(h, last, chunk_start(i))

    out_t = jnp.concatenate(
        [acc_ref[h, 0:ATT_HEAD_DIM, :] / acc_ref[h, ATT_HEAD_DIM:ATT_HEAD_DIM + 1, :]
         for h in range(ATT_HEADS)], axis=0)
    o_ref[0] = jnp.transpose(out_t).astype(BF16)


def _prompt_attention(rel_bias, q, iq, ikw, kb, vt, ikb, topk):
    bsz, s, _ = q.shape
    tile = lambda w: pl.BlockSpec((1, Q_TILE, w), lambda b, i: (b, i, 0))
    per_batch = lambda shape: pl.BlockSpec((1,) + shape, lambda b, i: (b, 0, 0),
                                           pipeline_mode=pl.Buffered(1))
    body = functools.partial(_pattn_body, topk=topk)
    return pl.pallas_call(
        body,
        grid=(bsz, s // Q_TILE),
        in_specs=[pl.BlockSpec(memory_space=pltpu.SMEM),
                  tile(ATT_WIDTH), tile(ATT_WIDTH), tile(LANE),
                  per_batch((s, ATT_WIDTH)), per_batch((ATT_WIDTH, s)), per_batch((s, LANE))],
        out_specs=tile(ATT_WIDTH),
        out_shape=jax.ShapeDtypeStruct((bsz, s, ATT_WIDTH), BF16),
        scratch_shapes=[
            pltpu.VMEM((s, Q_TILE), F32),
            pltpu.VMEM((s, Q_TILE), BF16),
            pltpu.VMEM((ATT_HEADS, Q_TILE, MXU_WIDTH), BF16),
            pltpu.VMEM((IDX_HEADS, Q_TILE, LANE), BF16),
            pltpu.VMEM((3, ATT_HEADS, K_CHUNK, Q_TILE), F32),
            pltpu.VMEM((2, ATT_HEADS, 1, Q_TILE), F32),
            pltpu.VMEM((2, ATT_HEADS, 1, Q_TILE), F32),
            pltpu.VMEM((ATT_HEADS, ACC_ROWS, Q_TILE), F32),
            pltpu.VMEM((2, ATT_HEADS, K_CHUNK, Q_TILE), F32),
            pltpu.VMEM((2, ATT_HEADS, K_CHUNK, Q_TILE), BF16),
        ],
        compiler_params=pltpu.CompilerParams(dimension_semantics=("arbitrary", "arbitrary"),
                                             vmem_limit_bytes=VMEM_LIMIT),
        name="prompt_attention",
    )(rel_bias, q, iq, ikw, kb, vt, ikb)


SCORE_PAGES = 32
ATTN_PAGES = 16
NEW_PAD = 16


def _head_sums(y, dec_seq):
    return jnp.concatenate(
        [jnp.sum(y[IDX_HEADS * t:IDX_HEADS * (t + 1)], axis=0, keepdims=True)
         for t in range(dec_seq)], axis=0)


def _sscore_body(pt_ref, iq_ref, w_ref, iknew_ref, *rest, dec_seq):
    del pt_ref
    pages = rest[:SCORE_PAGES]
    out_ref, new_ref = rest[SCORE_PAGES:]
    iq = iq_ref[0]
    w = w_ref[0]
    for m in range(SCORE_PAGES):
        s = jnp.dot(iq, pages[m][0].astype(BF16), preferred_element_type=F32)
        out_ref[0, :, m * PAGE_SIZE:(m + 1) * PAGE_SIZE] = _head_sums(jnp.maximum(s, 0.0) * w,
                                                                      dec_seq)

    @pl.when(pl.program_id(1) == 0)
    def _():
        s = jnp.dot(iq, iknew_ref[0], preferred_element_type=F32)
        new_sc = _head_sums(jnp.maximum(s, 0.0) * w, dec_seq)
        padded = jnp.concatenate(
            [new_sc, jnp.zeros((dec_seq, K_CHUNK - PAGE_SIZE), F32)], axis=1)
        t_idx = lax.broadcasted_iota(jnp.int32, (dec_seq, K_CHUNK), 0)
        k_idx = lax.broadcasted_iota(jnp.int32, (dec_seq, K_CHUNK), 1)
        new_ref[0] = jnp.where(k_idx <= t_idx, padded, -jnp.inf)


def _sample_scores(page_table, iq_rows, w_rows, iknew_t, cache_idx_t, dec_seq):
    nseq, n_pages = page_table.shape
    steps = n_pages // SCORE_PAGES
    rows = dec_seq * IDX_HEADS

    def page_spec(m):
        return pl.BlockSpec((1, IDX_DIM, PAGE_SIZE),
                            lambda b, j, pt: (pt[b, j * SCORE_PAGES + m], 0, 0))

    per_seq = lambda shape: pl.BlockSpec((1,) + shape, lambda b, j, pt: (b, 0, 0))
    grid_spec = pltpu.PrefetchScalarGridSpec(
        num_scalar_prefetch=1,
        grid=(nseq, steps),
        in_specs=[per_seq((rows, IDX_DIM)), per_seq((rows, 1)), per_seq((IDX_DIM, PAGE_SIZE))]
                 + [page_spec(m) for m in range(SCORE_PAGES)],
        out_specs=(pl.BlockSpec((1, dec_seq, SCORE_PAGES * PAGE_SIZE),
                                lambda b, j, pt: (b, 0, j)),
                   per_seq((dec_seq, K_CHUNK))),
    )
    return pl.pallas_call(
        functools.partial(_sscore_body, dec_seq=dec_seq),
        grid_spec=grid_spec,
        out_shape=(jax.ShapeDtypeStruct((nseq, dec_seq, n_pages * PAGE_SIZE), F32),
                   jax.ShapeDtypeStruct((nseq, dec_seq, K_CHUNK), F32)),
        compiler_params=pltpu.CompilerParams(dimension_semantics=("arbitrary", "arbitrary"),
                                             vmem_limit_bytes=VMEM_LIMIT),
        name="sample_scores",
    )(page_table, iq_rows, w_rows, iknew_t, *([cache_idx_t] * SCORE_PAGES))


SAMPLE_TOPK_LANES = LANE


def _stopk_body(sc_in_ref, o_ref, sc_ref, *, dec_seq, past, topk):
    n_keys, lanes = sc_in_ref.shape
    sc_ref[...] = sc_in_ref[...]
    q = pl.program_id(0) * lanes + lax.broadcasted_iota(jnp.int32, (1, lanes), 1)
    n_allowed = (past + 1 + q % dec_seq).astype(F32)
    _select_topk(sc_ref, lanes, n_keys // K_CHUNK, n_allowed, topk, keys_axis=0)
    o_ref[...] = sc_ref[...]


def _sample_topk(scores_t, dec_seq, past, topk):
    n_keys, n_q = scores_t.shape
    lanes = min(SAMPLE_TOPK_LANES, n_q)
    spec = pl.BlockSpec((n_keys, lanes), lambda g: (0, g))
    return pl.pallas_call(
        functools.partial(_stopk_body, dec_seq=dec_seq, past=past, topk=topk),
        grid=(n_q // lanes,),
        in_specs=[spec],
        out_specs=spec,
        out_shape=jax.ShapeDtypeStruct((n_keys, n_q), F32),
        scratch_shapes=[pltpu.VMEM((n_keys, lanes), F32)],
        compiler_params=pltpu.CompilerParams(dimension_semantics=("arbitrary",),
                                             vmem_limit_bytes=VMEM_LIMIT),
        name="sample_topk",
    )(scores_t)


def _sattn_body(pt_ref, rb_ref, mask_ref, qp_ref, knew_ref, vnew_ref, *rest, dec_seq, past):
    del pt_ref
    kpages = rest[:ATTN_PAGES]
    vpages = rest[ATTN_PAGES:2 * ATTN_PAGES]
    o_ref = rest[2 * ATTN_PAGES]
    mb_ref, m_ref, l_ref, acc_ref = rest[2 * ATTN_PAGES + 1:]
    j = pl.program_id(1)
    n_steps = pl.num_programs(1)
    rows = dec_seq * ATT_HEADS
    row_t = lax.broadcasted_iota(jnp.int32, (rows, 1), 0) // ATT_HEADS
    row_h = lax.broadcasted_iota(jnp.int32, (rows, 1), 0) % ATT_HEADS

    @pl.when(j == 0)
    def _():
        for t in range(dec_seq):
            mb_ref[ATT_HEADS * t:ATT_HEADS * (t + 1), :] = jnp.broadcast_to(
                mask_ref[0, t:t + 1, :], (ATT_HEADS, past + K_CHUNK))
        m_ref[...] = jnp.full(m_ref.shape, -1e30, F32)
        l_ref[...] = jnp.zeros(l_ref.shape, F32)
        acc_ref[...] = jnp.zeros(acc_ref.shape, F32)

    qp = qp_ref[0]

    def near_bias(dist):
        bucket = _rel_bucket(dist)
        acc = jnp.zeros(dist.shape, F32)
        for bk in range(REL_BUCKETS - 1):
            per_row = jnp.zeros((rows, 1), F32)
            for h in range(ATT_HEADS):
                per_row = jnp.where(row_h == h, rb_ref[bk, h] - rb_ref[REL_BUCKETS - 1, h], per_row)
            acc = jnp.where(bucket == bk, per_row, acc)
        return acc * LOG2E

    def attend_pages(k_ts, v_ts, mb, last_bias):
        logit = jnp.concatenate(
            [jnp.dot(qp, k_t, preferred_element_type=F32) for k_t in k_ts], axis=1) + mb
        n = len(k_ts)
        if n > 1:
            last_bias = jnp.concatenate(
                [jnp.zeros((rows, (n - 1) * PAGE_SIZE), F32), last_bias], axis=1)
        logit = logit + last_bias
        m_old = m_ref[...]
        m_new = jnp.maximum(m_old, jnp.max(logit, axis=1, keepdims=True))
        alpha = jnp.exp2(m_old - m_new)
        p = jnp.exp2(logit - m_new).astype(BF16)
        l_ref[...] = alpha * l_ref[...] + jnp.sum(p.astype(F32), axis=1, keepdims=True)
        acc = alpha * acc_ref[...]
        for m, v_t in enumerate(v_ts):
            acc = acc + lax.dot_general(p[:, m * PAGE_SIZE:(m + 1) * PAGE_SIZE], v_t, NT_DIMS,
                                        preferred_element_type=F32)
        acc_ref[...] = acc
        m_ref[...] = m_new

    is_last = j == n_steps - 1
    off = lax.broadcasted_iota(jnp.int32, (rows, PAGE_SIZE), 1)
    k0 = pl.multiple_of(j * (ATTN_PAGES * PAGE_SIZE), ATTN_PAGES * PAGE_SIZE)
    last_bias = lax.cond(is_last, lambda: near_bias(PAGE_SIZE + row_t - off),
                         lambda: jnp.zeros((rows, PAGE_SIZE), F32))
    attend_pages([kp[0].astype(BF16) for kp in kpages], [vp[0].astype(BF16) for vp in vpages],
                 mb_ref[:, pl.ds(k0, ATTN_PAGES * PAGE_SIZE)], last_bias)

    @pl.when(is_last)
    def _():
        attend_pages([knew_ref[0]], [vnew_ref[0]], mb_ref[:, past:past + PAGE_SIZE],
                     near_bias(row_t - off))
        res = acc_ref[...] / l_ref[...]
        lane_h = lax.broadcasted_iota(jnp.int32, (rows, ATT_WIDTH), 1) // ATT_HEAD_DIM
        res = jnp.where(lane_h == row_h, res, 0.0)
        o_ref[0] = _head_sums(res, dec_seq).astype(BF16)


def _sample_attention(page_table, rel_bias, mask, qp_rows, knew_t, vnew_t, cache_k_t, cache_v_t,
                      dec_seq):
    nseq, n_pages = page_table.shape
    past = n_pages * PAGE_SIZE
    steps = n_pages // ATTN_PAGES
    rows = dec_seq * ATT_HEADS

    def page_spec(m):
        return pl.BlockSpec((1, ATT_WIDTH, PAGE_SIZE),
                            lambda b, j, pt: (pt[b, j * ATTN_PAGES + m], 0, 0))

    per_seq = lambda shape: pl.BlockSpec((1,) + shape, lambda b, j, pt: (b, 0, 0))
    grid_spec = pltpu.PrefetchScalarGridSpec(
        num_scalar_prefetch=1,
        grid=(nseq, steps),
        in_specs=[pl.BlockSpec(memory_space=pltpu.SMEM),
                  per_seq((dec_seq, past + K_CHUNK)), per_seq((rows, ATT_WIDTH)),
                  per_seq((ATT_WIDTH, PAGE_SIZE)), per_seq((ATT_WIDTH, PAGE_SIZE))]
                 + [page_spec(m) for m in range(ATTN_PAGES)] * 2,
        out_specs=per_seq((dec_seq, ATT_WIDTH)),
        scratch_shapes=[
            pltpu.VMEM((rows, past + K_CHUNK), F32),
            pltpu.VMEM((rows, 1), F32),
            pltpu.VMEM((rows, 1), F32),
            pltpu.VMEM((rows, ATT_WIDTH), F32),
        ],
    )
    return pl.pallas_call(
        functools.partial(_sattn_body, dec_seq=dec_seq, past=past),
        grid_spec=grid_spec,
        out_shape=jax.ShapeDtypeStruct((nseq, dec_seq, ATT_WIDTH), BF16),
        compiler_params=pltpu.CompilerParams(dimension_semantics=("arbitrary", "arbitrary"),
                                             vmem_limit_bytes=VMEM_LIMIT),
        name="sample_attention",
    )(page_table, rel_bias, mask, qp_rows, knew_t, vnew_t,
      *([cache_k_t] * ATTN_PAGES), *([cache_v_t] * ATTN_PAGES))


FF_CHUNK = 256


def _post_body(att_ref, gla_ref, x_ref, p_ref, wo_ref, gpm_ref, gpf_ref, wg_ref, wu_ref, wd_ref,
               gpo_ref, wpg_ref, wpp_ref, o_ref, *, d_ff):
    wo = wo_ref
    mixed = jnp.dot(att_ref[...], wo[0:ATT_WIDTH, :], preferred_element_type=F32)
    mixed = mixed + jnp.dot(gla_ref[...], wo[ATT_WIDTH:ATT_WIDTH + GLA_WIDTH, :],
                            preferred_element_type=F32)
    h = x_ref[...] + _rms(mixed, gpm_ref[...])
    f = _rms(h, gpf_ref[...]).astype(BF16)
    ff = jnp.zeros(h.shape, F32)
    for c in range(d_ff // FF_CHUNK):
        cs = slice(c * FF_CHUNK, (c + 1) * FF_CHUNK)
        gate = jnp.dot(f, wg_ref[:, cs], preferred_element_type=F32)
        up = jnp.dot(f, wu_ref[:, cs], preferred_element_type=F32)
        act = (gate * jax.nn.sigmoid(gate) * up).astype(BF16)
        ff = ff + jnp.dot(act, wd_ref[cs, :], preferred_element_type=F32)
    h = h + _rms(ff, gpo_ref[...])
    gate = jax.nn.sigmoid(jnp.dot(h.astype(BF16), wpg_ref[...], preferred_element_type=F32))
    ple = jnp.dot(p_ref[...].astype(BF16), wpp_ref[...], preferred_element_type=F32)
    o_ref[...] = h + gate * ple


def _post(att, gla, x, p, wo, gpm, gpf, wg, wu, wd, gpo, wpg, wpp, tm):
    t = x.shape[0]
    d_ff = wg.shape[1]
    row = lambda w: pl.BlockSpec((tm, w), lambda i: (i, 0))
    const = lambda a: pl.BlockSpec(a.shape, lambda i: (0, 0), pipeline_mode=pl.Buffered(1))
    return pl.pallas_call(
        functools.partial(_post_body, d_ff=d_ff),
        grid=(t // tm,),
        in_specs=[row(ATT_WIDTH), row(GLA_WIDTH), row(D_MODEL), row(p.shape[1]),
                  const(wo), const(gpm), const(gpf), const(wg), const(wu), const(wd),
                  const(gpo), const(wpg), const(wpp)],
        out_specs=row(D_MODEL),
        out_shape=jax.ShapeDtypeStruct((t, D_MODEL), F32),
        compiler_params=pltpu.CompilerParams(dimension_semantics=("arbitrary",),
                                             vmem_limit_bytes=VMEM_LIMIT),
        name="post",
    )(att, gla, x, p, wo, gpm, gpf, wg, wu, wd, gpo, wpg, wpp)


def _pack_w_in(w_in):
    offs = [0]
    for wdt in (ATT_WIDTH, ATT_WIDTH, ATT_WIDTH, IDX_HEADS * IDX_DIM, IDX_DIM, IDX_HEADS,
                GLA_KEY_WIDTH, GLA_KEY_WIDTH, GLA_WIDTH, GLA_GATE_RANK, GLA_WIDTH):
        offs.append(offs[-1] + wdt)
    aq, ak, av, iq, ik, iw, gq, gk, gv, glr, gout = [w_in[:, offs[n]:offs[n + 1]] for n in range(11)]
    d = w_in.shape[0]
    zeros = lambda n: jnp.zeros((d, n), w_in.dtype)
    packed = jnp.concatenate([
        aq * (ATT_HEAD_DIM ** -0.5 * LOG2E), ak, av, iq * (IDX_DIM ** -0.5),
        ik, iw, zeros(LANE - IDX_DIM - IDX_HEADS),
        gq * (GLA_DK ** -0.5), gk, gv, glr, zeros(LANE - GLA_GATE_RANK), gout], axis=1)
    return packed.astype(BF16)


def _state_to_t(s):
    n = s.shape[0]
    return jnp.transpose(s, (0, 3, 1, 2)).reshape(n, GLA_DV, GLA_KEY_WIDTH)


def _state_from_t(st):
    n = st.shape[0]
    return jnp.transpose(st.reshape(n, GLA_DV, GLA_HEADS, GLA_DK), (0, 2, 3, 1))


def _layer(x_prompt, x_sample, p_prompt, p_sample, cache_k, cache_v, cache_idx_k, state_gla,
           page_table, rel_bias, g_pre_mix, w_in, w_gate_up, b_gate, g_gla_out, w_out,
           g_post_mix, g_pre_ffn, w_ff_gate, w_ff_up, w_ff_down, g_post_ffn, w_ple_gate,
           w_ple_proj):
    bsz, seq, d = x_prompt.shape
    nseq, dec_seq, _ = x_sample.shape
    n_pool = cache_k.shape[0]
    past = page_table.shape[1] * PAGE_SIZE
    row2 = lambda g: g.reshape(1, -1)

    w_packed = _pack_w_in(w_in)
    w_t = jnp.transpose(w_packed[:, SEG_K[0]:SEG_IKW[0] + IDX_DIM])
    w_t = jnp.concatenate([w_t[0:2 * ATT_WIDTH], w_t[SEG_IKW[0] - SEG_K[0]:]], axis=0)
    wgu = jnp.concatenate(
        [w_gate_up, jnp.zeros((LANE - GLA_GATE_RANK, GLA_KEY_WIDTH), w_gate_up.dtype)],
        axis=0).astype(BF16)
    post_w = (w_out.astype(BF16), row2(g_post_mix), row2(g_pre_ffn), w_ff_gate.astype(BF16),
              w_ff_up.astype(BF16), w_ff_down.astype(BF16), row2(g_post_ffn),
              w_ple_gate.astype(BF16), w_ple_proj.astype(BF16))
    gla_w = (wgu, row2(b_gate), row2(g_gla_out))

    tp = bsz * seq
    q, kb, iq, ikw, ikb, gin, k_t, v_t, vtb, ik_t = _inproj(
        x_prompt.reshape(tp, d), row2(g_pre_mix), w_packed, w_t, min(512, seq), seq)
    r3 = lambda a: a.reshape(bsz, seq, a.shape[-1])
    att = _prompt_attention(rel_bias, r3(q), r3(iq), r3(ikw), r3(kb), vtb, r3(ikb),
                            min(TOPK_MAX, seq // 4))
    gla_chunk = min(128, seq)
    gla_o, st_p = _gla(r3(gin), jnp.zeros((bsz, GLA_DV, GLA_KEY_WIDTH), F32), *gla_w,
                       tile=min(512, seq), chunk=gla_chunk, n_valid=gla_chunk)
    y_p = _post(att.reshape(tp, ATT_WIDTH), gla_o.reshape(tp, GLA_WIDTH), x_prompt.reshape(tp, d),
                p_prompt.reshape(tp, -1), *post_w, min(512, tp))
    heads_last = lambda a: jnp.transpose(
        a.reshape(bsz, ATT_HEADS, ATT_HEAD_DIM, seq), (0, 3, 1, 2))
    outs_p = (y_p.reshape(bsz, seq, d), heads_last(k_t), heads_last(v_t),
              jnp.swapaxes(ik_t, 1, 2), _state_from_t(st_p))

    ts = nseq * dec_seq
    q, kb, iq, ikw, ikb, gin, kf, vf, vb = _inproj(
        x_sample.reshape(ts, d), row2(g_pre_mix), w_packed, None, min(512, ts), ts)
    rows = dec_seq * ATT_HEADS
    iq_rows = iq.reshape(nseq, rows, IDX_DIM)
    w_rows = ikw[:, IDX_DIM:IDX_DIM + IDX_HEADS].reshape(nseq, rows, 1)
    head_of_col = jnp.arange(ATT_WIDTH, dtype=jnp.int32) // ATT_HEAD_DIM
    head_mask = head_of_col[None, :] == jnp.arange(ATT_HEADS, dtype=jnp.int32)[:, None]
    qp_rows = jnp.where(head_mask[None, None], q.reshape(nseq, dec_seq, 1, ATT_WIDTH),
                        jnp.zeros((), BF16)).reshape(nseq, rows, ATT_WIDTH)
    per_seq = lambda a: a.reshape(nseq, dec_seq, a.shape[-1])
    new_page_t = lambda a: jnp.pad(jnp.swapaxes(per_seq(a), 1, 2),
                                   ((0, 0), (0, 0), (0, PAGE_SIZE - dec_seq)))
    cache_idx_t = jnp.swapaxes(cache_idx_k, 1, 2)
    cache_k_t = jnp.transpose(cache_k, (0, 2, 3, 1)).reshape(n_pool, ATT_WIDTH, PAGE_SIZE)
    cache_v_t = jnp.transpose(cache_v, (0, 2, 3, 1)).reshape(n_pool, ATT_WIDTH, PAGE_SIZE)
    sc_past, sc_new = _sample_scores(page_table, iq_rows, w_rows, new_page_t(ikb[:, :IDX_DIM]),
                                     cache_idx_t, dec_seq)
    scores_t = jnp.transpose(jnp.concatenate([sc_past, sc_new], axis=2).reshape(ts, -1))
    mask_t = _sample_topk(scores_t, dec_seq, past, min(TOPK_MAX, (past + dec_seq) // 4))
    mask = jnp.transpose(mask_t).reshape(nseq, dec_seq, -1)
    att_s = _sample_attention(page_table, rel_bias, mask, qp_rows, new_page_t(kb),
                              new_page_t(vb), cache_k_t, cache_v_t, dec_seq)
    gin_s = jnp.pad(per_seq(gin), ((0, 0), (0, NEW_PAD - dec_seq), (0, 0)))
    gla_s, st_s = _gla(gin_s, _state_to_t(state_gla), *gla_w,
                       tile=NEW_PAD, chunk=NEW_PAD, n_valid=dec_seq)
    y_s = _post(att_s.reshape(ts, ATT_WIDTH), gla_s[:, :dec_seq].reshape(ts, GLA_WIDTH),
                x_sample.reshape(ts, d), p_sample.reshape(ts, -1), *post_w, min(256, ts))
    outs_s = (y_s.reshape(nseq, dec_seq, d),
              kf.reshape(nseq, dec_seq, ATT_HEADS, ATT_HEAD_DIM),
              vf.reshape(nseq, dec_seq, ATT_HEADS, ATT_HEAD_DIM),
              ikw[:, :IDX_DIM].reshape(nseq, dec_seq, IDX_DIM),
              _state_from_t(st_s))
    return outs_p, outs_s


def kernel(x_prompt, x_sample, p_prompt, p_sample, cache_k, cache_v, cache_idx_k, state_gla,
           page_table, rel_bias, g_pre_mix, w_in, w_gate_up, b_gate, g_gla_out, w_out,
           g_post_mix, g_pre_ffn, w_ff_gate, w_ff_up, w_ff_down, g_post_ffn, w_ple_gate,
           w_ple_proj):
    depth = w_in.shape[0]
    h_p, h_s = x_prompt, x_sample
    per_layer = []
    for i in range(depth):
        outs_p, outs_s = _layer(
            h_p, h_s, p_prompt[i], p_sample[i], cache_k[i], cache_v[i], cache_idx_k[i],
            state_gla[i], page_table, rel_bias, g_pre_mix[i], w_in[i], w_gate_up[i], b_gate[i],
            g_gla_out[i], w_out[i], g_post_mix[i], g_pre_ffn[i], w_ff_gate[i], w_ff_up[i],
            w_ff_down[i], g_post_ffn[i], w_ple_gate[i], w_ple_proj[i])
        h_p, h_s = outs_p[0], outs_s[0]
        per_layer.append(outs_p[1:] + outs_s[1:])
    stacked = [jnp.stack([lay[n] for lay in per_layer]) for n in range(8)]
    return (h_p, h_s, stacked[0], stacked[1], stacked[2], stacked[3],
            stacked[4], stacked[5], stacked[6], stacked[7])
```

```python
import functools
import math

import jax
import jax.numpy as jnp
from jax import lax
from jax.experimental import pallas as pl
from jax.experimental.pallas import tpu as pltpu

F32 = jnp.float32
BF16 = jnp.bfloat16

D_MODEL = 1024
ATT_HEADS = 8
ATT_HEAD_DIM = 64
ATT_WIDTH = ATT_HEADS * ATT_HEAD_DIM
IDX_HEADS = 8
IDX_DIM = 64
TOPK_MAX = 256
PAGE_SIZE = 128
GLA_HEADS = 4
GLA_DK = 64
GLA_DV = 128
GLA_KEY_WIDTH = GLA_HEADS * GLA_DK
GLA_WIDTH = GLA_HEADS * GLA_DV
GLA_GATE_RANK = 16
GLA_GATE_TAU = 16.0
REL_BUCKETS = 32
REL_MAX_DIST = 128
RMS_EPS = 1e-6
LOG2E = math.log2(math.e)

LANE = 128
SUBLANE = 8
BF16_ROWS = 16
BF16_ULP_BOUND = 2.0 ** -7
BF16_EXACT_INT = 256
MXU_WIDTH = 256
Q_TILE = MXU_WIDTH
K_CHUNK = MXU_WIDTH
HEAD_GROUP = MXU_WIDTH // ATT_HEAD_DIM
ACC_ROWS = ATT_HEAD_DIM + BF16_ROWS
VMEM_LIMIT = 56 * 1024 * 1024
COARSE_ITERS = 10
FINE_ITERS = 9

SEG_Q = (0, 512)
SEG_K = (512, 1024)
SEG_V = (1024, 1536)
SEG_IQ = (1536, 2048)
SEG_IKW = (2048, 2176)
SEG_GLA = (2176, 3840)
PROJ_PACKED = 3840
GLA_IN_WIDTH = SEG_GLA[1] - SEG_GLA[0]

NT_DIMS = (((1,), (1,)), ((), ()))
TN_DIMS = (((0,), (0,)), ((), ()))


def _rms(x, g):
    return x * lax.rsqrt(jnp.mean(x * x, axis=-1, keepdims=True) + RMS_EPS) * g


def _inproj_body(x_ref, g_ref, w_ref, *rest, tokens_minor):
    a = _rms(x_ref[...], g_ref[...]).astype(BF16)

    def proj(seg):
        return jnp.dot(a, w_ref[:, seg[0]:seg[1]], preferred_element_type=F32)

    if tokens_minor:
        wt_ref, q_ref, kb_ref, iq_ref, ikw_ref, ikb_ref, gla_ref, kt_ref, vt_ref, vtb_ref, ikt_ref = rest
    else:
        q_ref, kb_ref, iq_ref, ikw_ref, ikb_ref, gla_ref, k_ref, v_ref, vb_ref = rest

    q_ref[...] = proj(SEG_Q).astype(BF16)
    kf = proj(SEG_K)
    kb_ref[...] = kf.astype(BF16)
    iq_ref[...] = proj(SEG_IQ).astype(BF16)
    lane = lax.broadcasted_iota(jnp.int32, (1, LANE), 1)
    is_w = (lane >= IDX_DIM) & (lane < IDX_DIM + IDX_HEADS)
    ikw = proj(SEG_IKW) * jnp.where(is_w, IDX_HEADS ** -0.5, 1.0)
    ikw_ref[...] = ikw
    ikb_ref[...] = ikw.astype(BF16)
    gla_ref[...] = proj(SEG_GLA)
    if tokens_minor:
        def proj_t(lo, hi):
            return lax.dot_general(wt_ref[lo:hi, :], a, NT_DIMS, preferred_element_type=F32)

        kt_ref[0] = proj_t(0, ATT_WIDTH)
        vt = proj_t(ATT_WIDTH, 2 * ATT_WIDTH)
        vt_ref[0] = vt
        vtb_ref[0] = vt.astype(BF16)
        ikt_ref[0] = proj_t(2 * ATT_WIDTH, 2 * ATT_WIDTH + IDX_DIM)
    else:
        k_ref[...] = kf
        vf = proj(SEG_V)
        v_ref[...] = vf
        vb_ref[...] = vf.astype(BF16)


def _inproj(x, g, w_packed, w_t, tm, seq):
    t = x.shape[0]
    tokens_minor = w_t is not None
    tiles_per_seq = seq // tm
    row = lambda w: pl.BlockSpec((tm, w), lambda i: (i, 0))
    const = lambda s: pl.BlockSpec(s, lambda i: (0, 0))
    col = lambda r: pl.BlockSpec((1, r, tm), lambda i: (i // tiles_per_seq, 0, i % tiles_per_seq))
    rows_shape = lambda w, dt: jax.ShapeDtypeStruct((t, w), dt)
    cols_shape = lambda r, dt: jax.ShapeDtypeStruct((t // seq, r, seq), dt)
    out_shape = [rows_shape(ATT_WIDTH, BF16), rows_shape(ATT_WIDTH, BF16),
                 rows_shape(ATT_WIDTH, BF16), rows_shape(LANE, F32), rows_shape(LANE, BF16),
                 rows_shape(GLA_IN_WIDTH, F32)]
    out_specs = [row(ATT_WIDTH), row(ATT_WIDTH), row(ATT_WIDTH), row(LANE), row(LANE),
                 row(GLA_IN_WIDTH)]
    in_specs = [row(D_MODEL), const((1, D_MODEL)), const((D_MODEL, PROJ_PACKED))]
    operands = [x, g, w_packed]
    if tokens_minor:
        in_specs.append(const(w_t.shape))
        operands.append(w_t)
        out_shape += [cols_shape(ATT_WIDTH, F32), cols_shape(ATT_WIDTH, F32),
                      cols_shape(ATT_WIDTH, BF16), cols_shape(IDX_DIM, F32)]
        out_specs += [col(ATT_WIDTH), col(ATT_WIDTH), col(ATT_WIDTH), col(IDX_DIM)]
    else:
        out_shape += [rows_shape(ATT_WIDTH, F32), rows_shape(ATT_WIDTH, F32),
                      rows_shape(ATT_WIDTH, BF16)]
        out_specs += [row(ATT_WIDTH), row(ATT_WIDTH), row(ATT_WIDTH)]
    return pl.pallas_call(
        functools.partial(_inproj_body, tokens_minor=tokens_minor),
        grid=(t // tm,),
        in_specs=in_specs,
        out_specs=tuple(out_specs),
        out_shape=tuple(out_shape),
        compiler_params=pltpu.CompilerParams(dimension_semantics=("arbitrary",),
                                             vmem_limit_bytes=VMEM_LIMIT),
        name="inproj",
    )(*operands)


def _log_sigmoid(z):
    return jnp.minimum(z, 0.0) - jnp.log1p(jnp.exp(-jnp.abs(z)))


def _gla_body(gin_ref, s0_ref, wgu_ref, bg_ref, gg_ref, o_ref, sout_ref, st_ref, *, chunk, n_valid):
    @pl.when(pl.program_id(1) == 0)
    def _():
        st_ref[...] = s0_ref[0]

    n_chunks = gin_ref.shape[1] // chunk
    ri = lax.broadcasted_iota(jnp.int32, (chunk, chunk), 0)
    ci = lax.broadcasted_iota(jnp.int32, (chunk, chunk), 1)
    causal = ri >= ci
    tri = causal.astype(F32)
    row_valid = lax.broadcasted_iota(jnp.int32, (chunk, 1), 0) < n_valid

    def one_chunk(c, carry):
        r0 = pl.multiple_of(c * chunk, chunk)
        blk = gin_ref[0, pl.ds(r0, chunk), :]
        gq = blk[:, 0:256]
        gk = blk[:, 256:512]
        gv = blk[:, 512:1024]
        glr = blk[:, 1024:1152]
        gout = blk[:, 1152:1664]
        z = jnp.dot(glr.astype(BF16), wgu_ref[...], preferred_element_type=F32) + bg_ref[...]
        la = _log_sigmoid(z) / GLA_GATE_TAU
        if n_valid < chunk:
            la = jnp.where(row_valid, la, 0.0)
        b = jnp.dot(tri, la, precision=lax.Precision.HIGHEST, preferred_element_type=F32)
        b_last = b[chunk - 1:chunk, :]
        qt = gq * jnp.exp(b)
        kt = gk * jnp.exp(-b)
        kd = gk * jnp.exp(b_last - b)
        dec = jnp.exp(b_last)
        st = st_ref[...]
        o_parts = []
        st_parts = []
        for h in range(GLA_HEADS):
            ks = slice(GLA_DK * h, GLA_DK * (h + 1))
            vs = slice(GLA_DV * h, GLA_DV * (h + 1))
            qh = qt[:, ks].astype(BF16)
            kh = kt[:, ks].astype(BF16)
            kdh = kd[:, ks].astype(BF16)
            vh = gv[:, vs].astype(BF16)
            sth = st[:, ks]
            attn = lax.dot_general(qh, kh, NT_DIMS, preferred_element_type=F32)
            attn = jnp.where(causal, attn, 0.0)
            o = jnp.dot(attn.astype(BF16), vh, preferred_element_type=F32)
            o = o + lax.dot_general(qh, sth.astype(BF16), NT_DIMS, preferred_element_type=F32)
            upd = lax.dot_general(vh, kdh, TN_DIMS, preferred_element_type=F32)
            st_parts.append(dec[:, ks] * sth + upd)
            o_parts.append(_rms(o, gg_ref[...]))
        st_ref[...] = jnp.concatenate(st_parts, axis=1)
        o_all = jnp.concatenate(o_parts, axis=1)
        gate = gout * jax.nn.sigmoid(gout)
        o_ref[0, pl.ds(r0, chunk), :] = (o_all * gate).astype(BF16)
        return carry

    lax.fori_loop(0, n_chunks, one_chunk, 0)
    sout_ref[0] = st_ref[...]


def _gla(gin, s0_t, wgu, bg, gg, *, tile, chunk, n_valid):
    nseq, t, _ = gin.shape
    body = functools.partial(_gla_body, chunk=chunk, n_valid=n_valid)
    return pl.pallas_call(
        body,
        grid=(nseq, t // tile),
        in_specs=[
            pl.BlockSpec((1, tile, GLA_IN_WIDTH), lambda b, j: (b, j, 0)),
            pl.BlockSpec((1, GLA_DV, GLA_KEY_WIDTH), lambda b, j: (b, 0, 0)),
            pl.BlockSpec((LANE, GLA_KEY_WIDTH), lambda b, j: (0, 0)),
            pl.BlockSpec((1, GLA_KEY_WIDTH), lambda b, j: (0, 0)),
            pl.BlockSpec((1, GLA_DV), lambda b, j: (0, 0)),
        ],
        out_specs=(
            pl.BlockSpec((1, tile, GLA_WIDTH), lambda b, j: (b, j, 0)),
            pl.BlockSpec((1, GLA_DV, GLA_KEY_WIDTH), lambda b, j: (b, 0, 0)),
        ),
        out_shape=(
            jax.ShapeDtypeStruct((nseq, t, GLA_WIDTH), BF16),
            jax.ShapeDtypeStruct((nseq, GLA_DV, GLA_KEY_WIDTH), F32),
        ),
        scratch_shapes=[pltpu.VMEM((GLA_DV, GLA_KEY_WIDTH), F32)],
        compiler_params=pltpu.CompilerParams(dimension_semantics=("arbitrary", "arbitrary"),
                                             vmem_limit_bytes=VMEM_LIMIT),
        name="gla",
    )(gin, s0_t, wgu, bg, gg)


def _rel_bucket(dist):
    n = jnp.maximum(dist, 0)
    exact = REL_BUCKETS // 2
    nf = jnp.maximum(n, 1).astype(F32)
    large = exact + (jnp.log(nf / exact) / math.log(REL_MAX_DIST / exact)
                     * (REL_BUCKETS - exact)).astype(jnp.int32)
    large = jnp.minimum(large, REL_BUCKETS - 1)
    return jnp.where(n < exact, n, large)


def _select_topk(sc_ref, n_queries, n_chunks, n_allowed, topk, keys_axis, max_iter=18,
                 coarse_ref=None, coarse_iter=0):
    neg_inf = -jnp.inf
    n_acc = 4 if keys_axis == 0 else K_CHUNK // LANE
    if keys_axis == 1:
        part_shape = (n_queries, LANE)
        q_shape = (n_queries, 1)
        chunk_shape = (n_queries, K_CHUNK)
        load = lambda k0: sc_ref[:, pl.ds(k0, K_CHUNK)]
        pieces = lambda x: [x[:, j * LANE:(j + 1) * LANE] for j in range(K_CHUNK // LANE)]
        last = lambda r: r[:, K_CHUNK - 1:K_CHUNK]
    else:
        part_shape = (SUBLANE, n_queries)
        q_shape = (1, n_queries)
        chunk_shape = (K_CHUNK, n_queries)
        load = lambda k0: sc_ref[pl.ds(k0, K_CHUNK), :]
        pieces = lambda x: [x[j * SUBLANE:(j + 1) * SUBLANE] for j in range(K_CHUNK // SUBLANE)]
        last = lambda r: r[K_CHUNK - 1:K_CHUNK, :]

    def store(k0, val):
        if keys_axis == 1:
            sc_ref[:, pl.ds(k0, K_CHUNK)] = val
        else:
            sc_ref[pl.ds(k0, K_CHUNK), :] = val

    def sweep(fns, inits, reds, combines):
        n_stat = len(fns)

        def body(c, accs):
            x = load(pl.multiple_of(c * K_CHUNK, K_CHUNK))
            accs = [list(a) for a in accs]
            for n, piece in enumerate(pieces(x)):
                for s in range(n_stat):
                    accs[s][n % n_acc] = fns[s](accs[s][n % n_acc], piece)
            return tuple(tuple(a) for a in accs)

        init = tuple(tuple(jnp.full(part_shape, inits[s], F32) for _ in range(n_acc))
                     for s in range(n_stat))
        accs = lax.fori_loop(0, n_chunks, body, init)
        out = []
        for s in range(n_stat):
            acc = accs[s][0]
            for other in accs[s][1:]:
                acc = combines[s](acc, other)
            out.append(reds[s](acc, axis=keys_axis, keepdims=True))
        return out

    def count_ge(t):
        tb = jnp.broadcast_to(t, part_shape)
        return sweep([lambda a, x: jnp.where(x >= tb, a + 1.0, a)], [0.0], [jnp.sum],
                     [jnp.add])[0]

    def count_gt(t):
        tb = jnp.broadcast_to(t, part_shape)
        return sweep([lambda a, x: a + jnp.where(x > tb, 1.0, 0.0)], [0.0], [jnp.sum],
                     [jnp.add])[0]

    def max_below(t):
        tb = jnp.broadcast_to(t, part_shape)
        return sweep([lambda a, x: jnp.maximum(a, jnp.where(x < tb, x, neg_inf))], [neg_inf],
                     [jnp.max], [jnp.maximum])[0]

    row_max, row_min = sweep(
        [jnp.maximum, lambda a, x: jnp.minimum(a, jnp.where(x == neg_inf, jnp.inf, x))],
        [neg_inf, jnp.inf], [jnp.max, jnp.min], [jnp.maximum, jnp.minimum])

    k = float(topk)
    take_all = n_allowed <= k
    lo0 = row_min
    hi0 = row_max + jnp.maximum(jnp.abs(row_max) * 1e-6, 1e-30)
    t0 = jnp.full(q_shape, jnp.finfo(F32).min, F32)
    done0 = jnp.where(take_all, 1.0, 0.0)

    if coarse_ref is not None:
        rows_b = BF16_ROWS
        one_b = jnp.ones((rows_b, n_queries), BF16)
        zero_b = jnp.zeros((rows_b, n_queries), BF16)

        def count_b(m):
            mb_ = jnp.broadcast_to(m.astype(BF16), (rows_b, n_queries))

            def body(c, accs):
                x = coarse_ref[pl.ds(pl.multiple_of(c * K_CHUNK, K_CHUNK), K_CHUNK), :]
                accs = list(accs)
                for n in range(K_CHUNK // rows_b):
                    piece = x[n * rows_b:(n + 1) * rows_b]
                    accs[n % n_acc] = accs[n % n_acc] + jnp.where(piece >= mb_, one_b, zero_b)
                return tuple(accs)

            accs = lax.fori_loop(0, n_chunks, body, tuple(zero_b for _ in range(n_acc)))
            total = accs[0].astype(F32)
            for other in accs[1:]:
                total = total + other.astype(F32)
            return jnp.sum(total, axis=0, keepdims=True)

        def coarse_body(_, st):
            lo, hi = st
            m = (lo + (hi - lo) * 0.5).astype(BF16).astype(F32)
            inside = (m > lo) & (m < hi)
            c = count_b(m)
            below = m - (jnp.abs(m) * BF16_ULP_BOUND + 1e-30)
            hi = jnp.where(inside & (c < k), m, hi)
            lo = jnp.where(inside & (c >= k) & (below > lo), below, lo)
            return lo, hi

        lo0, hi0 = lax.fori_loop(0, coarse_iter, coarse_body, (lo0, hi0))

    def bis_body(_, st):
        lo, hi, t, done = st
        mid = lo + (hi - lo) * 0.5
        c = count_ge(mid)
        hit = (c == k) & (done < 0.5)
        t = jnp.where(hit, mid, t)
        done = jnp.where(hit, 1.0, done)
        lo = jnp.where(c > k, mid, lo)
        hi = jnp.where(c < k, mid, hi)
        return lo, hi, t, done

    lo, hi, t, done = lax.fori_loop(0, max_iter, bis_body, (lo0, hi0, t0, done0))

    all_done = jnp.min(done) > 0.5

    def write(fn):
        def body(c, carry):
            k0 = pl.multiple_of(c * K_CHUNK, K_CHUNK)
            sel, carry = fn(load(k0), carry)
            store(k0, jnp.where(sel, 0.0, neg_inf))
            return carry
        return body

    def walk_down():
        def snap_cond(st):
            return jnp.min(st[2]) < 0.5

        def snap_body(st):
            hi_, t_, done_, over_ = st
            v = max_below(hi_)
            c = count_ge(v)
            ok = (c >= k) & (done_ < 0.5)
            t_ = jnp.where(ok, v, t_)
            over_ = jnp.where(ok, c - k, over_)
            hi_ = jnp.where((done_ < 0.5) & jnp.logical_not(ok), v, hi_)
            done_ = jnp.where(ok, 1.0, done_)
            return hi_, t_, done_, over_

        _, t_, _, over_ = lax.while_loop(snap_cond, snap_body,
                                         (hi, t, done, jnp.zeros(q_shape, F32)))
        return t_, over_

    tv, over = lax.cond(all_done, lambda: (t, jnp.zeros(q_shape, F32)), walk_down)
    has_ties = jnp.max(over) > 0.5

    @pl.when(jnp.logical_not(has_ties))
    def _():
        tb = jnp.broadcast_to(tv, chunk_shape)
        lax.fori_loop(0, n_chunks, write(lambda x, cr: (x >= tb, cr)), 0)

    @pl.when(has_ties)
    def _():
        quota = k - count_gt(tv)
        tb = jnp.broadcast_to(tv, chunk_shape)
        qb = jnp.broadcast_to(quota, chunk_shape)
        ui = lax.broadcasted_iota(jnp.int32, (K_CHUNK, K_CHUNK), 0)
        uj = lax.broadcasted_iota(jnp.int32, (K_CHUNK, K_CHUNK), 1)
        prefix = ((ui <= uj) if keys_axis == 1 else (ui >= uj)).astype(BF16)

        def tie_fn(x, seen):
            tie = x == tb
            if keys_axis == 1:
                rank = jnp.dot(tie.astype(BF16), prefix, preferred_element_type=F32) + seen
            else:
                rank = jnp.dot(prefix, tie.astype(BF16), preferred_element_type=F32) + seen
            sel = (x > tb) | (tie & (rank <= qb))
            return sel, last(rank)

        lax.fori_loop(0, n_chunks, write(tie_fn), jnp.zeros(q_shape, F32))


def _head_lane_mask(h, width):
    lane = lax.broadcasted_iota(jnp.int32, (1, width), 1)
    lo = (h % HEAD_GROUP) * ATT_HEAD_DIM
    return (lane >= lo) & (lane < lo + ATT_HEAD_DIM)


def _pattn_body(rb_ref, q_ref, iq_ref, w_ref, k_ref, vt_ref, ik_ref, o_ref,
                sc_ref, scb_ref, qp_ref, iqp_ref, bias_ref, m_ref, alpha_ref, acc_ref, x_ref,
                p_ref, *, topk):
    i = pl.program_id(1)
    n_chunks = i + 1

    @pl.when((pl.program_id(0) == 0) & (i == 0))
    def _():
        kk = lax.broadcasted_iota(jnp.int32, (K_CHUNK, Q_TILE), 0)
        qq = lax.broadcasted_iota(jnp.int32, (K_CHUNK, Q_TILE), 1)
        for delta in range(2):
            bucket = _rel_bucket(qq - kk + Q_TILE * delta)
            for h in range(ATT_HEADS):
                acc = jnp.zeros((K_CHUNK, Q_TILE), F32)
                for bk in range(REL_BUCKETS - 1):
                    acc = jnp.where(bucket == bk, rb_ref[bk, h] - rb_ref[REL_BUCKETS - 1, h], acc)
                bias_ref[delta, h] = acc * LOG2E
        bias_ref[2] = jnp.zeros(bias_ref.shape[1:], F32)

    q = q_ref[0]
    iq = iq_ref[0]
    for h in range(ATT_HEADS):
        g = h // HEAD_GROUP
        qp_ref[h] = jnp.where(_head_lane_mask(h, MXU_WIDTH),
                              q[:, MXU_WIDTH * g:MXU_WIDTH * (g + 1)], 0)
        iqp_ref[h] = jnp.concatenate(
            [iq[:, IDX_DIM * h:IDX_DIM * (h + 1)], jnp.zeros((Q_TILE, LANE - IDX_DIM), BF16)], axis=1)
    w_t = jnp.transpose(w_ref[0])[IDX_DIM:IDX_DIM + IDX_HEADS, :]

    q_pos = i * Q_TILE + lax.broadcasted_iota(jnp.int32, (1, Q_TILE), 1)

    def score_chunk(c, carry, *, diagonal):
        k0 = pl.multiple_of(c * K_CHUNK, K_CHUNK)
        ikc = ik_ref[0, pl.ds(k0, K_CHUNK), :]
        acc = jnp.zeros((K_CHUNK, Q_TILE), F32)
        for h in range(IDX_HEADS):
            s = lax.dot_general(ikc, iqp_ref[h], NT_DIMS, preferred_element_type=F32)
            acc = acc + jnp.maximum(s, 0.0) * w_t[h:h + 1, :]
        if diagonal:
            k_pos = k0 + lax.broadcasted_iota(jnp.int32, (K_CHUNK, 1), 0)
            acc = jnp.where(k_pos <= q_pos, acc, -jnp.inf)
        sc_ref[pl.ds(k0, K_CHUNK), :] = acc
        scb_ref[pl.ds(k0, K_CHUNK), :] = acc.astype(BF16)
        return carry

    def score_pair(a, carry):
        score_chunk(2 * a, carry, diagonal=False)
        return score_chunk(2 * a + 1, carry, diagonal=False)

    lax.fori_loop(0, i // 2, score_pair, 0)
    lax.fori_loop(2 * (i // 2), i, functools.partial(score_chunk, diagonal=False), 0)
    score_chunk(i, 0, diagonal=True)

    _select_topk(sc_ref, Q_TILE, n_chunks, (q_pos + 1).astype(F32), topk, keys_axis=0,
                 max_iter=FINE_ITERS, coarse_ref=scb_ref, coarse_iter=COARSE_ITERS)

    m_ref[...] = jnp.full(m_ref.shape, -1e30, F32)
    alpha_ref[...] = jnp.ones(alpha_ref.shape, F32)
    acc_ref[...] = jnp.zeros(acc_ref.shape, F32)
    p_ref[0] = jnp.zeros(p_ref.shape[1:], BF16)
    x_ref[1] = jnp.full(x_ref.shape[1:], -jnp.inf, F32)
    ones_rows = jnp.ones((BF16_ROWS, K_CHUNK), BF16)
    row_block = 4 * SUBLANE

    def logits(h, slot, k0, mb, near):
        g = h // HEAD_GROUP
        kc = k_ref[0, pl.ds(k0, K_CHUNK), MXU_WIDTH * g:MXU_WIDTH * (g + 1)]
        s = lax.dot_general(kc, qp_ref[h], NT_DIMS, preferred_element_type=F32)
        mx = jnp.full((SUBLANE, Q_TILE), -jnp.inf, F32)
        for r in range(K_CHUNK // row_block):
            rs = slice(r * row_block, (r + 1) * row_block)
            xb = s[rs] + mb[rs]
            if near is not None:
                xb = xb + bias_ref[near, h, rs, :]
            x_ref[slot, h, rs, :] = xb
            for u in range(row_block // SUBLANE):
                mx = jnp.maximum(mx, xb[u * SUBLANE:(u + 1) * SUBLANE])
        m_old = m_ref[1 - slot, h]
        m_new = jnp.maximum(m_old, jnp.max(mx, axis=0, keepdims=True))
        alpha_ref[slot, h] = jnp.exp2(m_old - m_new)
        m_ref[slot, h] = m_new

    def probs(h, slot):
        p_ref[slot, h] = jnp.exp2(x_ref[slot, h] - m_ref[slot, h]).astype(BF16)

    def accumulate(h, slot, k0):
        vt = vt_ref[0, ATT_HEAD_DIM * h:ATT_HEAD_DIM * (h + 1), pl.ds(k0, K_CHUNK)]
        lhs = jnp.concatenate([vt, ones_rows], axis=0)
        acc_ref[h] = alpha_ref[slot, h] * acc_ref[h] + jnp.dot(
            lhs, p_ref[slot, h], preferred_element_type=F32)

    def chunk_start(c):
        return pl.multiple_of(jnp.maximum(c, 0) * K_CHUNK, K_CHUNK)

    def one_step(t, slot, with_bias):
        k_new = chunk_start(t)
        k_old = chunk_start(t - 2)
        mb = sc_ref[pl.ds(k_new, K_CHUNK), :]
        near = jnp.minimum(i - t, 2) if with_bias else None
        for h in range(ATT_HEADS):
            accumulate(h, slot, k_old)
            probs(h, 1 - slot)
            logits(h, slot, k_new, mb, near)

    def two_steps(a, carry, *, with_bias):
        one_step(2 * a, 0, with_bias)
        one_step(2 * a + 1, 1, with_bias)
        return carry

    far_pairs = jnp.maximum(i - 1, 0) // 2
    n_pairs = n_chunks // 2
    lax.fori_loop(0, far_pairs, functools.partial(two_steps, with_bias=False), 0)
    lax.fori_loop(far_pairs, n_pairs, functools.partial(two_steps, with_bias=True), 0)

    @pl.when(n_chunks % 2 == 1)
    def _():
        one_step(i, 0, True)

    last = i % 2
    for h in range(ATT_HEADS):
        accumulate(h, 1 - last, chunk_start(i - 1))
        probs(h, last)
    for h in range(ATT_HEADS):
        accumulate(h, last, chunk_start(i))

    out_t = jnp.concatenate(
        [acc_ref[h, 0:ATT_HEAD_DIM, :] / acc_ref[h, ATT_HEAD_DIM:ATT_HEAD_DIM + 1, :]
         for h in range(ATT_HEADS)], axis=0)
    o_ref[0] = jnp.transpose(out_t).astype(BF16)


def _prompt_attention(rel_bias, q, iq, ikw, kb, vt, ikb, topk):
    bsz, s, _ = q.shape
    assert (s // K_CHUNK) * (K_CHUNK // BF16_ROWS // 4) <= BF16_EXACT_INT
    tile = lambda w: pl.BlockSpec((1, Q_TILE, w), lambda b, i: (b, i, 0))
    per_batch = lambda shape: pl.BlockSpec((1,) + shape, lambda b, i: (b, 0, 0),
                                           pipeline_mode=pl.Buffered(1))
    body = functools.partial(_pattn_body, topk=topk)
    return pl.pallas_call(
        body,
        grid=(bsz, s // Q_TILE),
        in_specs=[pl.BlockSpec(memory_space=pltpu.SMEM),
                  tile(ATT_WIDTH), tile(ATT_WIDTH), tile(LANE),
                  per_batch((s, ATT_WIDTH)), per_batch((ATT_WIDTH, s)), per_batch((s, LANE))],
        out_specs=tile(ATT_WIDTH),
        out_shape=jax.ShapeDtypeStruct((bsz, s, ATT_WIDTH), BF16),
        scratch_shapes=[
            pltpu.VMEM((s, Q_TILE), F32),
            pltpu.VMEM((s, Q_TILE), BF16),
            pltpu.VMEM((ATT_HEADS, Q_TILE, MXU_WIDTH), BF16),
            pltpu.VMEM((IDX_HEADS, Q_TILE, LANE), BF16),
            pltpu.VMEM((3, ATT_HEADS, K_CHUNK, Q_TILE), F32),
            pltpu.VMEM((2, ATT_HEADS, 1, Q_TILE), F32),
            pltpu.VMEM((2, ATT_HEADS, 1, Q_TILE), F32),
            pltpu.VMEM((ATT_HEADS, ACC_ROWS, Q_TILE), F32),
            pltpu.VMEM((2, ATT_HEADS, K_CHUNK, Q_TILE), F32),
            pltpu.VMEM((2, ATT_HEADS, K_CHUNK, Q_TILE), BF16),
        ],
        compiler_params=pltpu.CompilerParams(dimension_semantics=("arbitrary", "arbitrary"),
                                             vmem_limit_bytes=VMEM_LIMIT),
        name="prompt_attention",
    )(rel_bias, q, iq, ikw, kb, vt, ikb)


SCORE_PAGES = 64
ATTN_PAGES = 16
NEW_PAD = 16


def _head_sums(y, dec_seq):
    return jnp.concatenate(
        [jnp.sum(y[IDX_HEADS * t:IDX_HEADS * (t + 1)], axis=0, keepdims=True)
         for t in range(dec_seq)], axis=0)


def _sscore_body(pt_ref, iq_ref, w_ref, iknew_ref, *rest, dec_seq):
    del pt_ref
    pages = rest[:SCORE_PAGES]
    out_ref, new_ref = rest[SCORE_PAGES:]
    iq = iq_ref[0]
    w = w_ref[0]
    for m in range(SCORE_PAGES):
        s = jnp.dot(iq, pages[m][0].astype(BF16), preferred_element_type=F32)
        out_ref[0, :, m * PAGE_SIZE:(m + 1) * PAGE_SIZE] = _head_sums(jnp.maximum(s, 0.0) * w,
                                                                      dec_seq)

    @pl.when(pl.program_id(1) == 0)
    def _():
        s = jnp.dot(iq, iknew_ref[0], preferred_element_type=F32)
        new_sc = _head_sums(jnp.maximum(s, 0.0) * w, dec_seq)
        padded = jnp.concatenate(
            [new_sc, jnp.zeros((dec_seq, K_CHUNK - PAGE_SIZE), F32)], axis=1)
        t_idx = lax.broadcasted_iota(jnp.int32, (dec_seq, K_CHUNK), 0)
        k_idx = lax.broadcasted_iota(jnp.int32, (dec_seq, K_CHUNK), 1)
        new_ref[0] = jnp.where(k_idx <= t_idx, padded, -jnp.inf)


def _sample_scores(page_table, iq_rows, w_rows, iknew_t, cache_idx_t, dec_seq):
    nseq, n_pages = page_table.shape
    assert n_pages % SCORE_PAGES == 0
    steps = n_pages // SCORE_PAGES
    rows = dec_seq * IDX_HEADS

    def page_spec(m):
        return pl.BlockSpec((1, IDX_DIM, PAGE_SIZE),
                            lambda b, j, pt: (pt[b, j * SCORE_PAGES + m], 0, 0))

    per_seq = lambda shape: pl.BlockSpec((1,) + shape, lambda b, j, pt: (b, 0, 0))
    grid_spec = pltpu.PrefetchScalarGridSpec(
        num_scalar_prefetch=1,
        grid=(nseq, steps),
        in_specs=[per_seq((rows, IDX_DIM)), per_seq((rows, 1)), per_seq((IDX_DIM, PAGE_SIZE))]
                 + [page_spec(m) for m in range(SCORE_PAGES)],
        out_specs=(pl.BlockSpec((1, dec_seq, SCORE_PAGES * PAGE_SIZE),
                                lambda b, j, pt: (b, 0, j)),
                   per_seq((dec_seq, K_CHUNK))),
    )
    return pl.pallas_call(
        functools.partial(_sscore_body, dec_seq=dec_seq),
        grid_spec=grid_spec,
        out_shape=(jax.ShapeDtypeStruct((nseq, dec_seq, n_pages * PAGE_SIZE), F32),
                   jax.ShapeDtypeStruct((nseq, dec_seq, K_CHUNK), F32)),
        compiler_params=pltpu.CompilerParams(dimension_semantics=("arbitrary", "arbitrary"),
                                             vmem_limit_bytes=VMEM_LIMIT),
        name="sample_scores",
    )(page_table, iq_rows, w_rows, iknew_t, *([cache_idx_t] * SCORE_PAGES))


SAMPLE_TOPK_LANES = LANE


def _stopk_body(sc_in_ref, o_ref, sc_ref, *, dec_seq, past, topk):
    n_keys, lanes = sc_in_ref.shape
    sc_ref[...] = sc_in_ref[...]
    q = pl.program_id(0) * lanes + lax.broadcasted_iota(jnp.int32, (1, lanes), 1)
    n_allowed = (past + 1 + q % dec_seq).astype(F32)
    _select_topk(sc_ref, lanes, n_keys // K_CHUNK, n_allowed, topk, keys_axis=0)
    o_ref[...] = sc_ref[...]


def _sample_topk(scores_t, dec_seq, past, topk):
    n_keys, n_q = scores_t.shape
    lanes = min(SAMPLE_TOPK_LANES, n_q)
    spec = pl.BlockSpec((n_keys, lanes), lambda g: (0, g))
    return pl.pallas_call(
        functools.partial(_stopk_body, dec_seq=dec_seq, past=past, topk=topk),
        grid=(n_q // lanes,),
        in_specs=[spec],
        out_specs=spec,
        out_shape=jax.ShapeDtypeStruct((n_keys, n_q), F32),
        scratch_shapes=[pltpu.VMEM((n_keys, lanes), F32)],
        compiler_params=pltpu.CompilerParams(dimension_semantics=("arbitrary",),
                                             vmem_limit_bytes=VMEM_LIMIT),
        name="sample_topk",
    )(scores_t)


def _sattn_body(pt_ref, rb_ref, mask_ref, qp_ref, knew_ref, vnew_ref, *rest, dec_seq, past):
    del pt_ref
    kpages = rest[:ATTN_PAGES]
    vpages = rest[ATTN_PAGES:2 * ATTN_PAGES]
    o_ref = rest[2 * ATTN_PAGES]
    mb_ref, m_ref, l_ref, acc_ref = rest[2 * ATTN_PAGES + 1:]
    j = pl.program_id(1)
    n_steps = pl.num_programs(1)
    rows = dec_seq * ATT_HEADS
    row_t = lax.broadcasted_iota(jnp.int32, (rows, 1), 0) // ATT_HEADS
    row_h = lax.broadcasted_iota(jnp.int32, (rows, 1), 0) % ATT_HEADS

    @pl.when(j == 0)
    def _():
        for t in range(dec_seq):
            mb_ref[ATT_HEADS * t:ATT_HEADS * (t + 1), :] = jnp.broadcast_to(
                mask_ref[0, t:t + 1, :], (ATT_HEADS, past + K_CHUNK))
        m_ref[...] = jnp.full(m_ref.shape, -1e30, F32)
        l_ref[...] = jnp.zeros(l_ref.shape, F32)
        acc_ref[...] = jnp.zeros(acc_ref.shape, F32)

    qp = qp_ref[0]

    def near_bias(dist):
        bucket = _rel_bucket(dist)
        acc = jnp.zeros(dist.shape, F32)
        for bk in range(REL_BUCKETS - 1):
            per_row = jnp.zeros((rows, 1), F32)
            for h in range(ATT_HEADS):
                per_row = jnp.where(row_h == h, rb_ref[bk, h] - rb_ref[REL_BUCKETS - 1, h], per_row)
            acc = jnp.where(bucket == bk, per_row, acc)
        return acc * LOG2E

    def attend_pages(k_ts, v_ts, mb, last_bias):
        logit = jnp.concatenate(
            [jnp.dot(qp, k_t, preferred_element_type=F32) for k_t in k_ts], axis=1) + mb
        n = len(k_ts)
        if n > 1:
            last_bias = jnp.concatenate(
                [jnp.zeros((rows, (n - 1) * PAGE_SIZE), F32), last_bias], axis=1)
        logit = logit + last_bias
        m_old = m_ref[...]
        m_new = jnp.maximum(m_old, jnp.max(logit, axis=1, keepdims=True))
        alpha = jnp.exp2(m_old - m_new)
        p = jnp.exp2(logit - m_new).astype(BF16)
        l_ref[...] = alpha * l_ref[...] + jnp.sum(p.astype(F32), axis=1, keepdims=True)
        acc = alpha * acc_ref[...]
        for m, v_t in enumerate(v_ts):
            acc = acc + lax.dot_general(p[:, m * PAGE_SIZE:(m + 1) * PAGE_SIZE], v_t, NT_DIMS,
                                        preferred_element_type=F32)
        acc_ref[...] = acc
        m_ref[...] = m_new

    is_last = j == n_steps - 1
    off = lax.broadcasted_iota(jnp.int32, (rows, PAGE_SIZE), 1)
    k0 = pl.multiple_of(j * (ATTN_PAGES * PAGE_SIZE), ATTN_PAGES * PAGE_SIZE)
    last_bias = lax.cond(is_last, lambda: near_bias(PAGE_SIZE + row_t - off),
                         lambda: jnp.zeros((rows, PAGE_SIZE), F32))
    attend_pages([kp[0].astype(BF16) for kp in kpages], [vp[0].astype(BF16) for vp in vpages],
                 mb_ref[:, pl.ds(k0, ATTN_PAGES * PAGE_SIZE)], last_bias)

    @pl.when(is_last)
    def _():
        attend_pages([knew_ref[0]], [vnew_ref[0]], mb_ref[:, past:past + PAGE_SIZE],
                     near_bias(row_t - off))
        res = acc_ref[...] / l_ref[...]
        lane_h = lax.broadcasted_iota(jnp.int32, (rows, ATT_WIDTH), 1) // ATT_HEAD_DIM
        res = jnp.where(lane_h == row_h, res, 0.0)
        o_ref[0] = _head_sums(res, dec_seq).astype(BF16)


def _sample_attention(page_table, rel_bias, mask, qp_rows, knew_t, vnew_t, cache_k_t, cache_v_t,
                      dec_seq):
    nseq, n_pages = page_table.shape
    past = n_pages * PAGE_SIZE
    assert n_pages % ATTN_PAGES == 0
    steps = n_pages // ATTN_PAGES
    rows = dec_seq * ATT_HEADS

    def page_spec(m):
        return pl.BlockSpec((1, ATT_WIDTH, PAGE_SIZE),
                            lambda b, j, pt: (pt[b, j * ATTN_PAGES + m], 0, 0))

    per_seq = lambda shape: pl.BlockSpec((1,) + shape, lambda b, j, pt: (b, 0, 0))
    grid_spec = pltpu.PrefetchScalarGridSpec(
        num_scalar_prefetch=1,
        grid=(nseq, steps),
        in_specs=[pl.BlockSpec(memory_space=pltpu.SMEM),
                  per_seq((dec_seq, past + K_CHUNK)), per_seq((rows, ATT_WIDTH)),
                  per_seq((ATT_WIDTH, PAGE_SIZE)), per_seq((ATT_WIDTH, PAGE_SIZE))]
                 + [page_spec(m) for m in range(ATTN_PAGES)] * 2,
        out_specs=per_seq((dec_seq, ATT_WIDTH)),
        scratch_shapes=[
            pltpu.VMEM((rows, past + K_CHUNK), F32),
            pltpu.VMEM((rows, 1), F32),
            pltpu.VMEM((rows, 1), F32),
            pltpu.VMEM((rows, ATT_WIDTH), F32),
        ],
    )
    return pl.pallas_call(
        functools.partial(_sattn_body, dec_seq=dec_seq, past=past),
        grid_spec=grid_spec,
        out_shape=jax.ShapeDtypeStruct((nseq, dec_seq, ATT_WIDTH), BF16),
        compiler_params=pltpu.CompilerParams(dimension_semantics=("arbitrary", "arbitrary"),
                                             vmem_limit_bytes=VMEM_LIMIT),
        name="sample_attention",
    )(page_table, rel_bias, mask, qp_rows, knew_t, vnew_t,
      *([cache_k_t] * ATTN_PAGES), *([cache_v_t] * ATTN_PAGES))


FF_CHUNK = 256


def _post_body(att_ref, gla_ref, x_ref, p_ref, wo_ref, gpm_ref, gpf_ref, wg_ref, wu_ref, wd_ref,
               gpo_ref, wpg_ref, wpp_ref, o_ref, *, d_ff):
    wo = wo_ref
    mixed = jnp.dot(att_ref[...], wo[0:ATT_WIDTH, :], preferred_element_type=F32)
    mixed = mixed + jnp.dot(gla_ref[...], wo[ATT_WIDTH:ATT_WIDTH + GLA_WIDTH, :],
                            preferred_element_type=F32)
    h = x_ref[...] + _rms(mixed, gpm_ref[...])
    f = _rms(h, gpf_ref[...]).astype(BF16)
    ff = jnp.zeros(h.shape, F32)
    for c in range(d_ff // FF_CHUNK):
        cs = slice(c * FF_CHUNK, (c + 1) * FF_CHUNK)
        gate = jnp.dot(f, wg_ref[:, cs], preferred_element_type=F32)
        up = jnp.dot(f, wu_ref[:, cs], preferred_element_type=F32)
        act = (gate * jax.nn.sigmoid(gate) * up).astype(BF16)
        ff = ff + jnp.dot(act, wd_ref[cs, :], preferred_element_type=F32)
    h = h + _rms(ff, gpo_ref[...])
    gate = jax.nn.sigmoid(jnp.dot(h.astype(BF16), wpg_ref[...], preferred_element_type=F32))
    ple = jnp.dot(p_ref[...].astype(BF16), wpp_ref[...], preferred_element_type=F32)
    o_ref[...] = h + gate * ple


def _post(att, gla, x, p, wo, gpm, gpf, wg, wu, wd, gpo, wpg, wpp, tm):
    t = x.shape[0]
    d_ff = wg.shape[1]
    row = lambda w: pl.BlockSpec((tm, w), lambda i: (i, 0))
    const = lambda a: pl.BlockSpec(a.shape, lambda i: (0, 0), pipeline_mode=pl.Buffered(1))
    return pl.pallas_call(
        functools.partial(_post_body, d_ff=d_ff),
        grid=(t // tm,),
        in_specs=[row(ATT_WIDTH), row(GLA_WIDTH), row(D_MODEL), row(p.shape[1]),
                  const(wo), const(gpm), const(gpf), const(wg), const(wu), const(wd),
                  const(gpo), const(wpg), const(wpp)],
        out_specs=row(D_MODEL),
        out_shape=jax.ShapeDtypeStruct((t, D_MODEL), F32),
        compiler_params=pltpu.CompilerParams(dimension_semantics=("arbitrary",),
                                             vmem_limit_bytes=VMEM_LIMIT),
        name="post",
    )(att, gla, x, p, wo, gpm, gpf, wg, wu, wd, gpo, wpg, wpp)


def _pack_w_in(w_in):
    offs = [0]
    for wdt in (ATT_WIDTH, ATT_WIDTH, ATT_WIDTH, IDX_HEADS * IDX_DIM, IDX_DIM, IDX_HEADS,
                GLA_KEY_WIDTH, GLA_KEY_WIDTH, GLA_WIDTH, GLA_GATE_RANK, GLA_WIDTH):
        offs.append(offs[-1] + wdt)
    aq, ak, av, iq, ik, iw, gq, gk, gv, glr, gout = [w_in[:, offs[n]:offs[n + 1]] for n in range(11)]
    d = w_in.shape[0]
    zeros = lambda n: jnp.zeros((d, n), w_in.dtype)
    packed = jnp.concatenate([
        aq * (ATT_HEAD_DIM ** -0.5 * LOG2E), ak, av, iq * (IDX_DIM ** -0.5),
        ik, iw, zeros(LANE - IDX_DIM - IDX_HEADS),
        gq * (GLA_DK ** -0.5), gk, gv, glr, zeros(LANE - GLA_GATE_RANK), gout], axis=1)
    return packed.astype(BF16)


def _state_to_t(s):
    n = s.shape[0]
    return jnp.transpose(s, (0, 3, 1, 2)).reshape(n, GLA_DV, GLA_KEY_WIDTH)


def _state_from_t(st):
    n = st.shape[0]
    return jnp.transpose(st.reshape(n, GLA_DV, GLA_HEADS, GLA_DK), (0, 2, 3, 1))


def _layer(x_prompt, x_sample, p_prompt, p_sample, cache_k, cache_v, cache_idx_k, state_gla,
           page_table, rel_bias, g_pre_mix, w_in, w_gate_up, b_gate, g_gla_out, w_out,
           g_post_mix, g_pre_ffn, w_ff_gate, w_ff_up, w_ff_down, g_post_ffn, w_ple_gate,
           w_ple_proj):
    bsz, seq, d = x_prompt.shape
    nseq, dec_seq, _ = x_sample.shape
    n_pool = cache_k.shape[0]
    past = page_table.shape[1] * PAGE_SIZE
    row2 = lambda g: g.reshape(1, -1)

    w_packed = _pack_w_in(w_in)
    w_t = jnp.transpose(w_packed[:, SEG_K[0]:SEG_IKW[0] + IDX_DIM])
    w_t = jnp.concatenate([w_t[0:2 * ATT_WIDTH], w_t[SEG_IKW[0] - SEG_K[0]:]], axis=0)
    wgu = jnp.concatenate(
        [w_gate_up, jnp.zeros((LANE - GLA_GATE_RANK, GLA_KEY_WIDTH), w_gate_up.dtype)],
        axis=0).astype(BF16)
    post_w = (w_out.astype(BF16), row2(g_post_mix), row2(g_pre_ffn), w_ff_gate.astype(BF16),
              w_ff_up.astype(BF16), w_ff_down.astype(BF16), row2(g_post_ffn),
              w_ple_gate.astype(BF16), w_ple_proj.astype(BF16))
    gla_w = (wgu, row2(b_gate), row2(g_gla_out))

    tp = bsz * seq
    q, kb, iq, ikw, ikb, gin, k_t, v_t, vtb, ik_t = _inproj(
        x_prompt.reshape(tp, d), row2(g_pre_mix), w_packed, w_t, min(512, seq), seq)
    r3 = lambda a: a.reshape(bsz, seq, a.shape[-1])
    att = _prompt_attention(rel_bias, r3(q), r3(iq), r3(ikw), r3(kb), vtb, r3(ikb),
                            min(TOPK_MAX, seq // 4))
    gla_chunk = min(128, seq)
    gla_o, st_p = _gla(r3(gin), jnp.zeros((bsz, GLA_DV, GLA_KEY_WIDTH), F32), *gla_w,
                       tile=min(512, seq), chunk=gla_chunk, n_valid=gla_chunk)
    y_p = _post(att.reshape(tp, ATT_WIDTH), gla_o.reshape(tp, GLA_WIDTH), x_prompt.reshape(tp, d),
                p_prompt.reshape(tp, -1), *post_w, min(512, tp))
    heads_last = lambda a: jnp.transpose(
        a.reshape(bsz, ATT_HEADS, ATT_HEAD_DIM, seq), (0, 3, 1, 2))
    outs_p = (y_p.reshape(bsz, seq, d), heads_last(k_t), heads_last(v_t),
              jnp.swapaxes(ik_t, 1, 2), _state_from_t(st_p))

    ts = nseq * dec_seq
    q, kb, iq, ikw, ikb, gin, kf, vf, vb = _inproj(
        x_sample.reshape(ts, d), row2(g_pre_mix), w_packed, None, min(512, ts), ts)
    rows = dec_seq * ATT_HEADS
    iq_rows = iq.reshape(nseq, rows, IDX_DIM)
    w_rows = ikw[:, IDX_DIM:IDX_DIM + IDX_HEADS].reshape(nseq, rows, 1)
    head_of_col = jnp.arange(ATT_WIDTH, dtype=jnp.int32) // ATT_HEAD_DIM
    head_mask = head_of_col[None, :] == jnp.arange(ATT_HEADS, dtype=jnp.int32)[:, None]
    qp_rows = jnp.where(head_mask[None, None], q.reshape(nseq, dec_seq, 1, ATT_WIDTH),
                        jnp.zeros((), BF16)).reshape(nseq, rows, ATT_WIDTH)
    per_seq = lambda a: a.reshape(nseq, dec_seq, a.shape[-1])
    new_page_t = lambda a: jnp.pad(jnp.swapaxes(per_seq(a), 1, 2),
                                   ((0, 0), (0, 0), (0, PAGE_SIZE - dec_seq)))
    cache_idx_t = jnp.swapaxes(cache_idx_k, 1, 2)
    cache_k_t = jnp.transpose(cache_k, (0, 2, 3, 1)).reshape(n_pool, ATT_WIDTH, PAGE_SIZE)
    cache_v_t = jnp.transpose(cache_v, (0, 2, 3, 1)).reshape(n_pool, ATT_WIDTH, PAGE_SIZE)
    sc_past, sc_new = _sample_scores(page_table, iq_rows, w_rows, new_page_t(ikb[:, :IDX_DIM]),
                                     cache_idx_t, dec_seq)
    scores_t = jnp.transpose(jnp.concatenate([sc_past, sc_new], axis=2).reshape(ts, -1))
    mask_t = _sample_topk(scores_t, dec_seq, past, min(TOPK_MAX, (past + dec_seq) // 4))
    mask = jnp.transpose(mask_t).reshape(nseq, dec_seq, -1)
    att_s = _sample_attention(page_table, rel_bias, mask, qp_rows, new_page_t(kb),
                              new_page_t(vb), cache_k_t, cache_v_t, dec_seq)
    gin_s = jnp.pad(per_seq(gin), ((0, 0), (0, NEW_PAD - dec_seq), (0, 0)))
    gla_s, st_s = _gla(gin_s, _state_to_t(state_gla), *gla_w,
                       tile=NEW_PAD, chunk=NEW_PAD, n_valid=dec_seq)
    y_s = _post(att_s.reshape(ts, ATT_WIDTH), gla_s[:, :dec_seq].reshape(ts, GLA_WIDTH),
                x_sample.reshape(ts, d), p_sample.reshape(ts, -1), *post_w, min(256, ts))
    outs_s = (y_s.reshape(nseq, dec_seq, d),
              kf.reshape(nseq, dec_seq, ATT_HEADS, ATT_HEAD_DIM),
              vf.reshape(nseq, dec_seq, ATT_HEADS, ATT_HEAD_DIM),
              ikw[:, :IDX_DIM].reshape(nseq, dec_seq, IDX_DIM),
              _state_from_t(st_s))
    return outs_p, outs_s


def kernel(x_prompt, x_sample, p_prompt, p_sample, cache_k, cache_v, cache_idx_k, state_gla,
           page_table, rel_bias, g_pre_mix, w_in, w_gate_up, b_gate, g_gla_out, w_out,
           g_post_mix, g_pre_ffn, w_ff_gate, w_ff_up, w_ff_down, g_post_ffn, w_ple_gate,
           w_ple_proj):
    depth = w_in.shape[0]
    h_p, h_s = x_prompt, x_sample
    per_layer = []
    for i in range(depth):
        outs_p, outs_s = _layer(
            h_p, h_s, p_prompt[i], p_sample[i], cache_k[i], cache_v[i], cache_idx_k[i],
            state_gla[i], page_table, rel_bias, g_pre_mix[i], w_in[i], w_gate_up[i], b_gate[i],
            g_gla_out[i], w_out[i], g_post_mix[i], g_pre_ffn[i], w_ff_gate[i], w_ff_up[i],
            w_ff_down[i], g_post_ffn[i], w_ple_gate[i], w_ple_proj[i])
        h_p, h_s = outs_p[0], outs_s[0]
        per_layer.append(outs_p[1:] + outs_s[1:])
    stacked = [jnp.stack([lay[n] for lay in per_layer]) for n in range(8)]
    return (h_p, h_s, stacked[0], stacked[1], stacked[2], stacked[3],
            stacked[4], stacked[5], stacked[6], stacked[7])
```

```python
import functools
import math

import jax
import jax.numpy as jnp
from jax import lax
from jax.experimental import pallas as pl
from jax.experimental.pallas import tpu as pltpu

F32 = jnp.float32
BF16 = jnp.bfloat16

D_MODEL = 1024
ATT_HEADS = 8
ATT_HEAD_DIM = 64
ATT_WIDTH = ATT_HEADS * ATT_HEAD_DIM
IDX_HEADS = 8
IDX_DIM = 64
TOPK_MAX = 256
PAGE_SIZE = 128
GLA_HEADS = 4
GLA_DK = 64
GLA_DV = 128
GLA_KEY_WIDTH = GLA_HEADS * GLA_DK
GLA_WIDTH = GLA_HEADS * GLA_DV
GLA_GATE_RANK = 16
GLA_GATE_TAU = 16.0
REL_BUCKETS = 32
REL_MAX_DIST = 128
RMS_EPS = 1e-6
LOG2E = math.log2(math.e)

LANE = 128
SUBLANE = 8
BF16_ROWS = 16
BF16_ULP_BOUND = 2.0 ** -7
BF16_EXACT_INT = 256
MXU_WIDTH = 256
Q_TILE = MXU_WIDTH
K_CHUNK = MXU_WIDTH
HEAD_GROUP = MXU_WIDTH // ATT_HEAD_DIM
ACC_ROWS = ATT_HEAD_DIM + BF16_ROWS
VMEM_LIMIT = 56 * 1024 * 1024
COARSE_ITERS = 10
FINE_ITERS = 9

SEG_Q = (0, 512)
SEG_K = (512, 1024)
SEG_V = (1024, 1536)
SEG_IQ = (1536, 2048)
SEG_IKW = (2048, 2176)
SEG_GLA = (2176, 3840)
PROJ_PACKED = 3840
GLA_IN_WIDTH = SEG_GLA[1] - SEG_GLA[0]

NT_DIMS = (((1,), (1,)), ((), ()))
TN_DIMS = (((0,), (0,)), ((), ()))


def _rms(x, g):
    return x * lax.rsqrt(jnp.mean(x * x, axis=-1, keepdims=True) + RMS_EPS) * g


def _inproj_body(x_ref, g_ref, w_ref, *rest, tokens_minor):
    a = _rms(x_ref[...], g_ref[...]).astype(BF16)

    def proj(seg):
        return jnp.dot(a, w_ref[:, seg[0]:seg[1]], preferred_element_type=F32)

    if tokens_minor:
        wt_ref, q_ref, kb_ref, iq_ref, ikw_ref, ikb_ref, gla_ref, kt_ref, vt_ref, vtb_ref, ikt_ref = rest
    else:
        q_ref, kb_ref, iq_ref, ikw_ref, ikb_ref, gla_ref, k_ref, v_ref, vb_ref = rest

    q_ref[...] = proj(SEG_Q).astype(BF16)
    kf = proj(SEG_K)
    kb_ref[...] = kf.astype(BF16)
    iq_ref[...] = proj(SEG_IQ).astype(BF16)
    lane = lax.broadcasted_iota(jnp.int32, (1, LANE), 1)
    is_w = (lane >= IDX_DIM) & (lane < IDX_DIM + IDX_HEADS)
    ikw = proj(SEG_IKW) * jnp.where(is_w, IDX_HEADS ** -0.5, 1.0)
    ikw_ref[...] = ikw
    ikb_ref[...] = ikw.astype(BF16)
    gla_ref[...] = proj(SEG_GLA)
    if tokens_minor:
        def proj_t(lo, hi):
            return lax.dot_general(wt_ref[lo:hi, :], a, NT_DIMS, preferred_element_type=F32)

        kt_ref[0] = proj_t(0, ATT_WIDTH)
        vt = proj_t(ATT_WIDTH, 2 * ATT_WIDTH)
        vt_ref[0] = vt
        vtb_ref[0] = vt.astype(BF16)
        ikt_ref[0] = proj_t(2 * ATT_WIDTH, 2 * ATT_WIDTH + IDX_DIM)
    else:
        k_ref[...] = kf
        vf = proj(SEG_V)
        v_ref[...] = vf
        vb_ref[...] = vf.astype(BF16)


def _inproj(x, g, w_packed, w_t, tm, seq):
    t = x.shape[0]
    tokens_minor = w_t is not None
    tiles_per_seq = seq // tm
    row = lambda w: pl.BlockSpec((tm, w), lambda i: (i, 0))
    const = lambda s: pl.BlockSpec(s, lambda i: (0, 0))
    col = lambda r: pl.BlockSpec((1, r, tm), lambda i: (i // tiles_per_seq, 0, i % tiles_per_seq))
    rows_shape = lambda w, dt: jax.ShapeDtypeStruct((t, w), dt)
    cols_shape = lambda r, dt: jax.ShapeDtypeStruct((t // seq, r, seq), dt)
    out_shape = [rows_shape(ATT_WIDTH, BF16), rows_shape(ATT_WIDTH, BF16),
                 rows_shape(ATT_WIDTH, BF16), rows_shape(LANE, F32), rows_shape(LANE, BF16),
                 rows_shape(GLA_IN_WIDTH, F32)]
    out_specs = [row(ATT_WIDTH), row(ATT_WIDTH), row(ATT_WIDTH), row(LANE), row(LANE),
                 row(GLA_IN_WIDTH)]
    in_specs = [row(D_MODEL), const((1, D_MODEL)), const((D_MODEL, PROJ_PACKED))]
    operands = [x, g, w_packed]
    if tokens_minor:
        in_specs.append(const(w_t.shape))
        operands.append(w_t)
        out_shape += [cols_shape(ATT_WIDTH, F32), cols_shape(ATT_WIDTH, F32),
                      cols_shape(ATT_WIDTH, BF16), cols_shape(IDX_DIM, F32)]
        out_specs += [col(ATT_WIDTH), col(ATT_WIDTH), col(ATT_WIDTH), col(IDX_DIM)]
    else:
        out_shape += [rows_shape(ATT_WIDTH, F32), rows_shape(ATT_WIDTH, F32),
                      rows_shape(ATT_WIDTH, BF16)]
        out_specs += [row(ATT_WIDTH), row(ATT_WIDTH), row(ATT_WIDTH)]
    return pl.pallas_call(
        functools.partial(_inproj_body, tokens_minor=tokens_minor),
        grid=(t // tm,),
        in_specs=in_specs,
        out_specs=tuple(out_specs),
        out_shape=tuple(out_shape),
        compiler_params=pltpu.CompilerParams(dimension_semantics=("arbitrary",),
                                             vmem_limit_bytes=VMEM_LIMIT),
        name="inproj",
    )(*operands)


def _log_sigmoid(z):
    return jnp.minimum(z, 0.0) - jnp.log1p(jnp.exp(-jnp.abs(z)))


def _gla_body(gin_ref, s0_ref, wgu_ref, bg_ref, gg_ref, o_ref, sout_ref, st_ref, *, chunk, n_valid):
    @pl.when(pl.program_id(1) == 0)
    def _():
        st_ref[...] = s0_ref[0]

    n_chunks = gin_ref.shape[1] // chunk
    ri = lax.broadcasted_iota(jnp.int32, (chunk, chunk), 0)
    ci = lax.broadcasted_iota(jnp.int32, (chunk, chunk), 1)
    causal = ri >= ci
    tri = causal.astype(F32)
    row_valid = lax.broadcasted_iota(jnp.int32, (chunk, 1), 0) < n_valid

    def one_chunk(c, carry):
        r0 = pl.multiple_of(c * chunk, chunk)
        blk = gin_ref[0, pl.ds(r0, chunk), :]
        gq = blk[:, 0:256]
        gk = blk[:, 256:512]
        gv = blk[:, 512:1024]
        glr = blk[:, 1024:1152]
        gout = blk[:, 1152:1664]
        z = jnp.dot(glr.astype(BF16), wgu_ref[...], preferred_element_type=F32) + bg_ref[...]
        la = _log_sigmoid(z) / GLA_GATE_TAU
        if n_valid < chunk:
            la = jnp.where(row_valid, la, 0.0)
        b = jnp.dot(tri, la, precision=lax.Precision.HIGHEST, preferred_element_type=F32)
        b_last = b[chunk - 1:chunk, :]
        qt = gq * jnp.exp(b)
        kt = gk * jnp.exp(-b)
        kd = gk * jnp.exp(b_last - b)
        dec = jnp.exp(b_last)
        st = st_ref[...]
        o_parts = []
        st_parts = []
        for h in range(GLA_HEADS):
            ks = slice(GLA_DK * h, GLA_DK * (h + 1))
            vs = slice(GLA_DV * h, GLA_DV * (h + 1))
            qh = qt[:, ks].astype(BF16)
            kh = kt[:, ks].astype(BF16)
            kdh = kd[:, ks].astype(BF16)
            vh = gv[:, vs].astype(BF16)
            sth = st[:, ks]
            attn = lax.dot_general(qh, kh, NT_DIMS, preferred_element_type=F32)
            attn = jnp.where(causal, attn, 0.0)
            o = jnp.dot(attn.astype(BF16), vh, preferred_element_type=F32)
            o = o + lax.dot_general(qh, sth.astype(BF16), NT_DIMS, preferred_element_type=F32)
            upd = lax.dot_general(vh, kdh, TN_DIMS, preferred_element_type=F32)
            st_parts.append(dec[:, ks] * sth + upd)
            o_parts.append(_rms(o, gg_ref[...]))
        st_ref[...] = jnp.concatenate(st_parts, axis=1)
        o_all = jnp.concatenate(o_parts, axis=1)
        gate = gout * jax.nn.sigmoid(gout)
        o_ref[0, pl.ds(r0, chunk), :] = (o_all * gate).astype(BF16)
        return carry

    lax.fori_loop(0, n_chunks, one_chunk, 0)
    sout_ref[0] = st_ref[...]


def _gla(gin, s0_t, wgu, bg, gg, *, tile, chunk, n_valid):
    nseq, t, _ = gin.shape
    body = functools.partial(_gla_body, chunk=chunk, n_valid=n_valid)
    return pl.pallas_call(
        body,
        grid=(nseq, t // tile),
        in_specs=[
            pl.BlockSpec((1, tile, GLA_IN_WIDTH), lambda b, j: (b, j, 0)),
            pl.BlockSpec((1, GLA_DV, GLA_KEY_WIDTH), lambda b, j: (b, 0, 0)),
            pl.BlockSpec((LANE, GLA_KEY_WIDTH), lambda b, j: (0, 0)),
            pl.BlockSpec((1, GLA_KEY_WIDTH), lambda b, j: (0, 0)),
            pl.BlockSpec((1, GLA_DV), lambda b, j: (0, 0)),
        ],
        out_specs=(
            pl.BlockSpec((1, tile, GLA_WIDTH), lambda b, j: (b, j, 0)),
            pl.BlockSpec((1, GLA_DV, GLA_KEY_WIDTH), lambda b, j: (b, 0, 0)),
        ),
        out_shape=(
            jax.ShapeDtypeStruct((nseq, t, GLA_WIDTH), BF16),
            jax.ShapeDtypeStruct((nseq, GLA_DV, GLA_KEY_WIDTH), F32),
        ),
        scratch_shapes=[pltpu.VMEM((GLA_DV, GLA_KEY_WIDTH), F32)],
        compiler_params=pltpu.CompilerParams(dimension_semantics=("arbitrary", "arbitrary"),
                                             vmem_limit_bytes=VMEM_LIMIT),
        name="gla",
    )(gin, s0_t, wgu, bg, gg)


def _rel_bucket(dist):
    n = jnp.maximum(dist, 0)
    exact = REL_BUCKETS // 2
    nf = jnp.maximum(n, 1).astype(F32)
    large = exact + (jnp.log(nf / exact) / math.log(REL_MAX_DIST / exact)
                     * (REL_BUCKETS - exact)).astype(jnp.int32)
    large = jnp.minimum(large, REL_BUCKETS - 1)
    return jnp.where(n < exact, n, large)


def _select_topk(sc_ref, n_queries, n_chunks, n_allowed, topk, keys_axis, max_iter=18,
                 coarse_ref=None, coarse_iter=0):
    neg_inf = -jnp.inf
    n_acc = 4 if keys_axis == 0 else K_CHUNK // LANE
    if keys_axis == 1:
        part_shape = (n_queries, LANE)
        q_shape = (n_queries, 1)
        chunk_shape = (n_queries, K_CHUNK)
        load = lambda k0: sc_ref[:, pl.ds(k0, K_CHUNK)]
        pieces = lambda x: [x[:, j * LANE:(j + 1) * LANE] for j in range(K_CHUNK // LANE)]
        last = lambda r: r[:, K_CHUNK - 1:K_CHUNK]
    else:
        part_shape = (SUBLANE, n_queries)
        q_shape = (1, n_queries)
        chunk_shape = (K_CHUNK, n_queries)
        load = lambda k0: sc_ref[pl.ds(k0, K_CHUNK), :]
        pieces = lambda x: [x[j * SUBLANE:(j + 1) * SUBLANE] for j in range(K_CHUNK // SUBLANE)]
        last = lambda r: r[K_CHUNK - 1:K_CHUNK, :]

    def store(k0, val):
        if keys_axis == 1:
            sc_ref[:, pl.ds(k0, K_CHUNK)] = val
        else:
            sc_ref[pl.ds(k0, K_CHUNK), :] = val

    def sweep(fns, inits, reds, combines):
        n_stat = len(fns)

        def body(c, accs):
            x = load(pl.multiple_of(c * K_CHUNK, K_CHUNK))
            accs = [list(a) for a in accs]
            for n, piece in enumerate(pieces(x)):
                for s in range(n_stat):
                    accs[s][n % n_acc] = fns[s](accs[s][n % n_acc], piece)
            return tuple(tuple(a) for a in accs)

        init = tuple(tuple(jnp.full(part_shape, inits[s], F32) for _ in range(n_acc))
                     for s in range(n_stat))
        accs = lax.fori_loop(0, n_chunks, body, init)
        out = []
        for s in range(n_stat):
            acc = accs[s][0]
            for other in accs[s][1:]:
                acc = combines[s](acc, other)
            out.append(reds[s](acc, axis=keys_axis, keepdims=True))
        return out

    def count_ge(t):
        tb = jnp.broadcast_to(t, part_shape)
        return sweep([lambda a, x: jnp.where(x >= tb, a + 1.0, a)], [0.0], [jnp.sum],
                     [jnp.add])[0]

    def count_gt(t):
        tb = jnp.broadcast_to(t, part_shape)
        return sweep([lambda a, x: a + jnp.where(x > tb, 1.0, 0.0)], [0.0], [jnp.sum],
                     [jnp.add])[0]

    def max_below(t):
        tb = jnp.broadcast_to(t, part_shape)
        return sweep([lambda a, x: jnp.maximum(a, jnp.where(x < tb, x, neg_inf))], [neg_inf],
                     [jnp.max], [jnp.maximum])[0]

    row_max, row_min = sweep(
        [jnp.maximum, lambda a, x: jnp.minimum(a, jnp.where(x == neg_inf, jnp.inf, x))],
        [neg_inf, jnp.inf], [jnp.max, jnp.min], [jnp.maximum, jnp.minimum])

    k = float(topk)
    take_all = n_allowed <= k
    lo0 = row_min
    hi0 = row_max + jnp.maximum(jnp.abs(row_max) * 1e-6, 1e-30)
    t0 = jnp.full(q_shape, jnp.finfo(F32).min, F32)
    done0 = jnp.where(take_all, 1.0, 0.0)

    if coarse_ref is not None:
        rows_b = BF16_ROWS
        one_b = jnp.ones((rows_b, n_queries), BF16)
        zero_b = jnp.zeros((rows_b, n_queries), BF16)

        def count_b(m):
            mb_ = jnp.broadcast_to(m.astype(BF16), (rows_b, n_queries))

            def body(c, accs):
                x = coarse_ref[pl.ds(pl.multiple_of(c * K_CHUNK, K_CHUNK), K_CHUNK), :]
                accs = list(accs)
                for n in range(K_CHUNK // rows_b):
                    piece = x[n * rows_b:(n + 1) * rows_b]
                    accs[n % n_acc] = accs[n % n_acc] + jnp.where(piece >= mb_, one_b, zero_b)
                return tuple(accs)

            accs = lax.fori_loop(0, n_chunks, body, tuple(zero_b for _ in range(n_acc)))
            total = accs[0].astype(F32)
            for other in accs[1:]:
                total = total + other.astype(F32)
            return jnp.sum(total, axis=0, keepdims=True)

        def coarse_body(_, st):
            lo, hi = st
            m = (lo + (hi - lo) * 0.5).astype(BF16).astype(F32)
            inside = (m > lo) & (m < hi)
            c = count_b(m)
            below = m - (jnp.abs(m) * BF16_ULP_BOUND + 1e-30)
            hi = jnp.where(inside & (c < k), m, hi)
            lo = jnp.where(inside & (c >= k) & (below > lo), below, lo)
            return lo, hi

        lo0, hi0 = lax.fori_loop(0, coarse_iter, coarse_body, (lo0, hi0))

    def bis_body(_, st):
        lo, hi, t, done = st
        mid = lo + (hi - lo) * 0.5
        c = count_ge(mid)
        hit = (c == k) & (done < 0.5)
        t = jnp.where(hit, mid, t)
        done = jnp.where(hit, 1.0, done)
        lo = jnp.where(c > k, mid, lo)
        hi = jnp.where(c < k, mid, hi)
        return lo, hi, t, done

    lo, hi, t, done = lax.fori_loop(0, max_iter, bis_body, (lo0, hi0, t0, done0))

    all_done = jnp.min(done) > 0.5

    def write(fn):
        def body(c, carry):
            k0 = pl.multiple_of(c * K_CHUNK, K_CHUNK)
            sel, carry = fn(load(k0), carry)
            store(k0, jnp.where(sel, 0.0, neg_inf))
            return carry
        return body

    def walk_down():
        def snap_cond(st):
            return jnp.min(st[2]) < 0.5

        def snap_body(st):
            hi_, t_, done_, over_ = st
            v = max_below(hi_)
            c = count_ge(v)
            ok = (c >= k) & (done_ < 0.5)
            t_ = jnp.where(ok, v, t_)
            over_ = jnp.where(ok, c - k, over_)
            hi_ = jnp.where((done_ < 0.5) & jnp.logical_not(ok), v, hi_)
            done_ = jnp.where(ok, 1.0, done_)
            return hi_, t_, done_, over_

        _, t_, _, over_ = lax.while_loop(snap_cond, snap_body,
                                         (hi, t, done, jnp.zeros(q_shape, F32)))
        return t_, over_

    tv, over = lax.cond(all_done, lambda: (t, jnp.zeros(q_shape, F32)), walk_down)
    has_ties = jnp.max(over) > 0.5

    @pl.when(jnp.logical_not(has_ties))
    def _():
        tb = jnp.broadcast_to(tv, chunk_shape)
        lax.fori_loop(0, n_chunks, write(lambda x, cr: (x >= tb, cr)), 0)

    @pl.when(has_ties)
    def _():
        quota = k - count_gt(tv)
        tb = jnp.broadcast_to(tv, chunk_shape)
        qb = jnp.broadcast_to(quota, chunk_shape)
        ui = lax.broadcasted_iota(jnp.int32, (K_CHUNK, K_CHUNK), 0)
        uj = lax.broadcasted_iota(jnp.int32, (K_CHUNK, K_CHUNK), 1)
        prefix = ((ui <= uj) if keys_axis == 1 else (ui >= uj)).astype(BF16)

        def tie_fn(x, seen):
            tie = x == tb
            if keys_axis == 1:
                rank = jnp.dot(tie.astype(BF16), prefix, preferred_element_type=F32) + seen
            else:
                rank = jnp.dot(prefix, tie.astype(BF16), preferred_element_type=F32) + seen
            sel = (x > tb) | (tie & (rank <= qb))
            return sel, last(rank)

        lax.fori_loop(0, n_chunks, write(tie_fn), jnp.zeros(q_shape, F32))


def _head_lane_mask(h, width):
    lane = lax.broadcasted_iota(jnp.int32, (1, width), 1)
    lo = (h % HEAD_GROUP) * ATT_HEAD_DIM
    return (lane >= lo) & (lane < lo + ATT_HEAD_DIM)


def _pattn_body(rb_ref, q_ref, iq_ref, w_ref, k_ref, vt_ref, ik_ref, o_ref,
                sc_ref, scb_ref, qp_ref, iqp_ref, bias_ref, m_ref, alpha_ref, acc_ref, x_ref,
                p_ref, *, topk):
    i = pl.program_id(1)
    n_chunks = i + 1

    @pl.when((pl.program_id(0) == 0) & (i == 0))
    def _():
        kk = lax.broadcasted_iota(jnp.int32, (K_CHUNK, Q_TILE), 0)
        qq = lax.broadcasted_iota(jnp.int32, (K_CHUNK, Q_TILE), 1)
        for delta in range(2):
            bucket = _rel_bucket(qq - kk + Q_TILE * delta)
            for h in range(ATT_HEADS):
                acc = jnp.zeros((K_CHUNK, Q_TILE), F32)
                for bk in range(REL_BUCKETS - 1):
                    acc = jnp.where(bucket == bk, rb_ref[bk, h] - rb_ref[REL_BUCKETS - 1, h], acc)
                bias_ref[delta, h] = acc * LOG2E
        bias_ref[2] = jnp.zeros(bias_ref.shape[1:], F32)

    q = q_ref[0]
    iq = iq_ref[0]
    for h in range(ATT_HEADS):
        g = h // HEAD_GROUP
        qp_ref[h] = jnp.where(_head_lane_mask(h, MXU_WIDTH),
                              q[:, MXU_WIDTH * g:MXU_WIDTH * (g + 1)], 0)
        iqp_ref[h] = jnp.concatenate(
            [iq[:, IDX_DIM * h:IDX_DIM * (h + 1)], jnp.zeros((Q_TILE, LANE - IDX_DIM), BF16)], axis=1)
    w_t = jnp.transpose(w_ref[0])[IDX_DIM:IDX_DIM + IDX_HEADS, :]

    q_pos = i * Q_TILE + lax.broadcasted_iota(jnp.int32, (1, Q_TILE), 1)

    def score_chunk(c, carry, *, diagonal):
        k0 = pl.multiple_of(c * K_CHUNK, K_CHUNK)
        ikc = ik_ref[0, pl.ds(k0, K_CHUNK), :]
        acc = jnp.zeros((K_CHUNK, Q_TILE), F32)
        for h in range(IDX_HEADS):
            s = lax.dot_general(ikc, iqp_ref[h], NT_DIMS, preferred_element_type=F32)
            acc = acc + jnp.maximum(s, 0.0) * w_t[h:h + 1, :]
        if diagonal:
            k_pos = k0 + lax.broadcasted_iota(jnp.int32, (K_CHUNK, 1), 0)
            acc = jnp.where(k_pos <= q_pos, acc, -jnp.inf)
        sc_ref[pl.ds(k0, K_CHUNK), :] = acc
        scb_ref[pl.ds(k0, K_CHUNK), :] = acc.astype(BF16)
        return carry

    def score_pair(a, carry):
        score_chunk(2 * a, carry, diagonal=False)
        return score_chunk(2 * a + 1, carry, diagonal=False)

    lax.fori_loop(0, i // 2, score_pair, 0)
    lax.fori_loop(2 * (i // 2), i, functools.partial(score_chunk, diagonal=False), 0)
    score_chunk(i, 0, diagonal=True)

    _select_topk(sc_ref, Q_TILE, n_chunks, (q_pos + 1).astype(F32), topk, keys_axis=0,
                 max_iter=FINE_ITERS, coarse_ref=scb_ref, coarse_iter=COARSE_ITERS)

    m_ref[...] = jnp.full(m_ref.shape, -1e30, F32)
    alpha_ref[...] = jnp.ones(alpha_ref.shape, F32)
    acc_ref[...] = jnp.zeros(acc_ref.shape, F32)
    p_ref[0] = jnp.zeros(p_ref.shape[1:], BF16)
    x_ref[1] = jnp.full(x_ref.shape[1:], -jnp.inf, F32)
    ones_rows = jnp.ones((BF16_ROWS, K_CHUNK), BF16)
    row_block = 4 * SUBLANE

    def logits(h, slot, k0, mb, near):
        g = h // HEAD_GROUP
        kc = k_ref[0, pl.ds(k0, K_CHUNK), MXU_WIDTH * g:MXU_WIDTH * (g + 1)]
        s = lax.dot_general(kc, qp_ref[h], NT_DIMS, preferred_element_type=F32)
        mx = jnp.full((SUBLANE, Q_TILE), -jnp.inf, F32)
        for r in range(K_CHUNK // row_block):
            rs = slice(r * row_block, (r + 1) * row_block)
            xb = s[rs] + mb[rs]
            if near is not None:
                xb = xb + bias_ref[near, h, rs, :]
            x_ref[slot, h, rs, :] = xb
            for u in range(row_block // SUBLANE):
                mx = jnp.maximum(mx, xb[u * SUBLANE:(u + 1) * SUBLANE])
        m_old = m_ref[1 - slot, h]
        m_new = jnp.maximum(m_old, jnp.max(mx, axis=0, keepdims=True))
        alpha_ref[slot, h] = jnp.exp2(m_old - m_new)
        m_ref[slot, h] = m_new

    def probs(h, slot):
        p_ref[slot, h] = jnp.exp2(x_ref[slot, h] - m_ref[slot, h]).astype(BF16)

    def accumulate(h, slot, k0):
        vt = vt_ref[0, ATT_HEAD_DIM * h:ATT_HEAD_DIM * (h + 1), pl.ds(k0, K_CHUNK)]
        lhs = jnp.concatenate([vt, ones_rows], axis=0)
        acc_ref[h] = alpha_ref[slot, h] * acc_ref[h] + jnp.dot(
            lhs, p_ref[slot, h], preferred_element_type=F32)

    def chunk_start(c):
        return pl.multiple_of(jnp.maximum(c, 0) * K_CHUNK, K_CHUNK)

    def one_step(t, slot, with_bias):
        k_new = chunk_start(t)
        k_old = chunk_start(t - 2)
        mb = sc_ref[pl.ds(k_new, K_CHUNK), :]
        near = jnp.minimum(i - t, 2) if with_bias else None
        for h in range(ATT_HEADS):
            accumulate(h, slot, k_old)
            probs(h, 1 - slot)
            logits(h, slot, k_new, mb, near)

    def two_steps(a, carry, *, with_bias):
        one_step(2 * a, 0, with_bias)
        one_step(2 * a + 1, 1, with_bias)
        return carry

    far_pairs = jnp.maximum(i - 1, 0) // 2
    n_pairs = n_chunks // 2
    lax.fori_loop(0, far_pairs, functools.partial(two_steps, with_bias=False), 0)
    lax.fori_loop(far_pairs, n_pairs, functools.partial(two_steps, with_bias=True), 0)

    @pl.when(n_chunks % 2 == 1)
    def _():
        one_step(i, 0, True)

    last = i % 2
    for h in range(ATT_HEADS):
        accumulate(h, 1 - last, chunk_start(i - 1))
        probs(h, last)
    for h in range(ATT_HEADS):
        accumulate(h, last, chunk_start(i))

    out_t = jnp.concatenate(
        [acc_ref[h, 0:ATT_HEAD_DIM, :] / acc_ref[h, ATT_HEAD_DIM:ATT_HEAD_DIM + 1, :]
         for h in range(ATT_HEADS)], axis=0)
    o_ref[0] = jnp.transpose(out_t).astype(BF16)


def _prompt_attention(rel_bias, q, iq, ikw, kb, vt, ikb, topk):
    bsz, s, _ = q.shape
    assert (s // K_CHUNK) * (K_CHUNK // BF16_ROWS // 4) <= BF16_EXACT_INT
    tile = lambda w: pl.BlockSpec((1, Q_TILE, w), lambda b, i: (b, i, 0))
    per_batch = lambda shape: pl.BlockSpec((1,) + shape, lambda b, i: (b, 0, 0),
                                           pipeline_mode=pl.Buffered(1))
    body = functools.partial(_pattn_body, topk=topk)
    return pl.pallas_call(
        body,
        grid=(bsz, s // Q_TILE),
        in_specs=[pl.BlockSpec(memory_space=pltpu.SMEM),
                  tile(ATT_WIDTH), tile(ATT_WIDTH), tile(LANE),
                  per_batch((s, ATT_WIDTH)), per_batch((ATT_WIDTH, s)), per_batch((s, LANE))],
        out_specs=tile(ATT_WIDTH),
        out_shape=jax.ShapeDtypeStruct((bsz, s, ATT_WIDTH), BF16),
        scratch_shapes=[
            pltpu.VMEM((s, Q_TILE), F32),
            pltpu.VMEM((s, Q_TILE), BF16),
            pltpu.VMEM((ATT_HEADS, Q_TILE, MXU_WIDTH), BF16),
            pltpu.VMEM((IDX_HEADS, Q_TILE, LANE), BF16),
            pltpu.VMEM((3, ATT_HEADS, K_CHUNK, Q_TILE), F32),
            pltpu.VMEM((2, ATT_HEADS, 1, Q_TILE), F32),
            pltpu.VMEM((2, ATT_HEADS, 1, Q_TILE), F32),
            pltpu.VMEM((ATT_HEADS, ACC_ROWS, Q_TILE), F32),
            pltpu.VMEM((2, ATT_HEADS, K_CHUNK, Q_TILE), F32),
            pltpu.VMEM((2, ATT_HEADS, K_CHUNK, Q_TILE), BF16),
        ],
        compiler_params=pltpu.CompilerParams(dimension_semantics=("arbitrary", "arbitrary"),
                                             vmem_limit_bytes=VMEM_LIMIT),
        name="prompt_attention",
    )(rel_bias, q, iq, ikw, kb, vt, ikb)


SCORE_PAGES = 64
ATTN_PAGES = 32
NEW_PAD = 16


def _head_sums(y, dec_seq):
    return jnp.concatenate(
        [jnp.sum(y[IDX_HEADS * t:IDX_HEADS * (t + 1)], axis=0, keepdims=True)
         for t in range(dec_seq)], axis=0)


def _sscore_body(pt_ref, iq_ref, w_ref, iknew_ref, *rest, dec_seq):
    del pt_ref
    pages = rest[:SCORE_PAGES]
    out_ref, new_ref = rest[SCORE_PAGES:]
    iq = iq_ref[0]
    w = w_ref[0]
    for m in range(SCORE_PAGES):
        s = jnp.dot(iq, pages[m][0].astype(BF16), preferred_element_type=F32)
        out_ref[0, :, m * PAGE_SIZE:(m + 1) * PAGE_SIZE] = _head_sums(jnp.maximum(s, 0.0) * w,
                                                                      dec_seq)

    @pl.when(pl.program_id(1) == 0)
    def _():
        s = jnp.dot(iq, iknew_ref[0], preferred_element_type=F32)
        new_sc = _head_sums(jnp.maximum(s, 0.0) * w, dec_seq)
        padded = jnp.concatenate(
            [new_sc, jnp.zeros((dec_seq, K_CHUNK - PAGE_SIZE), F32)], axis=1)
        t_idx = lax.broadcasted_iota(jnp.int32, (dec_seq, K_CHUNK), 0)
        k_idx = lax.broadcasted_iota(jnp.int32, (dec_seq, K_CHUNK), 1)
        new_ref[0] = jnp.where(k_idx <= t_idx, padded, -jnp.inf)


def _sample_scores(page_table, iq_rows, w_rows, iknew_t, cache_idx_t, dec_seq):
    nseq, n_pages = page_table.shape
    assert n_pages % SCORE_PAGES == 0
    steps = n_pages // SCORE_PAGES
    rows = dec_seq * IDX_HEADS

    def page_spec(m):
        return pl.BlockSpec((1, IDX_DIM, PAGE_SIZE),
                            lambda b, j, pt: (pt[b, j * SCORE_PAGES + m], 0, 0))

    per_seq = lambda shape: pl.BlockSpec((1,) + shape, lambda b, j, pt: (b, 0, 0))
    grid_spec = pltpu.PrefetchScalarGridSpec(
        num_scalar_prefetch=1,
        grid=(nseq, steps),
        in_specs=[per_seq((rows, IDX_DIM)), per_seq((rows, 1)), per_seq((IDX_DIM, PAGE_SIZE))]
                 + [page_spec(m) for m in range(SCORE_PAGES)],
        out_specs=(pl.BlockSpec((1, dec_seq, SCORE_PAGES * PAGE_SIZE),
                                lambda b, j, pt: (b, 0, j)),
                   per_seq((dec_seq, K_CHUNK))),
    )
    return pl.pallas_call(
        functools.partial(_sscore_body, dec_seq=dec_seq),
        grid_spec=grid_spec,
        out_shape=(jax.ShapeDtypeStruct((nseq, dec_seq, n_pages * PAGE_SIZE), F32),
                   jax.ShapeDtypeStruct((nseq, dec_seq, K_CHUNK), F32)),
        compiler_params=pltpu.CompilerParams(dimension_semantics=("arbitrary", "arbitrary"),
                                             vmem_limit_bytes=VMEM_LIMIT),
        name="sample_scores",
    )(page_table, iq_rows, w_rows, iknew_t, *([cache_idx_t] * SCORE_PAGES))


SAMPLE_TOPK_LANES = LANE


def _stopk_body(sc_in_ref, o_ref, sc_ref, *, dec_seq, past, topk):
    n_keys, lanes = sc_in_ref.shape
    sc_ref[...] = sc_in_ref[...]
    q = pl.program_id(0) * lanes + lax.broadcasted_iota(jnp.int32, (1, lanes), 1)
    n_allowed = (past + 1 + q % dec_seq).astype(F32)
    _select_topk(sc_ref, lanes, n_keys // K_CHUNK, n_allowed, topk, keys_axis=0)
    o_ref[...] = sc_ref[...]


def _sample_topk(scores_t, dec_seq, past, topk):
    n_keys, n_q = scores_t.shape
    lanes = min(SAMPLE_TOPK_LANES, n_q)
    spec = pl.BlockSpec((n_keys, lanes), lambda g: (0, g))
    return pl.pallas_call(
        functools.partial(_stopk_body, dec_seq=dec_seq, past=past, topk=topk),
        grid=(n_q // lanes,),
        in_specs=[spec],
        out_specs=spec,
        out_shape=jax.ShapeDtypeStruct((n_keys, n_q), F32),
        scratch_shapes=[pltpu.VMEM((n_keys, lanes), F32)],
        compiler_params=pltpu.CompilerParams(dimension_semantics=("arbitrary",),
                                             vmem_limit_bytes=VMEM_LIMIT),
        name="sample_topk",
    )(scores_t)


def _sattn_body(pt_ref, rb_ref, mask_ref, qp_ref, knew_ref, vnew_ref, *rest, dec_seq, past):
    del pt_ref
    kpages = rest[:ATTN_PAGES]
    vpages = rest[ATTN_PAGES:2 * ATTN_PAGES]
    o_ref = rest[2 * ATTN_PAGES]
    mb_ref, m_ref, l_ref, acc_ref = rest[2 * ATTN_PAGES + 1:]
    j = pl.program_id(1)
    n_steps = pl.num_programs(1)
    rows = dec_seq * ATT_HEADS
    row_t = lax.broadcasted_iota(jnp.int32, (rows, 1), 0) // ATT_HEADS
    row_h = lax.broadcasted_iota(jnp.int32, (rows, 1), 0) % ATT_HEADS

    @pl.when(j == 0)
    def _():
        for t in range(dec_seq):
            mb_ref[ATT_HEADS * t:ATT_HEADS * (t + 1), :] = jnp.broadcast_to(
                mask_ref[0, t:t + 1, :], (ATT_HEADS, past + K_CHUNK))
        m_ref[...] = jnp.full(m_ref.shape, -1e30, F32)
        l_ref[...] = jnp.zeros(l_ref.shape, F32)
        acc_ref[...] = jnp.zeros(acc_ref.shape, F32)

    qp = qp_ref[0]

    def near_bias(dist):
        bucket = _rel_bucket(dist)
        acc = jnp.zeros(dist.shape, F32)
        for bk in range(REL_BUCKETS - 1):
            per_row = jnp.zeros((rows, 1), F32)
            for h in range(ATT_HEADS):
                per_row = jnp.where(row_h == h, rb_ref[bk, h] - rb_ref[REL_BUCKETS - 1, h], per_row)
            acc = jnp.where(bucket == bk, per_row, acc)
        return acc * LOG2E

    def attend_pages(k_ts, v_ts, mb, last_bias):
        logit = jnp.concatenate(
            [jnp.dot(qp, k_t, preferred_element_type=F32) for k_t in k_ts], axis=1) + mb
        n = len(k_ts)
        if n > 1:
            last_bias = jnp.concatenate(
                [jnp.zeros((rows, (n - 1) * PAGE_SIZE), F32), last_bias], axis=1)
        logit = logit + last_bias
        m_old = m_ref[...]
        m_new = jnp.maximum(m_old, jnp.max(logit, axis=1, keepdims=True))
        alpha = jnp.exp2(m_old - m_new)
        p = jnp.exp2(logit - m_new).astype(BF16)
        l_ref[...] = alpha * l_ref[...] + jnp.sum(p.astype(F32), axis=1, keepdims=True)
        acc = alpha * acc_ref[...]
        for m, v_t in enumerate(v_ts):
            acc = acc + lax.dot_general(p[:, m * PAGE_SIZE:(m + 1) * PAGE_SIZE], v_t, NT_DIMS,
                                        preferred_element_type=F32)
        acc_ref[...] = acc
        m_ref[...] = m_new

    is_last = j == n_steps - 1
    off = lax.broadcasted_iota(jnp.int32, (rows, PAGE_SIZE), 1)
    k0 = pl.multiple_of(j * (ATTN_PAGES * PAGE_SIZE), ATTN_PAGES * PAGE_SIZE)
    last_bias = lax.cond(is_last, lambda: near_bias(PAGE_SIZE + row_t - off),
                         lambda: jnp.zeros((rows, PAGE_SIZE), F32))
    attend_pages([kp[0].astype(BF16) for kp in kpages], [vp[0].astype(BF16) for vp in vpages],
                 mb_ref[:, pl.ds(k0, ATTN_PAGES * PAGE_SIZE)], last_bias)

    @pl.when(is_last)
    def _():
        attend_pages([knew_ref[0]], [vnew_ref[0]], mb_ref[:, past:past + PAGE_SIZE],
                     near_bias(row_t - off))
        res = acc_ref[...] / l_ref[...]
        lane_h = lax.broadcasted_iota(jnp.int32, (rows, ATT_WIDTH), 1) // ATT_HEAD_DIM
        res = jnp.where(lane_h == row_h, res, 0.0)
        o_ref[0] = _head_sums(res, dec_seq).astype(BF16)


def _sample_attention(page_table, rel_bias, mask, qp_rows, knew_t, vnew_t, cache_k_t, cache_v_t,
                      dec_seq):
    nseq, n_pages = page_table.shape
    past = n_pages * PAGE_SIZE
    assert n_pages % ATTN_PAGES == 0
    steps = n_pages // ATTN_PAGES
    rows = dec_seq * ATT_HEADS

    def page_spec(m):
        return pl.BlockSpec((1, ATT_WIDTH, PAGE_SIZE),
                            lambda b, j, pt: (pt[b, j * ATTN_PAGES + m], 0, 0))

    per_seq = lambda shape: pl.BlockSpec((1,) + shape, lambda b, j, pt: (b, 0, 0))
    grid_spec = pltpu.PrefetchScalarGridSpec(
        num_scalar_prefetch=1,
        grid=(nseq, steps),
        in_specs=[pl.BlockSpec(memory_space=pltpu.SMEM),
                  per_seq((dec_seq, past + K_CHUNK)), per_seq((rows, ATT_WIDTH)),
                  per_seq((ATT_WIDTH, PAGE_SIZE)), per_seq((ATT_WIDTH, PAGE_SIZE))]
                 + [page_spec(m) for m in range(ATTN_PAGES)] * 2,
        out_specs=per_seq((dec_seq, ATT_WIDTH)),
        scratch_shapes=[
            pltpu.VMEM((rows, past + K_CHUNK), F32),
            pltpu.VMEM((rows, 1), F32),
            pltpu.VMEM((rows, 1), F32),
            pltpu.VMEM((rows, ATT_WIDTH), F32),
        ],
    )
    return pl.pallas_call(
        functools.partial(_sattn_body, dec_seq=dec_seq, past=past),
        grid_spec=grid_spec,
        out_shape=jax.ShapeDtypeStruct((nseq, dec_seq, ATT_WIDTH), BF16),
        compiler_params=pltpu.CompilerParams(dimension_semantics=("arbitrary", "arbitrary"),
                                             vmem_limit_bytes=VMEM_LIMIT),
        name="sample_attention",
    )(page_table, rel_bias, mask, qp_rows, knew_t, vnew_t,
      *([cache_k_t] * ATTN_PAGES), *([cache_v_t] * ATTN_PAGES))


FF_CHUNK = 256


def _post_body(att_ref, gla_ref, x_ref, p_ref, wo_ref, gpm_ref, gpf_ref, wg_ref, wu_ref, wd_ref,
               gpo_ref, wpg_ref, wpp_ref, o_ref, *, d_ff):
    wo = wo_ref
    mixed = jnp.dot(att_ref[...], wo[0:ATT_WIDTH, :], preferred_element_type=F32)
    mixed = mixed + jnp.dot(gla_ref[...], wo[ATT_WIDTH:ATT_WIDTH + GLA_WIDTH, :],
                            preferred_element_type=F32)
    h = x_ref[...] + _rms(mixed, gpm_ref[...])
    f = _rms(h, gpf_ref[...]).astype(BF16)
    ff = jnp.zeros(h.shape, F32)
    for c in range(d_ff // FF_CHUNK):
        cs = slice(c * FF_CHUNK, (c + 1) * FF_CHUNK)
        gate = jnp.dot(f, wg_ref[:, cs], preferred_element_type=F32)
        up = jnp.dot(f, wu_ref[:, cs], preferred_element_type=F32)
        act = (gate * jax.nn.sigmoid(gate) * up).astype(BF16)
        ff = ff + jnp.dot(act, wd_ref[cs, :], preferred_element_type=F32)
    h = h + _rms(ff, gpo_ref[...])
    gate = jax.nn.sigmoid(jnp.dot(h.astype(BF16), wpg_ref[...], preferred_element_type=F32))
    ple = jnp.dot(p_ref[...].astype(BF16), wpp_ref[...], preferred_element_type=F32)
    o_ref[...] = h + gate * ple


def _post(att, gla, x, p, wo, gpm, gpf, wg, wu, wd, gpo, wpg, wpp, tm):
    t = x.shape[0]
    d_ff = wg.shape[1]
    row = lambda w: pl.BlockSpec((tm, w), lambda i: (i, 0))
    const = lambda a: pl.BlockSpec(a.shape, lambda i: (0, 0), pipeline_mode=pl.Buffered(1))
    return pl.pallas_call(
        functools.partial(_post_body, d_ff=d_ff),
        grid=(t // tm,),
        in_specs=[row(ATT_WIDTH), row(GLA_WIDTH), row(D_MODEL), row(p.shape[1]),
                  const(wo), const(gpm), const(gpf), const(wg), const(wu), const(wd),
                  const(gpo), const(wpg), const(wpp)],
        out_specs=row(D_MODEL),
        out_shape=jax.ShapeDtypeStruct((t, D_MODEL), F32),
        compiler_params=pltpu.CompilerParams(dimension_semantics=("arbitrary",),
                                             vmem_limit_bytes=VMEM_LIMIT),
        name="post",
    )(att, gla, x, p, wo, gpm, gpf, wg, wu, wd, gpo, wpg, wpp)


def _pack_w_in(w_in):
    offs = [0]
    for wdt in (ATT_WIDTH, ATT_WIDTH, ATT_WIDTH, IDX_HEADS * IDX_DIM, IDX_DIM, IDX_HEADS,
                GLA_KEY_WIDTH, GLA_KEY_WIDTH, GLA_WIDTH, GLA_GATE_RANK, GLA_WIDTH):
        offs.append(offs[-1] + wdt)
    aq, ak, av, iq, ik, iw, gq, gk, gv, glr, gout = [w_in[:, offs[n]:offs[n + 1]] for n in range(11)]
    d = w_in.shape[0]
    zeros = lambda n: jnp.zeros((d, n), w_in.dtype)
    packed = jnp.concatenate([
        aq * (ATT_HEAD_DIM ** -0.5 * LOG2E), ak, av, iq * (IDX_DIM ** -0.5),
        ik, iw, zeros(LANE - IDX_DIM - IDX_HEADS),
        gq * (GLA_DK ** -0.5), gk, gv, glr, zeros(LANE - GLA_GATE_RANK), gout], axis=1)
    return packed.astype(BF16)


def _state_to_t(s):
    n = s.shape[0]
    return jnp.transpose(s, (0, 3, 1, 2)).reshape(n, GLA_DV, GLA_KEY_WIDTH)


def _state_from_t(st):
    n = st.shape[0]
    return jnp.transpose(st.reshape(n, GLA_DV, GLA_HEADS, GLA_DK), (0, 2, 3, 1))


def _layer(x_prompt, x_sample, p_prompt, p_sample, cache_k, cache_v, cache_idx_k, state_gla,
           page_table, rel_bias, g_pre_mix, w_in, w_gate_up, b_gate, g_gla_out, w_out,
           g_post_mix, g_pre_ffn, w_ff_gate, w_ff_up, w_ff_down, g_post_ffn, w_ple_gate,
           w_ple_proj):
    bsz, seq, d = x_prompt.shape
    nseq, dec_seq, _ = x_sample.shape
    n_pool = cache_k.shape[0]
    past = page_table.shape[1] * PAGE_SIZE
    row2 = lambda g: g.reshape(1, -1)

    w_packed = _pack_w_in(w_in)
    w_t = jnp.transpose(w_packed[:, SEG_K[0]:SEG_IKW[0] + IDX_DIM])
    w_t = jnp.concatenate([w_t[0:2 * ATT_WIDTH], w_t[SEG_IKW[0] - SEG_K[0]:]], axis=0)
    wgu = jnp.concatenate(
        [w_gate_up, jnp.zeros((LANE - GLA_GATE_RANK, GLA_KEY_WIDTH), w_gate_up.dtype)],
        axis=0).astype(BF16)
    post_w = (w_out.astype(BF16), row2(g_post_mix), row2(g_pre_ffn), w_ff_gate.astype(BF16),
              w_ff_up.astype(BF16), w_ff_down.astype(BF16), row2(g_post_ffn),
              w_ple_gate.astype(BF16), w_ple_proj.astype(BF16))
    gla_w = (wgu, row2(b_gate), row2(g_gla_out))

    tp = bsz * seq
    q, kb, iq, ikw, ikb, gin, k_t, v_t, vtb, ik_t = _inproj(
        x_prompt.reshape(tp, d), row2(g_pre_mix), w_packed, w_t, min(512, seq), seq)
    r3 = lambda a: a.reshape(bsz, seq, a.shape[-1])
    att = _prompt_attention(rel_bias, r3(q), r3(iq), r3(ikw), r3(kb), vtb, r3(ikb),
                            min(TOPK_MAX, seq // 4))
    gla_chunk = min(128, seq)
    gla_o, st_p = _gla(r3(gin), jnp.zeros((bsz, GLA_DV, GLA_KEY_WIDTH), F32), *gla_w,
                       tile=min(512, seq), chunk=gla_chunk, n_valid=gla_chunk)
    y_p = _post(att.reshape(tp, ATT_WIDTH), gla_o.reshape(tp, GLA_WIDTH), x_prompt.reshape(tp, d),
                p_prompt.reshape(tp, -1), *post_w, min(512, tp))
    heads_last = lambda a: jnp.transpose(
        a.reshape(bsz, ATT_HEADS, ATT_HEAD_DIM, seq), (0, 3, 1, 2))
    outs_p = (y_p.reshape(bsz, seq, d), heads_last(k_t), heads_last(v_t),
              jnp.swapaxes(ik_t, 1, 2), _state_from_t(st_p))

    ts = nseq * dec_seq
    q, kb, iq, ikw, ikb, gin, kf, vf, vb = _inproj(
        x_sample.reshape(ts, d), row2(g_pre_mix), w_packed, None, min(512, ts), ts)
    rows = dec_seq * ATT_HEADS
    iq_rows = iq.reshape(nseq, rows, IDX_DIM)
    w_rows = ikw[:, IDX_DIM:IDX_DIM + IDX_HEADS].reshape(nseq, rows, 1)
    head_of_col = jnp.arange(ATT_WIDTH, dtype=jnp.int32) // ATT_HEAD_DIM
    head_mask = head_of_col[None, :] == jnp.arange(ATT_HEADS, dtype=jnp.int32)[:, None]
    qp_rows = jnp.where(head_mask[None, None], q.reshape(nseq, dec_seq, 1, ATT_WIDTH),
                        jnp.zeros((), BF16)).reshape(nseq, rows, ATT_WIDTH)
    per_seq = lambda a: a.reshape(nseq, dec_seq, a.shape[-1])
    new_page_t = lambda a: jnp.pad(jnp.swapaxes(per_seq(a), 1, 2),
                                   ((0, 0), (0, 0), (0, PAGE_SIZE - dec_seq)))
    cache_idx_t = jnp.swapaxes(cache_idx_k, 1, 2)
    cache_k_t = jnp.transpose(cache_k, (0, 2, 3, 1)).reshape(n_pool, ATT_WIDTH, PAGE_SIZE)
    cache_v_t = jnp.transpose(cache_v, (0, 2, 3, 1)).reshape(n_pool, ATT_WIDTH, PAGE_SIZE)
    sc_past, sc_new = _sample_scores(page_table, iq_rows, w_rows, new_page_t(ikb[:, :IDX_DIM]),
                                     cache_idx_t, dec_seq)
    scores_t = jnp.transpose(jnp.concatenate([sc_past, sc_new], axis=2).reshape(ts, -1))
    mask_t = _sample_topk(scores_t, dec_seq, past, min(TOPK_MAX, (past + dec_seq) // 4))
    mask = jnp.transpose(mask_t).reshape(nseq, dec_seq, -1)
    att_s = _sample_attention(page_table, rel_bias, mask, qp_rows, new_page_t(kb),
                              new_page_t(vb), cache_k_t, cache_v_t, dec_seq)
    gin_s = jnp.pad(per_seq(gin), ((0, 0), (0, NEW_PAD - dec_seq), (0, 0)))
    gla_s, st_s = _gla(gin_s, _state_to_t(state_gla), *gla_w,
                       tile=NEW_PAD, chunk=NEW_PAD, n_valid=dec_seq)
    y_s = _post(att_s.reshape(ts, ATT_WIDTH), gla_s[:, :dec_seq].reshape(ts, GLA_WIDTH),
                x_sample.reshape(ts, d), p_sample.reshape(ts, -1), *post_w, min(256, ts))
    outs_s = (y_s.reshape(nseq, dec_seq, d),
              kf.reshape(nseq, dec_seq, ATT_HEADS, ATT_HEAD_DIM),
              vf.reshape(nseq, dec_seq, ATT_HEADS, ATT_HEAD_DIM),
              ikw[:, :IDX_DIM].reshape(nseq, dec_seq, IDX_DIM),
              _state_from_t(st_s))
    return outs_p, outs_s


def kernel(x_prompt, x_sample, p_prompt, p_sample, cache_k, cache_v, cache_idx_k, state_gla,
           page_table, rel_bias, g_pre_mix, w_in, w_gate_up, b_gate, g_gla_out, w_out,
           g_post_mix, g_pre_ffn, w_ff_gate, w_ff_up, w_ff_down, g_post_ffn, w_ple_gate,
           w_ple_proj):
    depth = w_in.shape[0]
    h_p, h_s = x_prompt, x_sample
    per_layer = []
    for i in range(depth):
        outs_p, outs_s = _layer(
            h_p, h_s, p_prompt[i], p_sample[i], cache_k[i], cache_v[i], cache_idx_k[i],
            state_gla[i], page_table, rel_bias, g_pre_mix[i], w_in[i], w_gate_up[i], b_gate[i],
            g_gla_out[i], w_out[i], g_post_mix[i], g_pre_ffn[i], w_ff_gate[i], w_ff_up[i],
            w_ff_down[i], g_post_ffn[i], w_ple_gate[i], w_ple_proj[i])
        h_p, h_s = outs_p[0], outs_s[0]
        per_layer.append(outs_p[1:] + outs_s[1:])
    stacked = [jnp.stack([lay[n] for lay in per_layer]) for n in range(8)]
    return (h_p, h_s, stacked[0], stacked[1], stacked[2], stacked[3],
            stacked[4], stacked[5], stacked[6], stacked[7])
```
